```python
import math
import jax
import jax.numpy as jnp
from jax import lax
import numpy as np

D_MODEL = 1024
BATCH = 2
SEQ = 16384
DEPTH = 4

N_MIXERS = 3
MOBA_HEADS = 8
MOBA_HEAD_DIM = D_MODEL // MOBA_HEADS
MOBA_BLOCK = 256
MOBA_TOPK = 3
MOBA_Q_BLOCK = 64
SCONV_WIDTH = 3
GDN_HEADS = 8
GDN_HEAD_DIM = D_MODEL // GDN_HEADS
GDN_CONV_WIDTH = 4
GDN_CHUNK = 64
FFN_DIM = 2816
FFN_CONV_WIDTH = 3

NORM_EPS = 1e-6
NEG_INF = -1e30

N_MOBA = (DEPTH + 2) // 3
N_SCONV = (DEPTH + 1) // 3
N_GDN = DEPTH // 3

kernel_name = 'hybrid_moba_shortconv_gdn_convffn'


def rmsnorm(x, g):
    xf = x.astype(jnp.float32)
    y = xf * lax.rsqrt(jnp.mean(xf * xf, axis=-1, keepdims=True) + NORM_EPS)
    return (y * g.astype(jnp.float32)).astype(x.dtype)


def l2norm(x):
    return x * lax.rsqrt(jnp.sum(x * x, axis=-1, keepdims=True) + NORM_EPS)


def causal_dwconv(x, w):
    width = w.shape[0]
    seq = x.shape[1]
    xp = jnp.pad(x, ((0, 0), (width - 1, 0), (0, 0)))
    out = w[width - 1] * x
    for k in range(width - 1):
        out = out + w[k] * xp[:, k:k + seq]
    return out


def alibi_slopes(n_heads):
    return jnp.exp2(-8.0 * jnp.arange(1, n_heads + 1, dtype=jnp.float32) / n_heads)


def moba_attention(h, w_qkv, w_o):
    bsz, seq, _ = h.shape
    nh, hd, blk, qb = MOBA_HEADS, MOBA_HEAD_DIM, MOBA_BLOCK, MOBA_Q_BLOCK
    qkv = (h @ w_qkv).reshape(bsz, seq, 3, nh, hd)
    q, k, v = [jnp.swapaxes(qkv[:, :, i], 1, 2) for i in range(3)]
    n_blk = -(-seq // blk)
    pad = n_blk * blk - seq
    k_blocks = jnp.pad(k, ((0, 0), (0, 0), (0, pad), (0, 0))).reshape(bsz, nh, n_blk, blk, hd)
    v_blocks = jnp.pad(v, ((0, 0), (0, 0), (0, pad), (0, 0))).reshape(bsz, nh, n_blk, blk, hd)
    k_mean = jnp.mean(k_blocks.astype(jnp.float32), axis=3)
    n_top = min(MOBA_TOPK, n_blk)
    slopes = alibi_slopes(nh)
    scale = hd ** -0.5
    n_qb = seq // qb
    q_chunks = jnp.moveaxis(q.reshape(bsz, nh, n_qb, qb, hd), 2, 0)
    b_idx = jnp.arange(bsz)[:, None, None, None]
    h_idx = jnp.arange(nh)[None, :, None, None]
    blk_ids = jnp.arange(n_blk)
    offs = jnp.arange(blk)

    def attend(args):
        c, qc = args
        pos_q = c * qb + jnp.arange(qb)
        own = (c * qb) // blk
        gate = jnp.einsum('bhqd,bhnd->bhqn', qc.astype(jnp.float32), k_mean)
        gate = jnp.where(blk_ids < own, gate, NEG_INF)
        _, sel = lax.top_k(gate, n_top)
        sel_ok = jnp.arange(n_top) < own
        k_sel = k_blocks[b_idx, h_idx, sel]
        v_sel = v_blocks[b_idx, h_idx, sel]
        k_own = lax.dynamic_index_in_dim(k_blocks, own, axis=2, keepdims=False)
        v_own = lax.dynamic_index_in_dim(v_blocks, own, axis=2, keepdims=False)
        s_sel = jnp.einsum('bhqd,bhqjkd->bhqjk', qc, k_sel).astype(jnp.float32) * scale
        dist_sel = (pos_q[:, None, None] - (sel[..., None] * blk + offs)).astype(jnp.float32)
        s_sel = s_sel - slopes[:, None, None, None] * dist_sel
        s_sel = jnp.where(sel_ok[:, None], s_sel, NEG_INF)
        dist_own = pos_q[:, None] - (own * blk + offs)[None, :]
        s_own = jnp.einsum('bhqd,bhkd->bhqk', qc, k_own).astype(jnp.float32) * scale
        s_own = s_own - slopes[:, None, None] * dist_own.astype(jnp.float32)
        s_own = jnp.where(dist_own >= 0, s_own, NEG_INF)
        scores = jnp.concatenate([s_sel.reshape(bsz, nh, qb, n_top * blk), s_own], axis=-1)
        p = jax.nn.softmax(scores, axis=-1).astype(v.dtype)
        p_sel = p[..., :n_top * blk].reshape(bsz, nh, qb, n_top, blk)
        p_own = p[..., n_top * blk:]
        return (jnp.einsum('bhqjk,bhqjkd->bhqd', p_sel, v_sel)
                + jnp.einsum('bhqk,bhkd->bhqd', p_own, v_own))

    o = lax.map(attend, (jnp.arange(n_qb), q_chunks))
    o = jnp.moveaxis(o, 0, 2).reshape(bsz, nh, seq, hd)
    o = jnp.swapaxes(o, 1, 2).reshape(bsz, seq, nh * hd)
    return o @ w_o


def short_conv_mixer(h, w_in, conv_w, w_out):
    d = D_MODEL
    proj = h @ w_in
    b_gate, c_gate, xv = proj[..., :d], proj[..., d:2 * d], proj[..., 2 * d:]
    return (b_gate * causal_dwconv(c_gate * xv, conv_w)) @ w_out


def gated_deltanet(h, w_in, conv_w, a_log, dt_bias, norm_w, w_o):
    bsz, seq, _ = h.shape
    nh, dh, cs = GDN_HEADS, GDN_HEAD_DIM, GDN_CHUNK
    d = nh * dh
    f32 = jnp.float32
    proj = h @ w_in
    qkv = jax.nn.silu(causal_dwconv(proj[..., :3 * d], conv_w))
    z = proj[..., 3 * d:4 * d].reshape(bsz, seq, nh, dh).astype(f32)
    b_logit = proj[..., 4 * d:4 * d + nh].astype(f32)
    a_in = proj[..., 4 * d + nh:].astype(f32)
    q, k, v = [qkv[..., i * d:(i + 1) * d].reshape(bsz, seq, nh, dh).astype(f32) for i in range(3)]
    q = l2norm(q) * (dh ** -0.5)
    k = l2norm(k)
    beta = jax.nn.sigmoid(b_logit)
    g = -jnp.exp(a_log.astype(f32)) * jax.nn.softplus(a_in + dt_bias.astype(f32))
    n_ch = seq // cs

    def chunks(t):
        t = jnp.moveaxis(t, 2, 1)
        return t.reshape((bsz, nh, n_ch, cs) + t.shape[3:])

    q, k, v, beta, g = chunks(q), chunks(k), chunks(v), chunks(beta), chunks(g)
    gc = jnp.cumsum(g, axis=-1)
    idx = jnp.arange(cs)
    incl = idx[:, None] >= idx[None, :]
    strict = idx[:, None] > idx[None, :]
    diff = gc[..., :, None] - gc[..., None, :]
    decay = jnp.where(incl, jnp.exp(jnp.where(incl, diff, 0.0)), 0.0)
    k_beta = k * beta[..., None]
    v_beta = v * beta[..., None]
    a_mat = jnp.where(strict, jnp.einsum('bhnid,bhnjd->bhnij', k_beta, k) * decay, 0.0)
    eye = jnp.eye(cs, dtype=f32)
    t_mat = lax.linalg.triangular_solve(a_mat + eye, jnp.broadcast_to(eye, a_mat.shape),
                                        left_side=True, lower=True, unit_diagonal=True)
    u = t_mat @ v_beta
    w = t_mat @ (k_beta * jnp.exp(gc)[..., None])
    attn = jnp.where(incl, jnp.einsum('bhnid,bhnjd->bhnij', q, k) * decay, 0.0)
    q_dec = q * jnp.exp(gc)[..., None]
    g_last = gc[..., -1]
    k_dec = k * jnp.exp(g_last[..., None] - gc)[..., None]
    state_decay = jnp.exp(g_last)

    def step(state, inp):
        q_i, k_i, u_i, w_i, a_i, sd_i = inp
        v_new = u_i - w_i @ state
        o_i = q_i @ state + a_i @ v_new
        state = state * sd_i[..., None, None] + jnp.einsum('bhck,bhcv->bhkv', k_i, v_new)
        return state, o_i

    xs = (jnp.moveaxis(q_dec, 2, 0), jnp.moveaxis(k_dec, 2, 0), jnp.moveaxis(u, 2, 0),
          jnp.moveaxis(w, 2, 0), jnp.moveaxis(attn, 2, 0), jnp.moveaxis(state_decay, 2, 0))
    state0 = jnp.zeros((bsz, nh, dh, dh), f32)
    _, o = lax.scan(step, state0, xs)
    o = jnp.moveaxis(o, 0, 2).reshape(bsz, nh, seq, dh)
    o = jnp.swapaxes(o, 1, 2)
    o = o * lax.rsqrt(jnp.mean(o * o, axis=-1, keepdims=True) + NORM_EPS) * norm_w.astype(f32)
    o = o * jax.nn.silu(z)
    return o.reshape(bsz, seq, d).astype(h.dtype) @ w_o


def conv_ffn(h, w_up, conv_w, w_down):
    u = causal_dwconv(h @ w_up, conv_w)
    gate, up = u[..., :FFN_DIM], u[..., FFN_DIM:]
    return (jax.nn.silu(gate) * up) @ w_down


def setup_inputs(seed: int = 0) -> dict:
    key = jax.random.key(seed)
    ks = jax.random.split(key, 24)
    f32 = jnp.float32
    d = D_MODEL

    def dense(k, shape):
        return jax.random.normal(k, shape, f32) * (shape[-2] ** -0.5)

    def gain(k, shape):
        return 1.0 + 0.02 * jax.random.normal(k, shape, f32)

    def conv(k, shape):
        return jax.random.normal(k, shape, f32) * (shape[-2] ** -0.5)

    dt = jnp.exp(jax.random.uniform(ks[14], (N_GDN, GDN_HEADS), f32, math.log(1e-3), math.log(1e-1)))
    return {
        'x': jax.random.normal(ks[0], (BATCH, SEQ, d), f32),
        'mix_norm': gain(ks[1], (DEPTH, d)),
        'ffn_norm': gain(ks[2], (DEPTH, d)),
        'final_norm': gain(ks[3], (d,)),
        'moba_w_qkv': dense(ks[4], (N_MOBA, d, 3 * d)),
        'moba_w_o': dense(ks[5], (N_MOBA, d, d)),
        'sconv_w_in': dense(ks[6], (N_SCONV, d, 3 * d)),
        'sconv_conv': conv(ks[7], (N_SCONV, SCONV_WIDTH, d)),
        'sconv_w_out': dense(ks[8], (N_SCONV, d, d)),
        'gdn_w_in': dense(ks[9], (N_GDN, d, 4 * d + 2 * GDN_HEADS)),
        'gdn_conv': conv(ks[10], (N_GDN, GDN_CONV_WIDTH, 3 * d)),
        'gdn_a_log': jnp.log(jax.random.uniform(ks[11], (N_GDN, GDN_HEADS), f32, 1.0, 16.0)),
        'gdn_dt_bias': dt + jnp.log(-jnp.expm1(-dt)),
        'gdn_norm': gain(ks[12], (N_GDN, GDN_HEAD_DIM)),
        'gdn_w_o': dense(ks[13], (N_GDN, d, d)),
        'ffn_w_up': dense(ks[15], (DEPTH, d, 2 * FFN_DIM)),
        'ffn_conv': conv(ks[16], (DEPTH, FFN_CONV_WIDTH, 2 * FFN_DIM)),
        'ffn_w_down': dense(ks[17], (DEPTH, FFN_DIM, d)),
    }


def reference(x, mix_norm, ffn_norm, final_norm, moba_w_qkv, moba_w_o, sconv_w_in, sconv_conv,
              sconv_w_out, gdn_w_in, gdn_conv, gdn_a_log, gdn_dt_bias, gdn_norm, gdn_w_o,
              ffn_w_up, ffn_conv, ffn_w_down):
    for i in range(DEPTH):
        kind, j = i % N_MIXERS, i // N_MIXERS
        h = rmsnorm(x, mix_norm[i])
        if kind == 0:
            h = moba_attention(h, moba_w_qkv[j], moba_w_o[j])
        elif kind == 1:
            h = short_conv_mixer(h, sconv_w_in[j], sconv_conv[j], sconv_w_out[j])
        else:
            h = gated_deltanet(h, gdn_w_in[j], gdn_conv[j], gdn_a_log[j], gdn_dt_bias[j],
                               gdn_norm[j], gdn_w_o[j])
        x = x + h
        x = x + conv_ffn(rmsnorm(x, ffn_norm[i]), ffn_w_up[i], ffn_conv[i], ffn_w_down[i])
    return rmsnorm(x, final_norm)
```

```python
import functools

import jax
import jax.numpy as jnp
from jax import lax
from jax.experimental import pallas as pl
from jax.experimental.pallas import tpu as pltpu

F32 = jnp.float32
BF16 = jnp.bfloat16

N_MIXERS = 3
MOBA_HEADS = 8
MOBA_BLOCK = 256
MOBA_TOPK = 3
GDN_HEADS = 8
GDN_CHUNK = 64
NORM_EPS = 1e-6
NEG_INF = -1e30

LANES = 128
SUBLANES = 8
VMEM_LIMIT = 56 * 1024 * 1024


def _params(*sem):
    return pltpu.CompilerParams(dimension_semantics=sem, vmem_limit_bytes=VMEM_LIMIT)


def _rmsnorm(x, g):
    return x * lax.rsqrt(jnp.mean(x * x, axis=-1, keepdims=True) + NORM_EPS) * g


def _silu(x):
    return x * jax.nn.sigmoid(x)


def _resident(shape):
    return pl.BlockSpec(shape, lambda *_: (0,) * len(shape))


def _norm_matmul_kernel(x_ref, g_ref, w_ref, o_ref, *km_ref, km_cols, km_rows):
    hn = _rmsnorm(x_ref[...], g_ref[...]).astype(BF16)
    y = jnp.dot(hn, w_ref[...], preferred_element_type=F32)
    o_ref[...] = y.astype(o_ref.dtype)
    if km_ref:
        lo, hi = km_cols
        yk = y[:, lo:hi]
        tm = yk.shape[0]
        km = jnp.mean(yk.reshape(tm // km_rows, km_rows, hi - lo), axis=1)
        km_ref[0][...] = km[:, None, :]


def norm_matmul(x2, g, w, out_dtype, tm, km_cols=None, km_rows=None):
    t, d = x2.shape
    n = w.shape[1]
    out_shape = [jax.ShapeDtypeStruct((t, n), out_dtype)]
    out_specs = [pl.BlockSpec((tm, n), lambda i: (i, 0))]
    if km_cols is not None:
        nk = km_cols[1] - km_cols[0]
        out_shape.append(jax.ShapeDtypeStruct((t // km_rows, 1, nk), F32))
        out_specs.append(pl.BlockSpec((tm // km_rows, 1, nk), lambda i: (i, 0, 0)))
    res = pl.pallas_call(
        functools.partial(_norm_matmul_kernel, km_cols=km_cols, km_rows=km_rows),
        grid=(t // tm,),
        in_specs=[pl.BlockSpec((tm, d), lambda i: (i, 0)),
                  _resident((1, d)),
                  _resident((d, n))],
        out_specs=out_specs,
        out_shape=out_shape,
        compiler_params=_params("parallel"),
        name="norm_matmul",
    )(x2, g.reshape(1, d), w)
    return res if km_cols is not None else res[0]


def _proj_residual_kernel(x_ref, o_ref, w_ref, out_ref):
    out_ref[...] = x_ref[...] + jnp.dot(o_ref[...], w_ref[...], preferred_element_type=F32)


def proj_residual(x2, o2, w, tm):
    t, d = x2.shape
    return pl.pallas_call(
        _proj_residual_kernel,
        grid=(t // tm,),
        in_specs=[pl.BlockSpec((tm, d), lambda i: (i, 0)),
                  pl.BlockSpec((tm, o2.shape[1]), lambda i: (i, 0)),
                  _resident(w.shape)],
        out_specs=pl.BlockSpec((tm, d), lambda i: (i, 0)),
        out_shape=jax.ShapeDtypeStruct((t, d), F32),
        compiler_params=_params("parallel"),
        name="proj_residual",
    )(x2, o2, w)


def _causal_conv_rows(buf_ref, tail_ref, cur, cw, width, tm):
    buf_ref[0:SUBLANES, :] = tail_ref[...]
    buf_ref[SUBLANES:SUBLANES + tm, :] = cur
    out = cw[width - 1:width, :] * cur
    for k in range(width - 1):
        off = SUBLANES - (width - 1) + k
        out = out + cw[k:k + 1, :] * buf_ref[off:off + tm, :]
    tail_ref[...] = cur[tm - SUBLANES:tm, :]
    return out


def _ffn_kernel(x_ref, g_ref, wup_ref, cw_ref, wdn_ref, *rest, ffn, tf, width, final):
    if final:
        fn_ref, out_ref, gbuf, ubuf, tail_ref, acc_ref = rest
    else:
        out_ref, gbuf, ubuf, tail_ref, acc_ref = rest
    tm = x_ref.shape[0]

    @pl.when(pl.program_id(1) == 0)
    def _():
        tail_ref[...] = jnp.zeros(tail_ref.shape, F32)

    x = x_ref[...]
    hn = _rmsnorm(x, g_ref[...]).astype(BF16)
    for c in range(ffn // tf):
        lo = c * tf
        hi = ffn + lo
        gate = jnp.dot(hn, wup_ref[:, lo:lo + tf], preferred_element_type=F32)
        up = jnp.dot(hn, wup_ref[:, hi:hi + tf], preferred_element_type=F32)
        gate = _causal_conv_rows(gbuf, tail_ref.at[:, lo:lo + tf], gate,
                                 cw_ref[:, lo:lo + tf], width, tm)
        up = _causal_conv_rows(ubuf, tail_ref.at[:, hi:hi + tf], up,
                               cw_ref[:, hi:hi + tf], width, tm)
        act = (_silu(gate) * up).astype(BF16)
        y = jnp.dot(act, wdn_ref[lo:lo + tf, :], preferred_element_type=F32)
        if c == 0:
            acc_ref[...] = y
        else:
            acc_ref[...] += y
    res = x + acc_ref[...]
    if final:
        res = _rmsnorm(res, fn_ref[...])
    out_ref[...] = res


def conv_ffn_layer(x, g, w_up, conv_w, w_down, final_g, tm, tf):
    b, s, d = x.shape
    ffn = w_down.shape[0]
    width = conv_w.shape[0]
    final = final_g is not None
    row = pl.BlockSpec((None, tm, d), lambda bi, i: (bi, i, 0))
    in_specs = [row, _resident((1, d)), _resident(w_up.shape), _resident(conv_w.shape),
                _resident(w_down.shape)]
    args = [x, g.reshape(1, d), w_up, conv_w, w_down]
    if final:
        in_specs.append(_resident((1, d)))
        args.append(final_g.reshape(1, d))
    return pl.pallas_call(
        functools.partial(_ffn_kernel, ffn=ffn, tf=tf, width=width, final=final),
        grid=(b, s // tm),
        in_specs=in_specs,
        out_specs=row,
        out_shape=jax.ShapeDtypeStruct((b, s, d), F32),
        scratch_shapes=[pltpu.VMEM((tm + SUBLANES, tf), F32),
                        pltpu.VMEM((tm + SUBLANES, tf), F32),
                        pltpu.VMEM((SUBLANES, 2 * ffn), F32),
                        pltpu.VMEM((tm, d), F32)],
        compiler_params=_params("parallel", "arbitrary"),
        name="conv_ffn",
    )(*args)


def _sconv_kernel(x_ref, g_ref, win_ref, cw_ref, wout_ref, out_ref, cbuf, tail_ref, *, width):
    tm, d = x_ref.shape

    @pl.when(pl.program_id(1) == 0)
    def _():
        tail_ref[...] = jnp.zeros(tail_ref.shape, F32)

    x = x_ref[...]
    hn = _rmsnorm(x, g_ref[...]).astype(BF16)
    b_gate = jnp.dot(hn, win_ref[:, 0:d], preferred_element_type=F32)
    c_gate = jnp.dot(hn, win_ref[:, d:2 * d], preferred_element_type=F32)
    xv = jnp.dot(hn, win_ref[:, 2 * d:3 * d], preferred_element_type=F32)
    y = b_gate * _causal_conv_rows(cbuf, tail_ref, c_gate * xv, cw_ref[...], width, tm)
    out_ref[...] = x + jnp.dot(y.astype(BF16), wout_ref[...], preferred_element_type=F32)


def short_conv_layer(x, g, w_in, conv_w, w_out, tm):
    b, s, d = x.shape
    row = pl.BlockSpec((None, tm, d), lambda bi, i: (bi, i, 0))
    return pl.pallas_call(
        functools.partial(_sconv_kernel, width=conv_w.shape[0]),
        grid=(b, s // tm),
        in_specs=[row, _resident((1, d)), _resident(w_in.shape), _resident(conv_w.shape),
                  _resident(w_out.shape)],
        out_specs=row,
        out_shape=jax.ShapeDtypeStruct((b, s, d), F32),
        scratch_shapes=[pltpu.VMEM((tm + SUBLANES, d), F32), pltpu.VMEM((SUBLANES, d), F32)],
        compiler_params=_params("parallel", "arbitrary"),
        name="short_conv",
    )(x, g.reshape(1, d), w_in, conv_w, w_out)


def _moba_kernel(slopes_ref, q_ref, k_ref, v_ref, km_ref, o_ref, m_ref, l_ref, acc_ref,
                 *, blk, n_top, scale):
    h = pl.program_id(1)
    qi = pl.program_id(2)
    tq, hd = q_ref.shape
    n_blk = km_ref.shape[0]
    slope = slopes_ref[h]
    q = q_ref[...]

    gate = lax.dot_general(q, km_ref[...].astype(BF16), (((1,), (1,)), ((), ())),
                           preferred_element_type=F32)
    blk_ids = lax.broadcasted_iota(jnp.int32, (tq, n_blk), 1)
    gate = jnp.where(blk_ids < qi, gate, NEG_INF)
    sels = []
    for _ in range(n_top):
        best = jnp.max(gate, axis=1, keepdims=True)
        idx = jnp.min(jnp.where(gate == best, blk_ids, n_blk), axis=1, keepdims=True)
        sels.append(jnp.where(idx < qi, idx, -1))
        gate = jnp.where(blk_ids == idx, -jnp.inf, gate)

    rows = lax.broadcasted_iota(jnp.int32, (tq, blk), 0)
    cols = lax.broadcasted_iota(jnp.int32, (tq, blk), 1)
    rel = (rows - cols).astype(F32)

    m_ref[...] = jnp.full(m_ref.shape, NEG_INF, F32)
    l_ref[...] = jnp.zeros(l_ref.shape, F32)
    acc_ref[...] = jnp.zeros(acc_ref.shape, F32)

    def attend(j, keep):
        start = pl.multiple_of(j * blk, blk)
        kj = k_ref[pl.ds(start, blk), :]
        vj = v_ref[pl.ds(start, blk), :]
        s = lax.dot_general(q, kj, (((1,), (1,)), ((), ())), preferred_element_type=F32)
        dist = rel + ((qi - j) * blk).astype(F32)
        s = jnp.where(keep(j), s * scale - slope * dist, NEG_INF)
        m_old = m_ref[...]
        m_new = jnp.maximum(m_old, jnp.max(s, axis=1, keepdims=True))
        alpha = jnp.exp(m_old - m_new)
        p = jnp.exp(s - m_new)
        l_ref[...] = alpha * l_ref[...] + jnp.sum(p, axis=1, keepdims=True)
        acc_ref[...] = alpha * acc_ref[...] + jnp.dot(p.astype(BF16), vj,
                                                      preferred_element_type=F32)
        m_ref[...] = m_new

    def past_body(j, carry):
        def keep(jj):
            hit = sels[0] == jj
            for sel in sels[1:]:
                hit = hit | (sel == jj)
            return hit
        attend(j, keep)
        return carry

    lax.fori_loop(0, qi, past_body, 0)
    attend(qi, lambda jj: rows >= cols)
    o_ref[...] = (acc_ref[...] / l_ref[...]).astype(o_ref.dtype)


def moba_attention(qkv, k_mean, n_heads):
    b, s, d3 = qkv.shape
    d = d3 // 3
    hd = d // n_heads
    blk = MOBA_BLOCK
    n_blk = s // blk
    n_top = min(MOBA_TOPK, n_blk)
    slopes = jnp.exp2(-8.0 * jnp.arange(1, n_heads + 1, dtype=F32) / n_heads)
    grid_spec = pltpu.PrefetchScalarGridSpec(
        num_scalar_prefetch=1,
        grid=(b, n_heads, n_blk),
        in_specs=[
            pl.BlockSpec((None, blk, hd), lambda bi, h, i, *_: (bi, i, h)),
            pl.BlockSpec((None, s, hd), lambda bi, h, i, *_: (bi, 0, n_heads + h)),
            pl.BlockSpec((None, s, hd), lambda bi, h, i, *_: (bi, 0, 2 * n_heads + h)),
            pl.BlockSpec((None, n_blk, hd), lambda bi, h, i, *_: (bi, 0, h)),
        ],
        out_specs=pl.BlockSpec((None, blk, hd), lambda bi, h, i, *_: (bi, i, h)),
        scratch_shapes=[pltpu.VMEM((blk, 1), F32), pltpu.VMEM((blk, 1), F32),
                        pltpu.VMEM((blk, hd), F32)],
    )
    return pl.pallas_call(
        functools.partial(_moba_kernel, blk=blk, n_top=n_top, scale=hd ** -0.5),
        grid_spec=grid_spec,
        out_shape=jax.ShapeDtypeStruct((b, s, d), BF16),
        compiler_params=_params("parallel", "parallel", "arbitrary"),
        name="moba_attention",
    )(slopes, qkv, qkv, qkv, k_mean)


def _dot_hi(a, b):
    return jnp.dot(a, b, preferred_element_type=F32, precision=lax.Precision.HIGHEST)


def _bdot(a, b, dims):
    return lax.dot_general(a.astype(BF16), b.astype(BF16), dims, preferred_element_type=F32)


def _gdn_kernel(q_ref, k_ref, v_ref, z_ref, sm_ref, cwq_ref, cwk_ref, cwv_ref, alog_ref,
                dt_ref, nw_ref, o_ref, qbuf, kbuf, vbuf, tail_ref, state_ref,
                *, cs, width, n_heads):
    h = pl.program_id(1)
    ts, dh = q_ref.shape
    nc = ts // cs

    @pl.when(pl.program_id(2) == 0)
    def _():
        tail_ref[...] = jnp.zeros(tail_ref.shape, F32)
        state_ref[...] = jnp.zeros(state_ref.shape, F32)

    q = _silu(_causal_conv_rows(qbuf, tail_ref.at[0], q_ref[...], cwq_ref[...], width, ts))
    k = _silu(_causal_conv_rows(kbuf, tail_ref.at[1], k_ref[...], cwk_ref[...], width, ts))
    v = _silu(_causal_conv_rows(vbuf, tail_ref.at[2], v_ref[...], cwv_ref[...], width, ts))
    q = q * lax.rsqrt(jnp.sum(q * q, axis=-1, keepdims=True) + NORM_EPS) * (dh ** -0.5)
    k = k * lax.rsqrt(jnp.sum(k * k, axis=-1, keepdims=True) + NORM_EPS)

    small = sm_ref[...]
    lane = lax.broadcasted_iota(jnp.int32, small.shape, 1)
    beta_all = jax.nn.sigmoid(small)
    sp = small + dt_ref[...]
    softplus = jnp.maximum(sp, 0.0) + jnp.log1p(jnp.exp(-jnp.abs(sp)))
    g_all = -jnp.exp(alog_ref[...]) * softplus
    beta = jnp.sum(jnp.where(lane == h, beta_all, 0.0), axis=1, keepdims=True)
    g = jnp.sum(jnp.where(lane == n_heads + h, g_all, 0.0), axis=1, keepdims=True)

    ii = lax.broadcasted_iota(jnp.int32, (cs, cs), 0)
    jj = lax.broadcasted_iota(jnp.int32, (cs, cs), 1)
    incl = ii >= jj
    strict = ii > jj
    eye = (ii == jj).astype(F32)
    tril = incl.astype(F32)
    ones = jnp.ones((cs, cs), F32)

    k_beta = k * beta
    v_beta = v * beta
    nt = (((1,), (1,)), ((), ()))
    nn = (((1,), (0,)), ((), ()))

    for c in range(nc):
        sl = slice(c * cs, (c + 1) * cs)
        g_col = jnp.broadcast_to(g[sl], (cs, cs))
        gc_col = _dot_hi(tril, g_col)
        gc_row = _dot_hi(ones, jnp.where(ii == jj, gc_col, 0.0))
        diff = gc_col - gc_row
        decay = jnp.where(incl, jnp.exp(jnp.where(incl, diff, 0.0)), 0.0)
        gc = gc_col[:, 0:1]
        g_last = gc_col[cs - 1:cs, 0:1]
        e_gc = jnp.exp(gc)

        kc, qc = k[sl], q[sl]
        a_mat = jnp.where(strict, _bdot(k_beta[sl], kc, nt) * decay, 0.0)
        neg = -a_mat
        t_mat = eye + neg
        power = neg
        span = 2
        while span < cs:
            power = _dot_hi(power, power)
            t_mat = t_mat + _dot_hi(t_mat, power)
            span *= 2
        u = _bdot(t_mat, v_beta[sl], nn)
        w = _bdot(t_mat, k_beta[sl] * e_gc, nn)
        attn = jnp.where(incl, _bdot(qc, kc, nt) * decay, 0.0)
        q_dec = qc * e_gc
        k_dec = kc * jnp.exp(g_last - gc)

        state = state_ref[...]
        v_new = u - _bdot(w, state, nn)
        o = _bdot(q_dec, state, nn) + _bdot(attn, v_new, nn)
        state_ref[...] = state * jnp.exp(g_last) + _bdot(k_dec.T, v_new, nn)

        o = o * lax.rsqrt(jnp.mean(o * o, axis=-1, keepdims=True) + NORM_EPS) * nw_ref[...]
        o_ref[sl, :] = (o * _silu(z_ref[sl, :])).astype(o_ref.dtype)


def gdn_core(proj, conv_w, a_log, dt_bias, norm_w, n_heads, ts):
    b, s, _ = proj.shape
    dh = norm_w.shape[0]
    d = n_heads * dh
    width = conv_w.shape[0]
    pad = (n_heads, LANES - 2 * n_heads)
    alog_vec = jnp.pad(a_log.astype(F32), pad).reshape(1, LANES)
    dt_vec = jnp.pad(dt_bias.astype(F32), pad).reshape(1, LANES)

    def col(off):
        return pl.BlockSpec((None, ts, dh), lambda bi, h, i: (bi, i, off + h))

    def cw(off):
        return pl.BlockSpec((width, dh), lambda bi, h, i: (0, off + h))

    return pl.pallas_call(
        functools.partial(_gdn_kernel, cs=GDN_CHUNK, width=width, n_heads=n_heads),
        grid=(b, n_heads, s // ts),
        in_specs=[col(0), col(n_heads), col(2 * n_heads), col(3 * n_heads),
                  pl.BlockSpec((None, ts, LANES), lambda bi, h, i: (bi, i, 4 * d // LANES)),
                  cw(0), cw(n_heads), cw(2 * n_heads),
                  _resident((1, LANES)), _resident((1, LANES)), _resident((1, dh))],
        out_specs=pl.BlockSpec((None, ts, dh), lambda bi, h, i: (bi, i, h)),
        out_shape=jax.ShapeDtypeStruct((b, s, d), BF16),
        scratch_shapes=[pltpu.VMEM((ts + SUBLANES, dh), F32)] * 3
                       + [pltpu.VMEM((3, SUBLANES, dh), F32), pltpu.VMEM((dh, dh), F32)],
        compiler_params=_params("parallel", "parallel", "arbitrary"),
        name="gdn_core",
    )(proj, proj, proj, proj, proj, conv_w, conv_w, conv_w, alog_vec, dt_vec,
      norm_w.astype(F32).reshape(1, dh))


def _row_tile(n, target):
    t = min(n, target)
    while n % t:
        t //= 2
    return t


def kernel(x, mix_norm, ffn_norm, final_norm, moba_w_qkv, moba_w_o, sconv_w_in, sconv_conv,
           sconv_w_out, gdn_w_in, gdn_conv, gdn_a_log, gdn_dt_bias, gdn_norm, gdn_w_o,
           ffn_w_up, ffn_conv, ffn_w_down):
    bsz, seq, d = x.shape
    depth = mix_norm.shape[0]
    t = bsz * seq
    tm = _row_tile(seq, 512)
    for i in range(depth):
        kind, j = i % N_MIXERS, i // N_MIXERS
        if kind == 0:
            qkv, k_mean = norm_matmul(x.reshape(t, d), mix_norm[i], moba_w_qkv[j].astype(BF16),
                                      BF16, tm, km_cols=(d, 2 * d), km_rows=MOBA_BLOCK)
            o = moba_attention(qkv.reshape(bsz, seq, 3 * d),
                               k_mean.reshape(bsz, seq // MOBA_BLOCK, d), MOBA_HEADS)
            x = proj_residual(x.reshape(t, d), o.reshape(t, d), moba_w_o[j].astype(BF16),
                              tm).reshape(bsz, seq, d)
        elif kind == 1:
            x = short_conv_layer(x, mix_norm[i], sconv_w_in[j].astype(BF16), sconv_conv[j],
                                 sconv_w_out[j].astype(BF16), tm)
        else:
            n_in = gdn_w_in.shape[-1]
            w_in = jnp.pad(gdn_w_in[j], ((0, 0), (0, 4 * d + LANES - n_in))).astype(BF16)
            proj = norm_matmul(x.reshape(t, d), mix_norm[i], w_in, F32, tm)
            o = gdn_core(proj.reshape(bsz, seq, 4 * d + LANES), gdn_conv[j], gdn_a_log[j],
                         gdn_dt_bias[j], gdn_norm[j], GDN_HEADS, _row_tile(seq, 256))
            x = proj_residual(x.reshape(t, d), o.reshape(t, d), gdn_w_o[j].astype(BF16),
                              tm).reshape(bsz, seq, d)
        x = conv_ffn_layer(x, ffn_norm[i], ffn_w_up[i].astype(BF16), ffn_conv[i],
                           ffn_w_down[i].astype(BF16),
                           final_norm if i == depth - 1 else None, tm, 256)
    return x
```

```python
import functools

import jax
import jax.numpy as jnp
from jax import lax
from jax.experimental import pallas as pl
from jax.experimental.pallas import tpu as pltpu

F32 = jnp.float32
BF16 = jnp.bfloat16

N_MIXERS = 3
MOBA_HEADS = 8
MOBA_BLOCK = 256
MOBA_TOPK = 3
GDN_HEADS = 8
GDN_CHUNK = 64
NORM_EPS = 1e-6
NEG_INF = -1e30
LOG2E = 1.4426950408889634
MOBA_GROUP = 4

LANES = 128
SUBLANES = 8
VMEM_LIMIT = 56 * 1024 * 1024


def _params(*sem):
    return pltpu.CompilerParams(dimension_semantics=sem, vmem_limit_bytes=VMEM_LIMIT)


def _rmsnorm(x, g):
    return x * lax.rsqrt(jnp.mean(x * x, axis=-1, keepdims=True) + NORM_EPS) * g


def _silu(x):
    return x * jax.nn.sigmoid(x)


def _resident(shape):
    return pl.BlockSpec(shape, lambda *_: (0,) * len(shape))


def _norm_matmul_kernel(x_ref, g_ref, w_ref, o_ref, *km_ref, km_cols, km_rows):
    hn = _rmsnorm(x_ref[...], g_ref[...]).astype(BF16)
    y = jnp.dot(hn, w_ref[...], preferred_element_type=F32)
    o_ref[...] = y.astype(o_ref.dtype)
    if km_ref:
        lo, hi = km_cols
        yk = y[:, lo:hi]
        tm = yk.shape[0]
        km = jnp.mean(yk.reshape(tm // km_rows, km_rows, hi - lo), axis=1)
        km_ref[0][...] = km[:, None, :]


def norm_matmul(x2, g, w, out_dtype, tm, km_cols=None, km_rows=None):
    t, d = x2.shape
    n = w.shape[1]
    out_shape = [jax.ShapeDtypeStruct((t, n), out_dtype)]
    out_specs = [pl.BlockSpec((tm, n), lambda i: (i, 0))]
    if km_cols is not None:
        nk = km_cols[1] - km_cols[0]
        out_shape.append(jax.ShapeDtypeStruct((t // km_rows, 1, nk), F32))
        out_specs.append(pl.BlockSpec((tm // km_rows, 1, nk), lambda i: (i, 0, 0)))
    res = pl.pallas_call(
        functools.partial(_norm_matmul_kernel, km_cols=km_cols, km_rows=km_rows),
        grid=(t // tm,),
        in_specs=[pl.BlockSpec((tm, d), lambda i: (i, 0)),
                  _resident((1, d)),
                  _resident((d, n))],
        out_specs=out_specs,
        out_shape=out_shape,
        compiler_params=_params("parallel"),
        name="norm_matmul",
    )(x2, g.reshape(1, d), w)
    return res if km_cols is not None else res[0]


def _proj_residual_kernel(x_ref, o_ref, w_ref, out_ref):
    out_ref[...] = x_ref[...] + jnp.dot(o_ref[...], w_ref[...], preferred_element_type=F32)


def proj_residual(x2, o2, w, tm):
    t, d = x2.shape
    return pl.pallas_call(
        _proj_residual_kernel,
        grid=(t // tm,),
        in_specs=[pl.BlockSpec((tm, d), lambda i: (i, 0)),
                  pl.BlockSpec((tm, o2.shape[1]), lambda i: (i, 0)),
                  _resident(w.shape)],
        out_specs=pl.BlockSpec((tm, d), lambda i: (i, 0)),
        out_shape=jax.ShapeDtypeStruct((t, d), F32),
        compiler_params=_params("parallel"),
        name="proj_residual",
    )(x2, o2, w)


def _causal_conv_rows(buf_ref, tail_ref, cur, cw, width, tm):
    buf_ref[0:SUBLANES, :] = tail_ref[...]
    buf_ref[SUBLANES:SUBLANES + tm, :] = cur
    out = cw[width - 1:width, :] * cur
    for k in range(width - 1):
        off = SUBLANES - (width - 1) + k
        out = out + cw[k:k + 1, :] * buf_ref[off:off + tm, :]
    tail_ref[...] = cur[tm - SUBLANES:tm, :]
    return out


def _ffn_kernel(x_ref, g_ref, wup_ref, cw_ref, wdn_ref, *rest, ffn, tf, width, final):
    if final:
        fn_ref, out_ref, gbuf, ubuf, tail_ref, acc_ref = rest
    else:
        out_ref, gbuf, ubuf, tail_ref, acc_ref = rest
    tm = x_ref.shape[0]

    @pl.when(pl.program_id(1) == 0)
    def _():
        tail_ref[...] = jnp.zeros(tail_ref.shape, F32)

    x = x_ref[...]
    hn = _rmsnorm(x, g_ref[...]).astype(BF16)
    for c in range(ffn // tf):
        lo = c * tf
        hi = ffn + lo
        gate = jnp.dot(hn, wup_ref[:, lo:lo + tf], preferred_element_type=F32)
        up = jnp.dot(hn, wup_ref[:, hi:hi + tf], preferred_element_type=F32)
        gate = _causal_conv_rows(gbuf, tail_ref.at[:, lo:lo + tf], gate,
                                 cw_ref[:, lo:lo + tf], width, tm)
        up = _causal_conv_rows(ubuf, tail_ref.at[:, hi:hi + tf], up,
                               cw_ref[:, hi:hi + tf], width, tm)
        act = (_silu(gate) * up).astype(BF16)
        y = jnp.dot(act, wdn_ref[lo:lo + tf, :], preferred_element_type=F32)
        if c == 0:
            acc_ref[...] = y
        else:
            acc_ref[...] += y
    res = x + acc_ref[...]
    if final:
        res = _rmsnorm(res, fn_ref[...])
    out_ref[...] = res


def conv_ffn_layer(x, g, w_up, conv_w, w_down, final_g, tm, tf):
    b, s, d = x.shape
    ffn = w_down.shape[0]
    width = conv_w.shape[0]
    final = final_g is not None
    row = pl.BlockSpec((None, tm, d), lambda bi, i: (bi, i, 0))
    in_specs = [row, _resident((1, d)), _resident(w_up.shape), _resident(conv_w.shape),
                _resident(w_down.shape)]
    args = [x, g.reshape(1, d), w_up, conv_w, w_down]
    if final:
        in_specs.append(_resident((1, d)))
        args.append(final_g.reshape(1, d))
    return pl.pallas_call(
        functools.partial(_ffn_kernel, ffn=ffn, tf=tf, width=width, final=final),
        grid=(b, s // tm),
        in_specs=in_specs,
        out_specs=row,
        out_shape=jax.ShapeDtypeStruct((b, s, d), F32),
        scratch_shapes=[pltpu.VMEM((tm + SUBLANES, tf), F32),
                        pltpu.VMEM((tm + SUBLANES, tf), F32),
                        pltpu.VMEM((SUBLANES, 2 * ffn), F32),
                        pltpu.VMEM((tm, d), F32)],
        compiler_params=_params("parallel", "arbitrary"),
        name="conv_ffn",
    )(*args)


def _sconv_kernel(x_ref, g_ref, win_ref, cw_ref, wout_ref, out_ref, cbuf, tail_ref, *, width):
    tm, d = x_ref.shape

    @pl.when(pl.program_id(1) == 0)
    def _():
        tail_ref[...] = jnp.zeros(tail_ref.shape, F32)

    x = x_ref[...]
    hn = _rmsnorm(x, g_ref[...]).astype(BF16)
    b_gate = jnp.dot(hn, win_ref[:, 0:d], preferred_element_type=F32)
    c_gate = jnp.dot(hn, win_ref[:, d:2 * d], preferred_element_type=F32)
    xv = jnp.dot(hn, win_ref[:, 2 * d:3 * d], preferred_element_type=F32)
    y = b_gate * _causal_conv_rows(cbuf, tail_ref, c_gate * xv, cw_ref[...], width, tm)
    out_ref[...] = x + jnp.dot(y.astype(BF16), wout_ref[...], preferred_element_type=F32)


def short_conv_layer(x, g, w_in, conv_w, w_out, tm):
    b, s, d = x.shape
    row = pl.BlockSpec((None, tm, d), lambda bi, i: (bi, i, 0))
    return pl.pallas_call(
        functools.partial(_sconv_kernel, width=conv_w.shape[0]),
        grid=(b, s // tm),
        in_specs=[row, _resident((1, d)), _resident(w_in.shape), _resident(conv_w.shape),
                  _resident(w_out.shape)],
        out_specs=row,
        out_shape=jax.ShapeDtypeStruct((b, s, d), F32),
        scratch_shapes=[pltpu.VMEM((tm + SUBLANES, d), F32), pltpu.VMEM((SUBLANES, d), F32)],
        compiler_params=_params("parallel", "arbitrary"),
        name="short_conv",
    )(x, g.reshape(1, d), w_in, conv_w, w_out)


def _moba_proj_kernel(x_ref, g_ref, w_ref, qk_ref, vt_ref, km_ref, *, blk):
    tm, d = x_ref.shape
    hn = _rmsnorm(x_ref[...], g_ref[...]).astype(BF16)
    y = jnp.dot(hn, w_ref[...], preferred_element_type=F32)
    qk_ref[...] = y[:, 0:2 * d].astype(BF16)
    vt_ref[...] = y[:, 2 * d:3 * d].T.astype(BF16)
    km = jnp.mean(y[:, d:2 * d].reshape(tm // blk, blk, d), axis=1)
    km_ref[...] = km[:, None, :]


def moba_proj(x, g, w, tm):
    b, s, d = x.shape
    blk = MOBA_BLOCK
    return pl.pallas_call(
        functools.partial(_moba_proj_kernel, blk=blk),
        grid=(b, s // tm),
        in_specs=[pl.BlockSpec((None, tm, d), lambda bi, i: (bi, i, 0)),
                  _resident((1, d)), _resident(w.shape)],
        out_specs=[pl.BlockSpec((None, tm, 2 * d), lambda bi, i: (bi, i, 0)),
                   pl.BlockSpec((None, d, tm), lambda bi, i: (bi, 0, i)),
                   pl.BlockSpec((None, tm // blk, 1, d), lambda bi, i: (bi, i, 0, 0))],
        out_shape=[jax.ShapeDtypeStruct((b, s, 2 * d), BF16),
                   jax.ShapeDtypeStruct((b, d, s), BF16),
                   jax.ShapeDtypeStruct((b, s // blk, 1, d), F32)],
        compiler_params=_params("parallel", "parallel"),
        name="moba_proj",
    )(x, g.reshape(1, d), w)


def _moba_kernel(slopes_ref, q_ref, k_ref, vt_ref, km_ref, o_ref, m_ref, l_ref, acc_ref,
                 u_ref, top_ref, *, blk, n_top, scale, group):
    h = pl.program_id(1)
    qi = pl.program_id(2)
    tq, hd = q_ref.shape
    n_blk = km_ref.shape[0]
    slope2 = slopes_ref[h] * LOG2E
    c1 = scale * LOG2E
    q = q_ref[...]
    nt = (((1,), (1,)), ((), ()))

    gate = lax.dot_general(km_ref[...].astype(BF16), q, nt, preferred_element_type=F32)
    blk_ids = lax.broadcasted_iota(jnp.int32, (n_blk, tq), 0)
    gate = jnp.where(blk_ids < qi, gate, NEG_INF)
    sels = []
    for _ in range(n_top):
        best = jnp.max(gate, axis=0, keepdims=True)
        idx = jnp.min(jnp.where(gate == best, blk_ids, n_blk), axis=0, keepdims=True)
        sels.append(jnp.where(idx < qi, idx, n_blk))
        gate = jnp.where(blk_ids == idx, -jnp.inf, gate)

    kk = lax.broadcasted_iota(jnp.int32, (blk, tq), 0)
    qq = lax.broadcasted_iota(jnp.int32, (blk, tq), 1)
    bias = slope2 * (kk - qq).astype(F32)

    s = lax.dot_general(k_ref[pl.ds(pl.multiple_of(qi * blk, blk), blk), :], q, nt,
                        preferred_element_type=F32)
    u = s * c1 + jnp.where(kk <= qq, bias, NEG_INF)
    m0 = jnp.max(u, axis=0, keepdims=True)
    p = jnp.exp2(u - m0)
    m_ref[...] = m0
    l_ref[...] = jnp.sum(p, axis=0, keepdims=True)
    acc_ref[...] = jnp.dot(vt_ref[:, pl.ds(pl.multiple_of(qi * blk, blk), blk)],
                           p.astype(BF16), preferred_element_type=F32)

    def block_start(j):
        return pl.multiple_of(jnp.maximum(j, 0) * blk, blk)

    def score_group(g, slot):
        for t in range(group):
            j = qi - 1 - (g * group + t)
            s = lax.dot_general(k_ref[pl.ds(block_start(j), blk), :], q, nt,
                                preferred_element_type=F32)
            u = s * c1 + bias
            u_ref[slot, t] = u
            top_ref[slot, t] = jnp.max(u, axis=0, keepdims=True)

    def consume_group(g, slot):
        m_old = m_ref[...]
        m_new = m_old
        hits, offs = [], []
        for t in range(group):
            j = qi - 1 - (g * group + t)
            hit = sels[0] == j
            for sel in sels[1:]:
                hit = hit | (sel == j)
            off = -slope2 * ((qi - j) * blk).astype(F32)
            m_new = jnp.maximum(m_new, jnp.where(hit, top_ref[slot, t] + off, NEG_INF))
            hits.append(hit); offs.append(off)
        alpha = jnp.exp2(m_old - m_new)
        l_new = alpha * l_ref[...]
        acc = alpha * acc_ref[...]
        for t in range(group):
            j = qi - 1 - (g * group + t)
            p = jnp.exp2(u_ref[slot, t] - jnp.where(hits[t], m_new - offs[t], -NEG_INF))
            l_new = l_new + jnp.sum(p, axis=0, keepdims=True)
            acc = acc + jnp.dot(vt_ref[:, pl.ds(block_start(j), blk)], p.astype(BF16),
                                preferred_element_type=F32)
        m_ref[...] = m_new
        l_ref[...] = l_new
        acc_ref[...] = acc

    score_group(0, 0)

    def group_body(g, carry):
        slot = g % 2
        consume_group(g, slot)
        score_group(g + 1, 1 - slot)
        return carry

    lax.fori_loop(0, (qi + group - 1) // group, group_body, 0)
    o_ref[...] = (acc_ref[...] / l_ref[...]).T.astype(o_ref.dtype)


def moba_attention(qk, vt, k_mean, n_heads):
    b, s, d2 = qk.shape
    d = d2 // 2
    hd = d // n_heads
    blk = MOBA_BLOCK
    n_blk = s // blk
    n_top = min(MOBA_TOPK, n_blk)
    slopes = jnp.exp2(-8.0 * jnp.arange(1, n_heads + 1, dtype=F32) / n_heads)
    grid_spec = pltpu.PrefetchScalarGridSpec(
        num_scalar_prefetch=1,
        grid=(b, n_heads, n_blk),
        in_specs=[
            pl.BlockSpec((None, blk, hd), lambda bi, h, i, *_: (bi, i, h)),
            pl.BlockSpec((None, s, hd), lambda bi, h, i, *_: (bi, 0, n_heads + h)),
            pl.BlockSpec((None, hd, s), lambda bi, h, i, *_: (bi, h, 0)),
            pl.BlockSpec((None, n_blk, hd), lambda bi, h, i, *_: (bi, 0, h)),
        ],
        out_specs=pl.BlockSpec((None, blk, hd), lambda bi, h, i, *_: (bi, i, h)),
        scratch_shapes=[pltpu.VMEM((1, blk), F32), pltpu.VMEM((1, blk), F32),
                        pltpu.VMEM((hd, blk), F32),
                        pltpu.VMEM((2, MOBA_GROUP, blk, blk), F32),
                        pltpu.VMEM((2, MOBA_GROUP, 1, blk), F32)],
    )
    return pl.pallas_call(
        functools.partial(_moba_kernel, blk=blk, n_top=n_top, scale=hd ** -0.5,
                          group=MOBA_GROUP),
        grid_spec=grid_spec,
        out_shape=jax.ShapeDtypeStruct((b, s, d), BF16),
        compiler_params=_params("parallel", "parallel", "arbitrary"),
        name="moba_attention",
    )(slopes, qk, qk, vt, k_mean)


def _dot_hi(a, b):
    return jnp.dot(a, b, preferred_element_type=F32, precision=lax.Precision.HIGHEST)


def _split_bf16(a):
    hi = a.astype(BF16)
    return hi, (a - hi.astype(F32)).astype(BF16)


def _dot_split(a, b):
    a_hi, a_lo = _split_bf16(a)
    b_hi, b_lo = _split_bf16(b)
    return (jnp.dot(a_hi, b_hi, preferred_element_type=F32)
            + jnp.dot(a_hi, b_lo, preferred_element_type=F32)
            + jnp.dot(a_lo, b_hi, preferred_element_type=F32))


def _gdn_gates_kernel(sm_ref, alog_ref, dt_ref, o_ref, *, cs, n_heads):
    small = sm_ref[...]
    tm = small.shape[0]
    sp = small + dt_ref[...]
    softplus = jnp.maximum(sp, 0.0) + jnp.log1p(jnp.exp(-jnp.abs(sp)))
    g = -jnp.exp(alog_ref[...]) * softplus
    ii = lax.broadcasted_iota(jnp.int32, (tm, tm), 0)
    jj = lax.broadcasted_iota(jnp.int32, (tm, tm), 1)
    same_chunk_prefix = ((ii // cs) == (jj // cs)) & (ii >= jj)
    gc = _dot_hi(same_chunk_prefix.astype(F32), g)
    lane = lax.broadcasted_iota(jnp.int32, small.shape, 1)
    o_ref[...] = jnp.where(lane < n_heads, jax.nn.sigmoid(small), gc)


def gdn_gates(proj2, a_log, dt_bias, n_heads, tm):
    t, n = proj2.shape
    last = n // LANES - 1
    pad = (n_heads, LANES - 2 * n_heads)
    alog_vec = jnp.pad(a_log.astype(F32), pad).reshape(1, LANES)
    dt_vec = jnp.pad(dt_bias.astype(F32), pad).reshape(1, LANES)
    return pl.pallas_call(
        functools.partial(_gdn_gates_kernel, cs=GDN_CHUNK, n_heads=n_heads),
        grid=(t // tm,),
        in_specs=[pl.BlockSpec((tm, LANES), lambda i: (i, last)),
                  _resident((1, LANES)), _resident((1, LANES))],
        out_specs=pl.BlockSpec((tm, LANES), lambda i: (i, 0)),
        out_shape=jax.ShapeDtypeStruct((t, LANES), F32),
        compiler_params=_params("parallel"),
        name="gdn_gates",
    )(proj2, alog_vec, dt_vec)


def _gdn_kernel(q_ref, k_ref, v_ref, z_ref, gate_ref, gcr_ref, cwq_ref, cwk_ref, cwv_ref,
                nw_ref, o_ref, qbuf, kbuf, vbuf, tail_ref, state_ref, *, cs, width, n_heads):
    h = pl.program_id(1)
    ts, dh = q_ref.shape
    nc = ts // cs

    @pl.when(pl.program_id(2) == 0)
    def _():
        tail_ref[...] = jnp.zeros(tail_ref.shape, F32)
        state_ref[...] = jnp.zeros(state_ref.shape, F32)

    q = _silu(_causal_conv_rows(qbuf, tail_ref.at[0], q_ref[...], cwq_ref[...], width, ts))
    k = _silu(_causal_conv_rows(kbuf, tail_ref.at[1], k_ref[...], cwk_ref[...], width, ts))
    v = _silu(_causal_conv_rows(vbuf, tail_ref.at[2], v_ref[...], cwv_ref[...], width, ts))
    q = q * lax.rsqrt(jnp.sum(q * q, axis=-1, keepdims=True) + NORM_EPS) * (dh ** -0.5)
    k = k * lax.rsqrt(jnp.sum(k * k, axis=-1, keepdims=True) + NORM_EPS)

    gates = gate_ref[...]
    lane = lax.broadcasted_iota(jnp.int32, gates.shape, 1)
    beta = jnp.sum(jnp.where(lane == h, gates, 0.0), axis=1, keepdims=True)
    gc = jnp.sum(jnp.where(lane == n_heads + h, gates, 0.0), axis=1, keepdims=True)
    gc_rows = gcr_ref[...]

    ii = lax.broadcasted_iota(jnp.int32, (cs, cs), 0)
    jj = lax.broadcasted_iota(jnp.int32, (cs, cs), 1)
    incl = ii >= jj
    strict = ii > jj
    eye = (ii == jj).astype(F32)
    nt = (((1,), (1,)), ((), ()))
    nn = (((1,), (0,)), ((), ()))

    e_gc = jnp.exp(gc)
    k_beta = k * beta
    v_beta = (v * beta).astype(BF16)
    kb_dec = (k_beta * e_gc).astype(BF16)
    q_dec = q * e_gc
    k_beta = k_beta.astype(BF16)
    k16 = k.astype(BF16)
    q16 = q.astype(BF16)
    chunks = [slice(c * cs, (c + 1) * cs) for c in range(nc)]

    decays, negs = [], []
    for c, sl in enumerate(chunks):
        diff = gc[sl] - gc_rows[c:c + 1, :]
        decay = jnp.where(incl, jnp.exp(jnp.where(incl, diff, 0.0)), 0.0)
        kk = lax.dot_general(k_beta[sl], k16[sl], nt, preferred_element_type=F32)
        decays.append(decay)
        negs.append(jnp.where(strict, -(kk * decay), 0.0))
    t_mats = [eye + neg for neg in negs]
    powers = negs
    span = 2
    while span < cs:
        powers = [_dot_split(p, p) for p in powers]
        t_mats = [t + _dot_split(t, p) for t, p in zip(t_mats, powers)]
        span *= 2

    stacked, n_mats, attns, sds = [], [], [], []
    for c, sl in enumerate(chunks):
        t16 = t_mats[c].astype(BF16)
        u = jnp.dot(t16, v_beta[sl], preferred_element_type=F32)
        w = jnp.dot(t16, kb_dec[sl], preferred_element_type=F32)
        qk = lax.dot_general(q16[sl], k16[sl], nt, preferred_element_type=F32)
        attns.append(jnp.where(incl, qk * decays[c], 0.0).astype(BF16))
        g_last = gc[sl][cs - 1:cs, :]
        kd_t = (k[sl] * jnp.exp(g_last - gc[sl])).T.astype(BF16)
        w16 = w.astype(BF16)
        kw = jnp.dot(kd_t, w16, preferred_element_type=F32)
        n_mats.append((u, jnp.dot(kd_t, u.astype(BF16), preferred_element_type=F32)))
        stacked.append(jnp.concatenate([w16, q_dec[sl].astype(BF16), kw.astype(BF16)], axis=0))
        sds.append(jnp.exp(g_last))

    state = state_ref[...]
    for c, sl in enumerate(chunks):
        prod = jnp.dot(stacked[c], state.astype(BF16), preferred_element_type=F32)
        u, ku = n_mats[c]
        v_new = u - prod[0:cs]
        o = prod[cs:2 * cs] + jnp.dot(attns[c], v_new.astype(BF16),
                                      preferred_element_type=F32)
        state = state * sds[c] - prod[2 * cs:] + ku
        o = o * lax.rsqrt(jnp.mean(o * o, axis=-1, keepdims=True) + NORM_EPS) * nw_ref[...]
        o_ref[sl, :] = (o * _silu(z_ref[sl, :])).astype(o_ref.dtype)
    state_ref[...] = state


def gdn_core(proj, gates, conv_w, norm_w, n_heads, ts):
    b, s, _ = proj.shape
    dh = norm_w.shape[0]
    d = n_heads * dh
    width = conv_w.shape[0]
    cs = GDN_CHUNK
    gc_rows = gates[:, :, n_heads:2 * n_heads].reshape(b, s // cs, cs, n_heads)
    gc_rows = gc_rows.transpose(0, 3, 1, 2)

    def col(off):
        return pl.BlockSpec((None, ts, dh), lambda bi, h, i: (bi, i, off + h))

    def cw(off):
        return pl.BlockSpec((width, dh), lambda bi, h, i: (0, off + h))

    return pl.pallas_call(
        functools.partial(_gdn_kernel, cs=GDN_CHUNK, width=width, n_heads=n_heads),
        grid=(b, n_heads, s // ts),
        in_specs=[col(0), col(n_heads), col(2 * n_heads), col(3 * n_heads),
                  pl.BlockSpec((None, ts, LANES), lambda bi, h, i: (bi, i, 0)),
                  pl.BlockSpec((None, None, ts // cs, cs), lambda bi, h, i: (bi, h, i, 0)),
                  cw(0), cw(n_heads), cw(2 * n_heads), _resident((1, dh))],
        out_specs=pl.BlockSpec((None, ts, dh), lambda bi, h, i: (bi, i, h)),
        out_shape=jax.ShapeDtypeStruct((b, s, d), BF16),
        scratch_shapes=[pltpu.VMEM((ts + SUBLANES, dh), F32)] * 3
                       + [pltpu.VMEM((3, SUBLANES, dh), F32), pltpu.VMEM((dh, dh), F32)],
        compiler_params=_params("parallel", "parallel", "arbitrary"),
        name="gdn_core",
    )(proj, proj, proj, proj, gates, gc_rows, conv_w, conv_w, conv_w,
      norm_w.astype(F32).reshape(1, dh))


def _row_tile(n, target):
    t = min(n, target)
    while n % t:
        t //= 2
    return t


def kernel(x, mix_norm, ffn_norm, final_norm, moba_w_qkv, moba_w_o, sconv_w_in, sconv_conv,
           sconv_w_out, gdn_w_in, gdn_conv, gdn_a_log, gdn_dt_bias, gdn_norm, gdn_w_o,
           ffn_w_up, ffn_conv, ffn_w_down):
    bsz, seq, d = x.shape
    depth = mix_norm.shape[0]
    t = bsz * seq
    tm = _row_tile(seq, 512)
    for i in range(depth):
        kind, j = i % N_MIXERS, i // N_MIXERS
        if kind == 0:
            qk, vt, k_mean = moba_proj(x, mix_norm[i], moba_w_qkv[j].astype(BF16), tm)
            o = moba_attention(qk, vt, k_mean.reshape(bsz, seq // MOBA_BLOCK, d), MOBA_HEADS)
            x = proj_residual(x.reshape(t, d), o.reshape(t, d), moba_w_o[j].astype(BF16),
                              tm).reshape(bsz, seq, d)
        elif kind == 1:
            x = short_conv_layer(x, mix_norm[i], sconv_w_in[j].astype(BF16), sconv_conv[j],
                                 sconv_w_out[j].astype(BF16), tm)
        else:
            n_in = gdn_w_in.shape[-1]
            w_in = jnp.pad(gdn_w_in[j], ((0, 0), (0, 4 * d + LANES - n_in))).astype(BF16)
            proj = norm_matmul(x.reshape(t, d), mix_norm[i], w_in, F32, tm)
            gates = gdn_gates(proj, gdn_a_log[j], gdn_dt_bias[j], GDN_HEADS, tm)
            o = gdn_core(proj.reshape(bsz, seq, 4 * d + LANES), gates.reshape(bsz, seq, LANES),
                         gdn_conv[j], gdn_norm[j], GDN_HEADS, _row_tile(seq, 512))
            x = proj_residual(x.reshape(t, d), o.reshape(t, d), gdn_w_o[j].astype(BF16),
                              tm).reshape(bsz, seq, d)
        x = conv_ffn_layer(x, ffn_norm[i], ffn_w_up[i].astype(BF16), ffn_conv[i],
                           ffn_w_down[i].astype(BF16),
                           final_norm if i == depth - 1 else None, tm, 256)
    return x
```

```python
import functools

import jax
import jax.numpy as jnp
from jax import lax
from jax.experimental import pallas as pl
from jax.experimental.pallas import tpu as pltpu

F32 = jnp.float32
BF16 = jnp.bfloat16

N_MIXERS = 3
MOBA_HEADS = 8
MOBA_BLOCK = 256
MOBA_TOPK = 3
GDN_HEADS = 8
GDN_CHUNK = 64
NORM_EPS = 1e-6
NEG_INF = -1e30
LOG2E = 1.4426950408889634
MOBA_GROUP = 8
MOBA_HEADS_PER_STEP = 2

LANES = 128
SUBLANES = 8
BF16_SUBLANES = 16
VMEM_LIMIT = 56 * 1024 * 1024


def _params(*sem):
    return pltpu.CompilerParams(dimension_semantics=sem, vmem_limit_bytes=VMEM_LIMIT)


def _rmsnorm(x, g):
    return x * lax.rsqrt(jnp.mean(x * x, axis=-1, keepdims=True) + NORM_EPS) * g


def _silu(x):
    return x * jax.nn.sigmoid(x)


def _resident(shape):
    return pl.BlockSpec(shape, lambda *_: (0,) * len(shape))


def _norm_matmul_kernel(x_ref, g_ref, w_ref, o_ref, *km_ref, km_cols, km_rows):
    hn = _rmsnorm(x_ref[...], g_ref[...]).astype(BF16)
    y = jnp.dot(hn, w_ref[...], preferred_element_type=F32)
    o_ref[...] = y.astype(o_ref.dtype)
    if km_ref:
        lo, hi = km_cols
        yk = y[:, lo:hi]
        tm = yk.shape[0]
        km = jnp.mean(yk.reshape(tm // km_rows, km_rows, hi - lo), axis=1)
        km_ref[0][...] = km[:, None, :]


def norm_matmul(x2, g, w, out_dtype, tm, km_cols=None, km_rows=None):
    t, d = x2.shape
    n = w.shape[1]
    out_shape = [jax.ShapeDtypeStruct((t, n), out_dtype)]
    out_specs = [pl.BlockSpec((tm, n), lambda i: (i, 0))]
    if km_cols is not None:
        nk = km_cols[1] - km_cols[0]
        out_shape.append(jax.ShapeDtypeStruct((t // km_rows, 1, nk), F32))
        out_specs.append(pl.BlockSpec((tm // km_rows, 1, nk), lambda i: (i, 0, 0)))
    res = pl.pallas_call(
        functools.partial(_norm_matmul_kernel, km_cols=km_cols, km_rows=km_rows),
        grid=(t // tm,),
        in_specs=[pl.BlockSpec((tm, d), lambda i: (i, 0)),
                  _resident((1, d)),
                  _resident((d, n))],
        out_specs=out_specs,
        out_shape=out_shape,
        compiler_params=_params("parallel"),
        name="norm_matmul",
    )(x2, g.reshape(1, d), w)
    return res if km_cols is not None else res[0]


def _proj_residual_kernel(x_ref, o_ref, w_ref, out_ref):
    out_ref[...] = x_ref[...] + jnp.dot(o_ref[...], w_ref[...], preferred_element_type=F32)


def proj_residual(x2, o2, w, tm):
    t, d = x2.shape
    return pl.pallas_call(
        _proj_residual_kernel,
        grid=(t // tm,),
        in_specs=[pl.BlockSpec((tm, d), lambda i: (i, 0)),
                  pl.BlockSpec((tm, o2.shape[1]), lambda i: (i, 0)),
                  _resident(w.shape)],
        out_specs=pl.BlockSpec((tm, d), lambda i: (i, 0)),
        out_shape=jax.ShapeDtypeStruct((t, d), F32),
        compiler_params=_params("parallel"),
        name="proj_residual",
    )(x2, o2, w)


def _causal_conv_rows(buf_ref, tail_ref, cur, cw, width, tm):
    buf_ref[0:SUBLANES, :] = tail_ref[...]
    buf_ref[SUBLANES:SUBLANES + tm, :] = cur
    out = cw[width - 1:width, :] * cur
    for k in range(width - 1):
        off = SUBLANES - (width - 1) + k
        out = out + cw[k:k + 1, :] * buf_ref[off:off + tm, :]
    tail_ref[...] = cur[tm - SUBLANES:tm, :]
    return out


def _ffn_kernel(x_ref, g_ref, wup_ref, cw_ref, wdn_ref, *rest, ffn, tf, width, final):
    if final:
        fn_ref, out_ref, gbuf, ubuf, tail_ref, acc_ref = rest
    else:
        out_ref, gbuf, ubuf, tail_ref, acc_ref = rest
    tm = x_ref.shape[0]

    @pl.when(pl.program_id(1) == 0)
    def _():
        tail_ref[...] = jnp.zeros(tail_ref.shape, F32)

    x = x_ref[...]
    hn = _rmsnorm(x, g_ref[...]).astype(BF16)
    for c in range(ffn // tf):
        lo = c * tf
        hi = ffn + lo
        gate = jnp.dot(hn, wup_ref[:, lo:lo + tf], preferred_element_type=F32)
        up = jnp.dot(hn, wup_ref[:, hi:hi + tf], preferred_element_type=F32)
        gate = _causal_conv_rows(gbuf, tail_ref.at[:, lo:lo + tf], gate,
                                 cw_ref[:, lo:lo + tf], width, tm)
        up = _causal_conv_rows(ubuf, tail_ref.at[:, hi:hi + tf], up,
                               cw_ref[:, hi:hi + tf], width, tm)
        act = (_silu(gate) * up).astype(BF16)
        y = jnp.dot(act, wdn_ref[lo:lo + tf, :], preferred_element_type=F32)
        if c == 0:
            acc_ref[...] = y
        else:
            acc_ref[...] += y
    res = x + acc_ref[...]
    if final:
        res = _rmsnorm(res, fn_ref[...])
    out_ref[...] = res


def conv_ffn_layer(x, g, w_up, conv_w, w_down, final_g, tm, tf):
    b, s, d = x.shape
    ffn = w_down.shape[0]
    width = conv_w.shape[0]
    final = final_g is not None
    row = pl.BlockSpec((None, tm, d), lambda bi, i: (bi, i, 0))
    in_specs = [row, _resident((1, d)), _resident(w_up.shape), _resident(conv_w.shape),
                _resident(w_down.shape)]
    args = [x, g.reshape(1, d), w_up, conv_w, w_down]
    if final:
        in_specs.append(_resident((1, d)))
        args.append(final_g.reshape(1, d))
    return pl.pallas_call(
        functools.partial(_ffn_kernel, ffn=ffn, tf=tf, width=width, final=final),
        grid=(b, s // tm),
        in_specs=in_specs,
        out_specs=row,
        out_shape=jax.ShapeDtypeStruct((b, s, d), F32),
        scratch_shapes=[pltpu.VMEM((tm + SUBLANES, tf), F32),
                        pltpu.VMEM((tm + SUBLANES, tf), F32),
                        pltpu.VMEM((SUBLANES, 2 * ffn), F32),
                        pltpu.VMEM((tm, d), F32)],
        compiler_params=_params("parallel", "arbitrary"),
        name="conv_ffn",
    )(*args)


def _sconv_kernel(x_ref, g_ref, win_ref, cw_ref, wout_ref, out_ref, cbuf, tail_ref, *, width):
    tm, d = x_ref.shape

    @pl.when(pl.program_id(1) == 0)
    def _():
        tail_ref[...] = jnp.zeros(tail_ref.shape, F32)

    x = x_ref[...]
    hn = _rmsnorm(x, g_ref[...]).astype(BF16)
    b_gate = jnp.dot(hn, win_ref[:, 0:d], preferred_element_type=F32)
    c_gate = jnp.dot(hn, win_ref[:, d:2 * d], preferred_element_type=F32)
    xv = jnp.dot(hn, win_ref[:, 2 * d:3 * d], preferred_element_type=F32)
    y = b_gate * _causal_conv_rows(cbuf, tail_ref, c_gate * xv, cw_ref[...], width, tm)
    out_ref[...] = x + jnp.dot(y.astype(BF16), wout_ref[...], preferred_element_type=F32)


def short_conv_layer(x, g, w_in, conv_w, w_out, tm):
    b, s, d = x.shape
    row = pl.BlockSpec((None, tm, d), lambda bi, i: (bi, i, 0))
    return pl.pallas_call(
        functools.partial(_sconv_kernel, width=conv_w.shape[0]),
        grid=(b, s // tm),
        in_specs=[row, _resident((1, d)), _resident(w_in.shape), _resident(conv_w.shape),
                  _resident(w_out.shape)],
        out_specs=row,
        out_shape=jax.ShapeDtypeStruct((b, s, d), F32),
        scratch_shapes=[pltpu.VMEM((tm + SUBLANES, d), F32), pltpu.VMEM((SUBLANES, d), F32)],
        compiler_params=_params("parallel", "arbitrary"),
        name="short_conv",
    )(x, g.reshape(1, d), w_in, conv_w, w_out)


def _moba_proj_kernel(x_ref, g_ref, w_ref, qk_ref, vt_ref, km_ref, *, blk, q_scale):
    tm, d = x_ref.shape
    n_heads, rows, _ = vt_ref.shape
    hd = d // n_heads
    hn = _rmsnorm(x_ref[...], g_ref[...]).astype(BF16)
    y = jnp.dot(hn, w_ref[...], preferred_element_type=F32)
    qk_ref[:, 0:d] = (y[:, 0:d] * q_scale).astype(BF16)
    qk_ref[:, d:2 * d] = y[:, d:2 * d].astype(BF16)
    vt_ref[:, 0:hd, :] = y[:, 2 * d:3 * d].T.reshape(n_heads, hd, tm).astype(BF16)
    vt_ref[:, hd:rows, :] = jnp.ones((n_heads, rows - hd, tm), BF16)
    km = jnp.mean(y[:, d:2 * d].reshape(tm // blk, blk, d), axis=1)
    km_ref[...] = km[:, None, :]


def moba_proj(x, g, w, n_heads, tm):
    b, s, d = x.shape
    blk = MOBA_BLOCK
    hd = d // n_heads
    rows = hd + BF16_SUBLANES
    return pl.pallas_call(
        functools.partial(_moba_proj_kernel, blk=blk, q_scale=hd ** -0.5 * LOG2E),
        grid=(b, s // tm),
        in_specs=[pl.BlockSpec((None, tm, d), lambda bi, i: (bi, i, 0)),
                  _resident((1, d)), _resident(w.shape)],
        out_specs=[pl.BlockSpec((None, tm, 2 * d), lambda bi, i: (bi, i, 0)),
                   pl.BlockSpec((None, n_heads, rows, tm), lambda bi, i: (bi, 0, 0, i)),
                   pl.BlockSpec((None, tm // blk, 1, d), lambda bi, i: (bi, i, 0, 0))],
        out_shape=[jax.ShapeDtypeStruct((b, s, 2 * d), BF16),
                   jax.ShapeDtypeStruct((b, n_heads, rows, s), BF16),
                   jax.ShapeDtypeStruct((b, s // blk, 1, d), F32)],
        compiler_params=_params("parallel", "parallel"),
        name="moba_proj",
    )(x, g.reshape(1, d), w)


def _moba_kernel(slopes_ref, q_ref, k_ref, vt_ref, km_ref, o_ref, m_ref, acc_ref,
                 u0_ref, u1_ref, top0_ref, top1_ref, *, blk, n_top, group):
    qi = pl.program_id(2)
    tq = q_ref.shape[0]
    n_heads = vt_ref.shape[0]
    hd = q_ref.shape[1] // n_heads
    n_blk = km_ref.shape[0]
    nt = (((1,), (1,)), ((), ()))
    n_groups = (qi + group) // group
    u_refs = (u0_ref, u1_ref)
    top_refs = (top0_ref, top1_ref)
    heads = range(n_heads)
    cols = [slice(a * hd, (a + 1) * hd) for a in heads]
    slope2 = [slopes_ref[pl.program_id(1) * n_heads + a] * LOG2E for a in heads]
    qs = [q_ref[:, cols[a]] for a in heads]

    kk = lax.broadcasted_iota(jnp.int32, (blk, tq), 0)
    qq = lax.broadcasted_iota(jnp.int32, (blk, tq), 1)
    rel = (kk - qq).astype(F32)
    bias = [slope2[a] * rel for a in heads]
    bias_own = [jnp.where(kk <= qq, bias[a], NEG_INF) for a in heads]

    def block_start(j):
        return pl.multiple_of(jnp.maximum(j, 0) * blk, blk)

    def score_group(g, slot):
        for t in range(group):
            j = qi - (g * group + t)
            for a in heads:
                s = lax.dot_general(k_ref[pl.ds(block_start(j), blk), cols[a]], qs[a], nt,
                                    preferred_element_type=F32)
                u = s + (jnp.where(g == 0, bias_own[a], bias[a]) if t == 0 else bias[a])
                u_refs[slot][a, t] = u
                top_refs[slot][a, t] = jnp.max(u, axis=0, keepdims=True)

    score_group(0, 0)
    for a in heads:
        m_ref[a] = jnp.full((1, tq), NEG_INF, F32)
        acc_ref[a] = jnp.zeros(acc_ref.shape[1:], F32)

    blk_ids = lax.broadcasted_iota(jnp.int32, (n_blk, tq), 0)
    sels = []
    for a in heads:
        gate = lax.dot_general(km_ref[:, cols[a]].astype(BF16), qs[a], nt,
                               preferred_element_type=F32)
        gate = jnp.where(blk_ids < qi, gate, NEG_INF)
        picks = []
        for _ in range(n_top):
            best = jnp.max(gate, axis=0, keepdims=True)
            idx = jnp.min(jnp.where(gate == best, blk_ids, n_blk), axis=0, keepdims=True)
            picks.append(jnp.where(idx < qi, idx, n_blk))
            gate = jnp.where(blk_ids == idx, -jnp.inf, gate)
        sels.append(picks)

    def consume_group(g, slot):
        for a in heads:
            m_old = m_ref[a]
            m_new = m_old
            hits, offs = [], []
            for t in range(group):
                j = qi - (g * group + t)
                hit = (sels[a][0] == j) | (j == qi)
                for sel in sels[a][1:]:
                    hit = hit | (sel == j)
                off = -slope2[a] * ((qi - j) * blk).astype(F32)
                m_new = jnp.maximum(m_new,
                                    jnp.where(hit, top_refs[slot][a, t] + off, NEG_INF))
                hits.append(hit); offs.append(off)
            acc = jnp.exp2(m_old - m_new) * acc_ref[a]
            for t in range(group):
                j = qi - (g * group + t)
                shift = jnp.where(hits[t], m_new - offs[t], -NEG_INF)
                p = jnp.exp2((u_refs[slot][a, t] - shift).astype(BF16))
                acc = acc + jnp.dot(vt_ref[a, :, pl.ds(block_start(j), blk)], p,
                                    preferred_element_type=F32)
            m_ref[a] = m_new
            acc_ref[a] = acc

    def group_body(g, carry):
        for slot in range(2):
            @pl.when(g % 2 == slot)
            def _():
                score_group(g + 1, 1 - slot)
                consume_group(g, slot)
        return carry

    lax.fori_loop(0, n_groups - 1, group_body, 0)
    for slot in range(2):
        @pl.when((n_groups - 1) % 2 == slot)
        def _():
            consume_group(n_groups - 1, slot)

    for a in heads:
        acc = acc_ref[a]
        o_ref[:, cols[a]] = (acc[0:hd] / acc[hd:hd + 1]).T.astype(o_ref.dtype)


def moba_attention(qk, vt, k_mean, n_heads):
    b, s, d2 = qk.shape
    d = d2 // 2
    hd = d // n_heads
    rows = vt.shape[2]
    blk = MOBA_BLOCK
    n_blk = s // blk
    n_top = min(MOBA_TOPK, n_blk)
    slopes = jnp.exp2(-8.0 * jnp.arange(1, n_heads + 1, dtype=F32) / n_heads)
    hps = MOBA_HEADS_PER_STEP
    wide = hps * hd
    once = pl.Buffered(1)
    grid_spec = pltpu.PrefetchScalarGridSpec(
        num_scalar_prefetch=1,
        grid=(b, n_heads // hps, n_blk),
        in_specs=[
            pl.BlockSpec((None, blk, wide), lambda bi, h, i, *_: (bi, i, h)),
            pl.BlockSpec((None, s, wide), lambda bi, h, i, *_: (bi, 0, n_heads // hps + h),
                         pipeline_mode=once),
            pl.BlockSpec((None, hps, rows, s), lambda bi, h, i, *_: (bi, h, 0, 0),
                         pipeline_mode=once),
            pl.BlockSpec((None, n_blk, wide), lambda bi, h, i, *_: (bi, 0, h)),
        ],
        out_specs=pl.BlockSpec((None, blk, wide), lambda bi, h, i, *_: (bi, i, h)),
        scratch_shapes=[pltpu.VMEM((hps, 1, blk), F32),
                        pltpu.VMEM((hps, rows, blk), F32),
                        pltpu.VMEM((hps, MOBA_GROUP, blk, blk), F32),
                        pltpu.VMEM((hps, MOBA_GROUP, blk, blk), F32),
                        pltpu.VMEM((hps, MOBA_GROUP, 1, blk), F32),
                        pltpu.VMEM((hps, MOBA_GROUP, 1, blk), F32)],
    )
    return pl.pallas_call(
        functools.partial(_moba_kernel, blk=blk, n_top=n_top, group=MOBA_GROUP),
        grid_spec=grid_spec,
        out_shape=jax.ShapeDtypeStruct((b, s, d), BF16),
        compiler_params=_params("parallel", "parallel", "arbitrary"),
        name="moba_attention",
    )(slopes, qk, qk, vt, k_mean)


def _dot_hi(a, b):
    return jnp.dot(a, b, preferred_element_type=F32, precision=lax.Precision.HIGHEST)


def _split_bf16(a):
    hi = a.astype(BF16)
    return hi, (a - hi.astype(F32)).astype(BF16)


def _dot_split(a, b):
    a_hi, a_lo = _split_bf16(a)
    b_hi, b_lo = _split_bf16(b)
    return (jnp.dot(a_hi, b_hi, preferred_element_type=F32)
            + jnp.dot(a_hi, b_lo, preferred_element_type=F32)
            + jnp.dot(a_lo, b_hi, preferred_element_type=F32))


def _gdn_gates_kernel(sm_ref, alog_ref, dt_ref, o_ref, *, cs, n_heads):
    small = sm_ref[...]
    tm = small.shape[0]
    sp = small + dt_ref[...]
    softplus = jnp.maximum(sp, 0.0) + jnp.log1p(jnp.exp(-jnp.abs(sp)))
    g = -jnp.exp(alog_ref[...]) * softplus
    ii = lax.broadcasted_iota(jnp.int32, (tm, tm), 0)
    jj = lax.broadcasted_iota(jnp.int32, (tm, tm), 1)
    same_chunk_prefix = ((ii // cs) == (jj // cs)) & (ii >= jj)
    gc = _dot_hi(same_chunk_prefix.astype(F32), g)
    lane = lax.broadcasted_iota(jnp.int32, small.shape, 1)
    o_ref[...] = jnp.where(lane < n_heads, jax.nn.sigmoid(small), gc)


def gdn_gates(proj2, a_log, dt_bias, n_heads, tm):
    t, n = proj2.shape
    last = n // LANES - 1
    pad = (n_heads, LANES - 2 * n_heads)
    alog_vec = jnp.pad(a_log.astype(F32), pad).reshape(1, LANES)
    dt_vec = jnp.pad(dt_bias.astype(F32), pad).reshape(1, LANES)
    return pl.pallas_call(
        functools.partial(_gdn_gates_kernel, cs=GDN_CHUNK, n_heads=n_heads),
        grid=(t // tm,),
        in_specs=[pl.BlockSpec((tm, LANES), lambda i: (i, last)),
                  _resident((1, LANES)), _resident((1, LANES))],
        out_specs=pl.BlockSpec((tm, LANES), lambda i: (i, 0)),
        out_shape=jax.ShapeDtypeStruct((t, LANES), F32),
        compiler_params=_params("parallel"),
        name="gdn_gates",
    )(proj2, alog_vec, dt_vec)


def _gdn_kernel(q_ref, k_ref, v_ref, z_ref, gate_ref, gcr_ref, cwq_ref, cwk_ref, cwv_ref,
                nw_ref, o_ref, qbuf, kbuf, vbuf, tail_ref, state_ref, *, cs, width, n_heads):
    h = pl.program_id(1)
    ts, dh = q_ref.shape
    nc = ts // cs

    @pl.when(pl.program_id(2) == 0)
    def _():
        tail_ref[...] = jnp.zeros(tail_ref.shape, F32)
        state_ref[...] = jnp.zeros(state_ref.shape, F32)

    q = _silu(_causal_conv_rows(qbuf, tail_ref.at[0], q_ref[...], cwq_ref[...], width, ts))
    k = _silu(_causal_conv_rows(kbuf, tail_ref.at[1], k_ref[...], cwk_ref[...], width, ts))
    v = _silu(_causal_conv_rows(vbuf, tail_ref.at[2], v_ref[...], cwv_ref[...], width, ts))
    q = q * lax.rsqrt(jnp.sum(q * q, axis=-1, keepdims=True) + NORM_EPS) * (dh ** -0.5)
    k = k * lax.rsqrt(jnp.sum(k * k, axis=-1, keepdims=True) + NORM_EPS)

    gates = gate_ref[...]
    lane = lax.broadcasted_iota(jnp.int32, gates.shape, 1)
    beta = jnp.sum(jnp.where(lane == h, gates, 0.0), axis=1, keepdims=True)
    gc = jnp.sum(jnp.where(lane == n_heads + h, gates, 0.0), axis=1, keepdims=True)
    gc_rows = gcr_ref[...]

    ii = lax.broadcasted_iota(jnp.int32, (cs, cs), 0)
    jj = lax.broadcasted_iota(jnp.int32, (cs, cs), 1)
    incl = ii >= jj
    strict = ii > jj
    eye = (ii == jj).astype(F32)
    nt = (((1,), (1,)), ((), ()))
    nn = (((1,), (0,)), ((), ()))

    e_gc = jnp.exp(gc)
    k_beta = k * beta
    v_beta = (v * beta).astype(BF16)
    kb_dec = (k_beta * e_gc).astype(BF16)
    q_dec = q * e_gc
    k_beta = k_beta.astype(BF16)
    k16 = k.astype(BF16)
    q16 = q.astype(BF16)
    chunks = [slice(c * cs, (c + 1) * cs) for c in range(nc)]

    decays, negs = [], []
    for c, sl in enumerate(chunks):
        diff = gc[sl] - gc_rows[c:c + 1, :]
        decay = jnp.where(incl, jnp.exp(jnp.where(incl, diff, 0.0)), 0.0)
        kk = lax.dot_general(k_beta[sl], k16[sl], nt, preferred_element_type=F32)
        decays.append(decay)
        negs.append(jnp.where(strict, -(kk * decay), 0.0))
    t_mats = [eye + neg for neg in negs]
    powers = negs
    span = 2
    while span < cs:
        powers = [_dot_split(p, p) for p in powers]
        t_mats = [t + _dot_split(t, p) for t, p in zip(t_mats, powers)]
        span *= 2

    stacked, n_mats, attns, sds = [], [], [], []
    for c, sl in enumerate(chunks):
        t16 = t_mats[c].astype(BF16)
        u = jnp.dot(t16, v_beta[sl], preferred_element_type=F32)
        w = jnp.dot(t16, kb_dec[sl], preferred_element_type=F32)
        qk = lax.dot_general(q16[sl], k16[sl], nt, preferred_element_type=F32)
        attns.append(jnp.where(incl, qk * decays[c], 0.0).astype(BF16))
        g_last = gc[sl][cs - 1:cs, :]
        kd_t = (k[sl] * jnp.exp(g_last - gc[sl])).T.astype(BF16)
        w16 = w.astype(BF16)
        kw = jnp.dot(kd_t, w16, preferred_element_type=F32)
        n_mats.append((u, jnp.dot(kd_t, u.astype(BF16), preferred_element_type=F32)))
        stacked.append(jnp.concatenate([w16, q_dec[sl].astype(BF16), kw.astype(BF16)], axis=0))
        sds.append(jnp.exp(g_last))

    state = state_ref[...]
    for c, sl in enumerate(chunks):
        prod = jnp.dot(stacked[c], state.astype(BF16), preferred_element_type=F32)
        u, ku = n_mats[c]
        v_new = u - prod[0:cs]
        o = prod[cs:2 * cs] + jnp.dot(attns[c], v_new.astype(BF16),
                                      preferred_element_type=F32)
        state = state * sds[c] - prod[2 * cs:] + ku
        o = o * lax.rsqrt(jnp.mean(o * o, axis=-1, keepdims=True) + NORM_EPS) * nw_ref[...]
        o_ref[sl, :] = (o * _silu(z_ref[sl, :])).astype(o_ref.dtype)
    state_ref[...] = state


def gdn_core(proj, gates, conv_w, norm_w, n_heads, ts):
    b, s, _ = proj.shape
    dh = norm_w.shape[0]
    d = n_heads * dh
    width = conv_w.shape[0]
    cs = GDN_CHUNK
    gc_rows = gates[:, :, n_heads:2 * n_heads].reshape(b, s // cs, cs, n_heads)
    gc_rows = gc_rows.transpose(0, 3, 1, 2)

    def col(off):
        return pl.BlockSpec((None, ts, dh), lambda bi, h, i: (bi, i, off + h))

    def cw(off):
        return pl.BlockSpec((width, dh), lambda bi, h, i: (0, off + h))

    return pl.pallas_call(
        functools.partial(_gdn_kernel, cs=GDN_CHUNK, width=width, n_heads=n_heads),
        grid=(b, n_heads, s // ts),
        in_specs=[col(0), col(n_heads), col(2 * n_heads), col(3 * n_heads),
                  pl.BlockSpec((None, ts, LANES), lambda bi, h, i: (bi, i, 0)),
                  pl.BlockSpec((None, None, ts // cs, cs), lambda bi, h, i: (bi, h, i, 0)),
                  cw(0), cw(n_heads), cw(2 * n_heads), _resident((1, dh))],
        out_specs=pl.BlockSpec((None, ts, dh), lambda bi, h, i: (bi, i, h)),
        out_shape=jax.ShapeDtypeStruct((b, s, d), BF16),
        scratch_shapes=[pltpu.VMEM((ts + SUBLANES, dh), F32)] * 3
                       + [pltpu.VMEM((3, SUBLANES, dh), F32), pltpu.VMEM((dh, dh), F32)],
        compiler_params=_params("parallel", "parallel", "arbitrary"),
        name="gdn_core",
    )(proj, proj, proj, proj, gates, gc_rows, conv_w, conv_w, conv_w,
      norm_w.astype(F32).reshape(1, dh))


def _row_tile(n, target):
    t = min(n, target)
    while n % t:
        t //= 2
    return t


def kernel(x, mix_norm, ffn_norm, final_norm, moba_w_qkv, moba_w_o, sconv_w_in, sconv_conv,
           sconv_w_out, gdn_w_in, gdn_conv, gdn_a_log, gdn_dt_bias, gdn_norm, gdn_w_o,
           ffn_w_up, ffn_conv, ffn_w_down):
    bsz, seq, d = x.shape
    depth = mix_norm.shape[0]
    t = bsz * seq
    tm = _row_tile(seq, 512)
    for i in range(depth):
        kind, j = i % N_MIXERS, i // N_MIXERS
        if kind == 0:
            qk, vt, k_mean = moba_proj(x, mix_norm[i], moba_w_qkv[j].astype(BF16), MOBA_HEADS,
                                       tm)
            o = moba_attention(qk, vt, k_mean.reshape(bsz, seq // MOBA_BLOCK, d), MOBA_HEADS)
            x = proj_residual(x.reshape(t, d), o.reshape(t, d), moba_w_o[j].astype(BF16),
                              tm).reshape(bsz, seq, d)
        elif kind == 1:
            x = short_conv_layer(x, mix_norm[i], sconv_w_in[j].astype(BF16), sconv_conv[j],
                                 sconv_w_out[j].astype(BF16), tm)
        else:
            n_in = gdn_w_in.shape[-1]
            w_in = jnp.pad(gdn_w_in[j], ((0, 0), (0, 4 * d + LANES - n_in))).astype(BF16)
            proj = norm_matmul(x.reshape(t, d), mix_norm[i], w_in, F32, tm)
            gates = gdn_gates(proj, gdn_a_log[j], gdn_dt_bias[j], GDN_HEADS, tm)
            o = gdn_core(proj.reshape(bsz, seq, 4 * d + LANES), gates.reshape(bsz, seq, LANES),
                         gdn_conv[j], gdn_norm[j], GDN_HEADS, _row_tile(seq, 512))
            x = proj_residual(x.reshape(t, d), o.reshape(t, d), gdn_w_o[j].astype(BF16),
                              tm).reshape(bsz, seq, d)
        x = conv_ffn_layer(x, ffn_norm[i], ffn_w_up[i].astype(BF16), ffn_conv[i],
                           ffn_w_down[i].astype(BF16),
                           final_norm if i == depth - 1 else None, tm, 256)
    return x
```

```python
import functools

import jax
import jax.numpy as jnp
from jax import lax
from jax.experimental import pallas as pl
from jax.experimental.pallas import tpu as pltpu

F32 = jnp.float32
BF16 = jnp.bfloat16

N_MIXERS = 3
MOBA_HEADS = 8
MOBA_BLOCK = 256
MOBA_TOPK = 3
GDN_HEADS = 8
GDN_CHUNK = 64
NORM_EPS = 1e-6
NEG_INF = -1e30
LOG2E = 1.4426950408889634
MOBA_GROUP = 8
MOBA_HEADS_PER_STEP = 2
GDN_HEADS_PER_STEP = 4

LANES = 128
SUBLANES = 8
BF16_SUBLANES = 16
VMEM_LIMIT = 56 * 1024 * 1024


def _params(*sem):
    return pltpu.CompilerParams(dimension_semantics=sem, vmem_limit_bytes=VMEM_LIMIT)


def _rmsnorm(x, g):
    return x * lax.rsqrt(jnp.mean(x * x, axis=-1, keepdims=True) + NORM_EPS) * g


def _silu(x):
    return x * jax.nn.sigmoid(x)


def _resident(shape):
    return pl.BlockSpec(shape, lambda *_: (0,) * len(shape))


def _norm_matmul_kernel(x_ref, g_ref, w_ref, o_ref, *km_ref, km_cols, km_rows):
    hn = _rmsnorm(x_ref[...], g_ref[...]).astype(BF16)
    y = jnp.dot(hn, w_ref[...], preferred_element_type=F32)
    o_ref[...] = y.astype(o_ref.dtype)
    if km_ref:
        lo, hi = km_cols
        yk = y[:, lo:hi]
        tm = yk.shape[0]
        km = jnp.mean(yk.reshape(tm // km_rows, km_rows, hi - lo), axis=1)
        km_ref[0][...] = km[:, None, :]


def norm_matmul(x2, g, w, out_dtype, tm, km_cols=None, km_rows=None):
    t, d = x2.shape
    n = w.shape[1]
    out_shape = [jax.ShapeDtypeStruct((t, n), out_dtype)]
    out_specs = [pl.BlockSpec((tm, n), lambda i: (i, 0))]
    if km_cols is not None:
        nk = km_cols[1] - km_cols[0]
        out_shape.append(jax.ShapeDtypeStruct((t // km_rows, 1, nk), F32))
        out_specs.append(pl.BlockSpec((tm // km_rows, 1, nk), lambda i: (i, 0, 0)))
    res = pl.pallas_call(
        functools.partial(_norm_matmul_kernel, km_cols=km_cols, km_rows=km_rows),
        grid=(t // tm,),
        in_specs=[pl.BlockSpec((tm, d), lambda i: (i, 0)),
                  _resident((1, d)),
                  _resident((d, n))],
        out_specs=out_specs,
        out_shape=out_shape,
        compiler_params=_params("parallel"),
        name="norm_matmul",
    )(x2, g.reshape(1, d), w)
    return res if km_cols is not None else res[0]


def _proj_residual_kernel(x_ref, o_ref, w_ref, out_ref):
    out_ref[...] = x_ref[...] + jnp.dot(o_ref[...], w_ref[...], preferred_element_type=F32)


def proj_residual(x2, o2, w, tm):
    t, d = x2.shape
    return pl.pallas_call(
        _proj_residual_kernel,
        grid=(t // tm,),
        in_specs=[pl.BlockSpec((tm, d), lambda i: (i, 0)),
                  pl.BlockSpec((tm, o2.shape[1]), lambda i: (i, 0)),
                  _resident(w.shape)],
        out_specs=pl.BlockSpec((tm, d), lambda i: (i, 0)),
        out_shape=jax.ShapeDtypeStruct((t, d), F32),
        compiler_params=_params("parallel"),
        name="proj_residual",
    )(x2, o2, w)


def _causal_conv_rows(buf_ref, tail_ref, cur, cw, width, tm):
    buf_ref[0:SUBLANES, :] = tail_ref[...]
    buf_ref[SUBLANES:SUBLANES + tm, :] = cur
    out = cw[width - 1:width, :] * cur
    for k in range(width - 1):
        off = SUBLANES - (width - 1) + k
        out = out + cw[k:k + 1, :] * buf_ref[off:off + tm, :]
    tail_ref[...] = cur[tm - SUBLANES:tm, :]
    return out


def _ffn_kernel(x_ref, g_ref, wup_ref, cw_ref, wdn_ref, *rest, ffn, tf, width, final):
    if final:
        fn_ref, out_ref, gbuf, ubuf, tail_ref, acc_ref = rest
    else:
        out_ref, gbuf, ubuf, tail_ref, acc_ref = rest
    tm = x_ref.shape[0]

    @pl.when(pl.program_id(1) == 0)
    def _():
        tail_ref[...] = jnp.zeros(tail_ref.shape, F32)

    x = x_ref[...]
    hn = _rmsnorm(x, g_ref[...]).astype(BF16)
    n_chunks = ffn // tf

    def down(c, act):
        y = jnp.dot(act, wdn_ref[c * tf:(c + 1) * tf, :], preferred_element_type=F32)
        if c == 0:
            acc_ref[...] = y
        else:
            acc_ref[...] += y

    act = None
    for c in range(n_chunks):
        lo = c * tf
        hi = ffn + lo
        gate = jnp.dot(hn, wup_ref[:, lo:lo + tf], preferred_element_type=F32)
        up = jnp.dot(hn, wup_ref[:, hi:hi + tf], preferred_element_type=F32)
        if act is not None:
            down(c - 1, act)
        gate = _causal_conv_rows(gbuf.at[c % 2], tail_ref.at[:, lo:lo + tf], gate,
                                 cw_ref[:, lo:lo + tf], width, tm)
        up = _causal_conv_rows(ubuf.at[c % 2], tail_ref.at[:, hi:hi + tf], up,
                               cw_ref[:, hi:hi + tf], width, tm)
        act = (_silu(gate) * up).astype(BF16)
    down(n_chunks - 1, act)
    res = x + acc_ref[...]
    if final:
        res = _rmsnorm(res, fn_ref[...])
    out_ref[...] = res


def conv_ffn_layer(x, g, w_up, conv_w, w_down, final_g, tm, tf):
    b, s, d = x.shape
    ffn = w_down.shape[0]
    width = conv_w.shape[0]
    final = final_g is not None
    row = pl.BlockSpec((None, tm, d), lambda bi, i: (bi, i, 0))
    in_specs = [row, _resident((1, d)), _resident(w_up.shape), _resident(conv_w.shape),
                _resident(w_down.shape)]
    args = [x, g.reshape(1, d), w_up, conv_w, w_down]
    if final:
        in_specs.append(_resident((1, d)))
        args.append(final_g.reshape(1, d))
    return pl.pallas_call(
        functools.partial(_ffn_kernel, ffn=ffn, tf=tf, width=width, final=final),
        grid=(b, s // tm),
        in_specs=in_specs,
        out_specs=row,
        out_shape=jax.ShapeDtypeStruct((b, s, d), F32),
        scratch_shapes=[pltpu.VMEM((2, tm + SUBLANES, tf), F32),
                        pltpu.VMEM((2, tm + SUBLANES, tf), F32),
                        pltpu.VMEM((SUBLANES, 2 * ffn), F32),
                        pltpu.VMEM((tm, d), F32)],
        compiler_params=_params("parallel", "arbitrary"),
        name="conv_ffn",
    )(*args)


def _sconv_kernel(x_ref, g_ref, win_ref, cw_ref, wout_ref, out_ref, cbuf, tail_ref, *, width):
    tm, d = x_ref.shape

    @pl.when(pl.program_id(1) == 0)
    def _():
        tail_ref[...] = jnp.zeros(tail_ref.shape, F32)

    x = x_ref[...]
    hn = _rmsnorm(x, g_ref[...]).astype(BF16)
    b_gate = jnp.dot(hn, win_ref[:, 0:d], preferred_element_type=F32)
    c_gate = jnp.dot(hn, win_ref[:, d:2 * d], preferred_element_type=F32)
    xv = jnp.dot(hn, win_ref[:, 2 * d:3 * d], preferred_element_type=F32)
    y = b_gate * _causal_conv_rows(cbuf, tail_ref, c_gate * xv, cw_ref[...], width, tm)
    out_ref[...] = x + jnp.dot(y.astype(BF16), wout_ref[...], preferred_element_type=F32)


def short_conv_layer(x, g, w_in, conv_w, w_out, tm):
    b, s, d = x.shape
    row = pl.BlockSpec((None, tm, d), lambda bi, i: (bi, i, 0))
    return pl.pallas_call(
        functools.partial(_sconv_kernel, width=conv_w.shape[0]),
        grid=(b, s // tm),
        in_specs=[row, _resident((1, d)), _resident(w_in.shape), _resident(conv_w.shape),
                  _resident(w_out.shape)],
        out_specs=row,
        out_shape=jax.ShapeDtypeStruct((b, s, d), F32),
        scratch_shapes=[pltpu.VMEM((tm + SUBLANES, d), F32), pltpu.VMEM((SUBLANES, d), F32)],
        compiler_params=_params("parallel", "arbitrary"),
        name="short_conv",
    )(x, g.reshape(1, d), w_in, conv_w, w_out)


def _moba_proj_kernel(x_ref, g_ref, w_ref, qk_ref, vt_ref, km_ref, *, blk, q_scale):
    tm, d = x_ref.shape
    n_heads, rows, _ = vt_ref.shape
    hd = d // n_heads
    hn = _rmsnorm(x_ref[...], g_ref[...]).astype(BF16)
    y = jnp.dot(hn, w_ref[...], preferred_element_type=F32)
    qk_ref[:, 0:d] = (y[:, 0:d] * q_scale).astype(BF16)
    qk_ref[:, d:2 * d] = y[:, d:2 * d].astype(BF16)
    vt_ref[:, 0:hd, :] = y[:, 2 * d:3 * d].T.reshape(n_heads, hd, tm).astype(BF16)
    vt_ref[:, hd:rows, :] = jnp.ones((n_heads, rows - hd, tm), BF16)
    km = jnp.mean(y[:, d:2 * d].reshape(tm // blk, blk, d), axis=1)
    km_ref[...] = km[:, None, :]


def moba_proj(x, g, w, n_heads, tm):
    b, s, d = x.shape
    blk = MOBA_BLOCK
    hd = d // n_heads
    rows = hd + BF16_SUBLANES
    return pl.pallas_call(
        functools.partial(_moba_proj_kernel, blk=blk, q_scale=hd ** -0.5 * LOG2E),
        grid=(b, s // tm),
        in_specs=[pl.BlockSpec((None, tm, d), lambda bi, i: (bi, i, 0)),
                  _resident((1, d)), _resident(w.shape)],
        out_specs=[pl.BlockSpec((None, tm, 2 * d), lambda bi, i: (bi, i, 0)),
                   pl.BlockSpec((None, n_heads, rows, tm), lambda bi, i: (bi, 0, 0, i)),
                   pl.BlockSpec((None, tm // blk, 1, d), lambda bi, i: (bi, i, 0, 0))],
        out_shape=[jax.ShapeDtypeStruct((b, s, 2 * d), BF16),
                   jax.ShapeDtypeStruct((b, n_heads, rows, s), BF16),
                   jax.ShapeDtypeStruct((b, s // blk, 1, d), F32)],
        compiler_params=_params("parallel", "parallel"),
        name="moba_proj",
    )(x, g.reshape(1, d), w)


def _moba_kernel(slopes_ref, q_ref, k_ref, vt_ref, km_ref, o_ref, m_ref, acc_ref,
                 u0_ref, u1_ref, top0_ref, top1_ref, *, blk, n_top, group):
    qi = pl.program_id(2)
    tq = q_ref.shape[0]
    n_heads = vt_ref.shape[0]
    hd = q_ref.shape[1] // n_heads
    n_blk = km_ref.shape[0]
    nt = (((1,), (1,)), ((), ()))
    n_groups = (qi + group) // group
    u_refs = (u0_ref, u1_ref)
    top_refs = (top0_ref, top1_ref)
    heads = range(n_heads)
    cols = [slice(a * hd, (a + 1) * hd) for a in heads]
    slope2 = [slopes_ref[pl.program_id(1) * n_heads + a] * LOG2E for a in heads]
    qs = [q_ref[:, cols[a]] for a in heads]

    kk = lax.broadcasted_iota(jnp.int32, (blk, tq), 0)
    qq = lax.broadcasted_iota(jnp.int32, (blk, tq), 1)
    rel = (kk - qq).astype(F32)
    bias = [slope2[a] * rel for a in heads]
    bias_own = [jnp.where(kk <= qq, bias[a], NEG_INF) for a in heads]

    def block_start(j):
        return pl.multiple_of(jnp.maximum(j, 0) * blk, blk)

    def score_group(g, slot):
        for t in range(group):
            j = qi - (g * group + t)
            for a in heads:
                s = lax.dot_general(k_ref[pl.ds(block_start(j), blk), cols[a]], qs[a], nt,
                                    preferred_element_type=F32)
                u = s + (jnp.where(g == 0, bias_own[a], bias[a]) if t == 0 else bias[a])
                u_refs[slot][a, t] = u
                top_refs[slot][a, t] = jnp.max(u, axis=0, keepdims=True)

    score_group(0, 0)
    for a in heads:
        m_ref[a] = jnp.full((1, tq), NEG_INF, F32)
        acc_ref[a] = jnp.zeros(acc_ref.shape[1:], F32)

    blk_ids = lax.broadcasted_iota(jnp.int32, (n_blk, tq), 0)
    sels = []
    for a in heads:
        gate = lax.dot_general(km_ref[:, cols[a]].astype(BF16), qs[a], nt,
                               preferred_element_type=F32)
        gate = jnp.where(blk_ids < qi, gate, NEG_INF)
        picks = []
        for _ in range(n_top):
            best = jnp.max(gate, axis=0, keepdims=True)
            idx = jnp.min(jnp.where(gate == best, blk_ids, n_blk), axis=0, keepdims=True)
            picks.append(jnp.where(idx < qi, idx, n_blk))
            gate = jnp.where(blk_ids == idx, -jnp.inf, gate)
        sels.append(picks)

    def consume_group(g, slot):
        for a in heads:
            m_old = m_ref[a]
            m_new = m_old
            hits, offs = [], []
            for t in range(group):
                j = qi - (g * group + t)
                hit = (sels[a][0] == j) | (j == qi)
                for sel in sels[a][1:]:
                    hit = hit | (sel == j)
                off = -slope2[a] * ((qi - j) * blk).astype(F32)
                m_new = jnp.maximum(m_new,
                                    jnp.where(hit, top_refs[slot][a, t] + off, NEG_INF))
                hits.append(hit); offs.append(off)
            acc = jnp.exp2(m_old - m_new) * acc_ref[a]
            for t in range(group):
                j = qi - (g * group + t)
                shift = jnp.where(hits[t], m_new - offs[t], -NEG_INF)
                p = jnp.exp2((u_refs[slot][a, t] - shift).astype(BF16))
                acc = acc + jnp.dot(vt_ref[a, :, pl.ds(block_start(j), blk)], p,
                                    preferred_element_type=F32)
            m_ref[a] = m_new
            acc_ref[a] = acc

    def group_body(g, carry):
        for slot in range(2):
            @pl.when(g % 2 == slot)
            def _():
                score_group(g + 1, 1 - slot)
                consume_group(g, slot)
        return carry

    lax.fori_loop(0, n_groups - 1, group_body, 0)
    for slot in range(2):
        @pl.when((n_groups - 1) % 2 == slot)
        def _():
            consume_group(n_groups - 1, slot)

    for a in heads:
        acc = acc_ref[a]
        o_ref[:, cols[a]] = (acc[0:hd] / acc[hd:hd + 1]).T.astype(o_ref.dtype)


def moba_attention(qk, vt, k_mean, n_heads):
    b, s, d2 = qk.shape
    d = d2 // 2
    hd = d // n_heads
    rows = vt.shape[2]
    blk = MOBA_BLOCK
    n_blk = s // blk
    n_top = min(MOBA_TOPK, n_blk)
    slopes = jnp.exp2(-8.0 * jnp.arange(1, n_heads + 1, dtype=F32) / n_heads)
    hps = MOBA_HEADS_PER_STEP
    wide = hps * hd
    once = pl.Buffered(1)
    grid_spec = pltpu.PrefetchScalarGridSpec(
        num_scalar_prefetch=1,
        grid=(b, n_heads // hps, n_blk),
        in_specs=[
            pl.BlockSpec((None, blk, wide), lambda bi, h, i, *_: (bi, i, h)),
            pl.BlockSpec((None, s, wide), lambda bi, h, i, *_: (bi, 0, n_heads // hps + h),
                         pipeline_mode=once),
            pl.BlockSpec((None, hps, rows, s), lambda bi, h, i, *_: (bi, h, 0, 0),
                         pipeline_mode=once),
            pl.BlockSpec((None, n_blk, wide), lambda bi, h, i, *_: (bi, 0, h)),
        ],
        out_specs=pl.BlockSpec((None, blk, wide), lambda bi, h, i, *_: (bi, i, h)),
        scratch_shapes=[pltpu.VMEM((hps, 1, blk), F32),
                        pltpu.VMEM((hps, rows, blk), F32),
                        pltpu.VMEM((hps, MOBA_GROUP, blk, blk), F32),
                        pltpu.VMEM((hps, MOBA_GROUP, blk, blk), F32),
                        pltpu.VMEM((hps, MOBA_GROUP, 1, blk), F32),
                        pltpu.VMEM((hps, MOBA_GROUP, 1, blk), F32)],
    )
    return pl.pallas_call(
        functools.partial(_moba_kernel, blk=blk, n_top=n_top, group=MOBA_GROUP),
        grid_spec=grid_spec,
        out_shape=jax.ShapeDtypeStruct((b, s, d), BF16),
        compiler_params=_params("parallel", "parallel", "arbitrary"),
        name="moba_attention",
    )(slopes, qk, qk, vt, k_mean)


def _dot_hi(a, b):
    return jnp.dot(a, b, preferred_element_type=F32, precision=lax.Precision.HIGHEST)


def _split_bf16(a):
    hi = a.astype(BF16)
    return hi, (a - hi.astype(F32)).astype(BF16)


def _dot_split(a, b):
    a_hi, a_lo = _split_bf16(a)
    b_hi, b_lo = _split_bf16(b)
    return (jnp.dot(a_hi, b_hi, preferred_element_type=F32)
            + jnp.dot(a_hi, b_lo, preferred_element_type=F32)
            + jnp.dot(a_lo, b_hi, preferred_element_type=F32))


def _gdn_gates_kernel(sm_ref, alog_ref, dt_ref, o_ref, *, cs, n_heads):
    small = sm_ref[...]
    tm = small.shape[0]
    sp = small + dt_ref[...]
    softplus = jnp.maximum(sp, 0.0) + jnp.log1p(jnp.exp(-jnp.abs(sp)))
    g = -jnp.exp(alog_ref[...]) * softplus
    ii = lax.broadcasted_iota(jnp.int32, (tm, tm), 0)
    jj = lax.broadcasted_iota(jnp.int32, (tm, tm), 1)
    same_chunk_prefix = ((ii // cs) == (jj // cs)) & (ii >= jj)
    gc = _dot_hi(same_chunk_prefix.astype(F32), g)
    lane = lax.broadcasted_iota(jnp.int32, small.shape, 1)
    o_ref[...] = jnp.where(lane < n_heads, jax.nn.sigmoid(small), gc)


def gdn_gates(proj2, a_log, dt_bias, n_heads, tm):
    t, n = proj2.shape
    last = n // LANES - 1
    pad = (n_heads, LANES - 2 * n_heads)
    alog_vec = jnp.pad(a_log.astype(F32), pad).reshape(1, LANES)
    dt_vec = jnp.pad(dt_bias.astype(F32), pad).reshape(1, LANES)
    return pl.pallas_call(
        functools.partial(_gdn_gates_kernel, cs=GDN_CHUNK, n_heads=n_heads),
        grid=(t // tm,),
        in_specs=[pl.BlockSpec((tm, LANES), lambda i: (i, last)),
                  _resident((1, LANES)), _resident((1, LANES))],
        out_specs=pl.BlockSpec((tm, LANES), lambda i: (i, 0)),
        out_shape=jax.ShapeDtypeStruct((t, LANES), F32),
        compiler_params=_params("parallel"),
        name="gdn_gates",
    )(proj2, alog_vec, dt_vec)


def _gdn_kernel(q_ref, k_ref, v_ref, z_ref, gate_ref, gcr_ref, cwq_ref, cwk_ref, cwv_ref,
                nw_ref, o_ref, qbuf, kbuf, vbuf, tail_ref, state_ref, *, cs, width, n_heads):
    ts = q_ref.shape[0]
    hps = state_ref.shape[0]
    dh = q_ref.shape[1] // hps
    nc = ts // cs
    heads = range(hps)
    cols = [slice(a * dh, (a + 1) * dh) for a in heads]
    chunks = [slice(c * cs, (c + 1) * cs) for c in range(nc)]
    pairs = [(a, c) for a in heads for c in range(nc)]
    head0 = pl.program_id(1) * hps

    @pl.when(pl.program_id(2) == 0)
    def _():
        tail_ref[...] = jnp.zeros(tail_ref.shape, F32)
        state_ref[...] = jnp.zeros(state_ref.shape, F32)

    q_all = _silu(_causal_conv_rows(qbuf, tail_ref.at[0], q_ref[...], cwq_ref[...], width, ts))
    k_all = _silu(_causal_conv_rows(kbuf, tail_ref.at[1], k_ref[...], cwk_ref[...], width, ts))
    v_all = _silu(_causal_conv_rows(vbuf, tail_ref.at[2], v_ref[...], cwv_ref[...], width, ts))

    gates = gate_ref[...]
    lane = lax.broadcasted_iota(jnp.int32, gates.shape, 1)

    ii = lax.broadcasted_iota(jnp.int32, (cs, cs), 0)
    jj = lax.broadcasted_iota(jnp.int32, (cs, cs), 1)
    incl = ii >= jj
    strict = ii > jj
    eye = (ii == jj).astype(F32)
    nt = (((1,), (1,)), ((), ()))

    ks, gcs, k16s, q16s, v_betas, kb_decs, q_decs, k_betas = [], [], [], [], [], [], [], []
    for a in heads:
        q = q_all[:, cols[a]]
        k = k_all[:, cols[a]]
        q = q * lax.rsqrt(jnp.sum(q * q, axis=-1, keepdims=True) + NORM_EPS) * (dh ** -0.5)
        k = k * lax.rsqrt(jnp.sum(k * k, axis=-1, keepdims=True) + NORM_EPS)
        beta = jnp.sum(jnp.where(lane == head0 + a, gates, 0.0), axis=1, keepdims=True)
        gc = jnp.sum(jnp.where(lane == n_heads + head0 + a, gates, 0.0), axis=1,
                     keepdims=True)
        e_gc = jnp.exp(gc)
        k_beta = k * beta
        ks.append(k); gcs.append(gc)
        v_betas.append((v_all[:, cols[a]] * beta).astype(BF16))
        kb_decs.append((k_beta * e_gc).astype(BF16))
        q_decs.append((q * e_gc).astype(BF16))
        k_betas.append(k_beta.astype(BF16))
        k16s.append(k.astype(BF16))
        q16s.append(q.astype(BF16))

    decays, negs = {}, {}
    for a, c in pairs:
        sl = chunks[c]
        diff = gcs[a][sl] - gcr_ref[a, c:c + 1, :]
        decay = jnp.where(incl, jnp.exp(jnp.where(incl, diff, 0.0)), 0.0)
        kk = lax.dot_general(k_betas[a][sl], k16s[a][sl], nt, preferred_element_type=F32)
        decays[a, c] = decay
        negs[a, c] = jnp.where(strict, -(kk * decay), 0.0)
    t_mats = {p: eye + negs[p] for p in pairs}
    powers = negs
    span = 2
    while span < cs:
        powers = {p: _dot_split(powers[p], powers[p]) for p in pairs}
        t_mats = {p: t_mats[p] + _dot_split(t_mats[p], powers[p]) for p in pairs}
        span *= 2

    us, w16s, kus, kws, wqs, attns, sds = {}, {}, {}, {}, {}, {}, {}
    for a, c in pairs:
        sl = chunks[c]
        t16 = t_mats[a, c].astype(BF16)
        us[a, c] = jnp.dot(t16, v_betas[a][sl], preferred_element_type=F32)
        w16s[a, c] = jnp.dot(t16, kb_decs[a][sl], preferred_element_type=F32).astype(BF16)
        qk = lax.dot_general(q16s[a][sl], k16s[a][sl], nt, preferred_element_type=F32)
        attns[a, c] = jnp.where(incl, qk * decays[a, c], 0.0).astype(BF16)
    for a, c in pairs:
        sl = chunks[c]
        gc = gcs[a][sl]
        g_last = gc[cs - 1:cs, :]
        kd_t = (ks[a][sl] * jnp.exp(g_last - gc)).T.astype(BF16)
        kws[a, c] = jnp.dot(kd_t, w16s[a, c], preferred_element_type=F32).astype(BF16)
        kus[a, c] = jnp.dot(kd_t, us[a, c].astype(BF16), preferred_element_type=F32)
        wqs[a, c] = jnp.concatenate([w16s[a, c], q_decs[a][sl]], axis=0)
        sds[a, c] = jnp.exp(g_last)

    def emit_output(c, a, inter, v_new):
        sl = chunks[c]
        o = inter + jnp.dot(attns[a, c], v_new.astype(BF16), preferred_element_type=F32)
        o = o * lax.rsqrt(jnp.mean(o * o, axis=-1, keepdims=True) + NORM_EPS) * nw_ref[...]
        o_ref[sl, cols[a]] = (o * _silu(z_ref[sl, cols[a]])).astype(o_ref.dtype)

    states = [state_ref[a] for a in heads]
    pending = None
    for c in range(nc):
        prods = []
        for a in heads:
            s16 = states[a].astype(BF16)
            prods.append((jnp.dot(kws[a, c], s16, preferred_element_type=F32),
                          jnp.dot(wqs[a, c], s16, preferred_element_type=F32)))
        if pending is not None:
            for a in heads:
                emit_output(c - 1, a, *pending[a])
        pending = []
        for a in heads:
            decayed, prod = prods[a]
            states[a] = states[a] * sds[a, c] - decayed + kus[a, c]
            pending.append((prod[cs:2 * cs], us[a, c] - prod[0:cs]))
    for a in heads:
        emit_output(nc - 1, a, *pending[a])
        state_ref[a] = states[a]


def gdn_core(proj, gates, conv_w, norm_w, n_heads, ts):
    b, s, _ = proj.shape
    dh = norm_w.shape[0]
    d = n_heads * dh
    width = conv_w.shape[0]
    cs = GDN_CHUNK
    gc_rows = gates[:, :, n_heads:2 * n_heads].reshape(b, s // cs, cs, n_heads)
    gc_rows = gc_rows.transpose(0, 3, 1, 2)

    hps = GDN_HEADS_PER_STEP
    wide = hps * dh
    groups = n_heads // hps

    def col(section):
        return pl.BlockSpec((None, ts, wide), lambda bi, h, i: (bi, i, section * groups + h))

    def cw(section):
        return pl.BlockSpec((width, wide), lambda bi, h, i: (0, section * groups + h))

    return pl.pallas_call(
        functools.partial(_gdn_kernel, cs=GDN_CHUNK, width=width, n_heads=n_heads),
        grid=(b, groups, s // ts),
        in_specs=[col(0), col(1), col(2), col(3),
                  pl.BlockSpec((None, ts, LANES), lambda bi, h, i: (bi, i, 0)),
                  pl.BlockSpec((None, hps, ts // cs, cs), lambda bi, h, i: (bi, h, i, 0)),
                  cw(0), cw(1), cw(2), _resident((1, dh))],
        out_specs=pl.BlockSpec((None, ts, wide), lambda bi, h, i: (bi, i, h)),
        out_shape=jax.ShapeDtypeStruct((b, s, d), BF16),
        scratch_shapes=[pltpu.VMEM((ts + SUBLANES, wide), F32)] * 3
                       + [pltpu.VMEM((3, SUBLANES, wide), F32),
                          pltpu.VMEM((hps, dh, dh), F32)],
        compiler_params=_params("parallel", "parallel", "arbitrary"),
        name="gdn_core",
    )(proj, proj, proj, proj, gates, gc_rows, conv_w, conv_w, conv_w,
      norm_w.astype(F32).reshape(1, dh))


def _row_tile(n, target):
    t = min(n, target)
    while n % t:
        t //= 2
    return t


def kernel(x, mix_norm, ffn_norm, final_norm, moba_w_qkv, moba_w_o, sconv_w_in, sconv_conv,
           sconv_w_out, gdn_w_in, gdn_conv, gdn_a_log, gdn_dt_bias, gdn_norm, gdn_w_o,
           ffn_w_up, ffn_conv, ffn_w_down):
    bsz, seq, d = x.shape
    depth = mix_norm.shape[0]
    t = bsz * seq
    tm = _row_tile(seq, 512)
    for i in range(depth):
        kind, j = i % N_MIXERS, i // N_MIXERS
        if kind == 0:
            qk, vt, k_mean = moba_proj(x, mix_norm[i], moba_w_qkv[j].astype(BF16), MOBA_HEADS,
                                       tm)
            o = moba_attention(qk, vt, k_mean.reshape(bsz, seq // MOBA_BLOCK, d), MOBA_HEADS)
            x = proj_residual(x.reshape(t, d), o.reshape(t, d), moba_w_o[j].astype(BF16),
                              tm).reshape(bsz, seq, d)
        elif kind == 1:
            x = short_conv_layer(x, mix_norm[i], sconv_w_in[j].astype(BF16), sconv_conv[j],
                                 sconv_w_out[j].astype(BF16), tm)
        else:
            n_in = gdn_w_in.shape[-1]
            w_in = jnp.pad(gdn_w_in[j], ((0, 0), (0, 4 * d + LANES - n_in))).astype(BF16)
            proj = norm_matmul(x.reshape(t, d), mix_norm[i], w_in, F32, tm)
            gates = gdn_gates(proj, gdn_a_log[j], gdn_dt_bias[j], GDN_HEADS, tm)
            o = gdn_core(proj.reshape(bsz, seq, 4 * d + LANES), gates.reshape(bsz, seq, LANES),
                         gdn_conv[j], gdn_norm[j], GDN_HEADS, _row_tile(seq, 512))
            x = proj_residual(x.reshape(t, d), o.reshape(t, d), gdn_w_o[j].astype(BF16),
                              tm).reshape(bsz, seq, d)
        x = conv_ffn_layer(x, ffn_norm[i], ffn_w_up[i].astype(BF16), ffn_conv[i],
                           ffn_w_down[i].astype(BF16),
                           final_norm if i == depth - 1 else None, tm, 256)
    return x
```

```python
import functools

import jax
import jax.numpy as jnp
from jax import lax
from jax.experimental import pallas as pl
from jax.experimental.pallas import tpu as pltpu

F32 = jnp.float32
BF16 = jnp.bfloat16

N_MIXERS = 3
MOBA_HEADS = 8
MOBA_BLOCK = 256
MOBA_TOPK = 3
GDN_HEADS = 8
GDN_CHUNK = 64
NORM_EPS = 1e-6
NEG_INF = -1e30
LOG2E = 1.4426950408889634
MOBA_GROUP = 4
MOBA_HEADS_PER_STEP = 4
GDN_HEADS_PER_STEP = 4

LANES = 128
SUBLANES = 8
BF16_SUBLANES = 16
VMEM_LIMIT = 56 * 1024 * 1024


def _params(*sem):
    return pltpu.CompilerParams(dimension_semantics=sem, vmem_limit_bytes=VMEM_LIMIT)


def _rmsnorm(x, g):
    return x * lax.rsqrt(jnp.mean(x * x, axis=-1, keepdims=True) + NORM_EPS) * g


def _silu(x):
    return x * jax.nn.sigmoid(x)


def _resident(shape):
    return pl.BlockSpec(shape, lambda *_: (0,) * len(shape))


def _norm_matmul_kernel(x_ref, g_ref, w_ref, o_ref, *km_ref, km_cols, km_rows):
    hn = _rmsnorm(x_ref[...], g_ref[...]).astype(BF16)
    y = jnp.dot(hn, w_ref[...], preferred_element_type=F32)
    o_ref[...] = y.astype(o_ref.dtype)
    if km_ref:
        lo, hi = km_cols
        yk = y[:, lo:hi]
        tm = yk.shape[0]
        km = jnp.mean(yk.reshape(tm // km_rows, km_rows, hi - lo), axis=1)
        km_ref[0][...] = km[:, None, :]


def norm_matmul(x2, g, w, out_dtype, tm, km_cols=None, km_rows=None):
    t, d = x2.shape
    n = w.shape[1]
    out_shape = [jax.ShapeDtypeStruct((t, n), out_dtype)]
    out_specs = [pl.BlockSpec((tm, n), lambda i: (i, 0))]
    if km_cols is not None:
        nk = km_cols[1] - km_cols[0]
        out_shape.append(jax.ShapeDtypeStruct((t // km_rows, 1, nk), F32))
        out_specs.append(pl.BlockSpec((tm // km_rows, 1, nk), lambda i: (i, 0, 0)))
    res = pl.pallas_call(
        functools.partial(_norm_matmul_kernel, km_cols=km_cols, km_rows=km_rows),
        grid=(t // tm,),
        in_specs=[pl.BlockSpec((tm, d), lambda i: (i, 0)),
                  _resident((1, d)),
                  _resident((d, n))],
        out_specs=out_specs,
        out_shape=out_shape,
        compiler_params=_params("parallel"),
        name="norm_matmul",
    )(x2, g.reshape(1, d), w)
    return res if km_cols is not None else res[0]


def _proj_residual_kernel(x_ref, o_ref, w_ref, out_ref):
    out_ref[...] = x_ref[...] + jnp.dot(o_ref[...], w_ref[...], preferred_element_type=F32)


def proj_residual(x2, o2, w, tm):
    t, d = x2.shape
    return pl.pallas_call(
        _proj_residual_kernel,
        grid=(t // tm,),
        in_specs=[pl.BlockSpec((tm, d), lambda i: (i, 0)),
                  pl.BlockSpec((tm, o2.shape[1]), lambda i: (i, 0)),
                  _resident(w.shape)],
        out_specs=pl.BlockSpec((tm, d), lambda i: (i, 0)),
        out_shape=jax.ShapeDtypeStruct((t, d), F32),
        compiler_params=_params("parallel"),
        name="proj_residual",
    )(x2, o2, w)


def _causal_conv_rows(buf_ref, tail_ref, cur, cw, width, tm):
    buf_ref[0:SUBLANES, :] = tail_ref[...]
    buf_ref[SUBLANES:SUBLANES + tm, :] = cur
    out = cw[width - 1:width, :] * cur
    for k in range(width - 1):
        off = SUBLANES - (width - 1) + k
        out = out + cw[k:k + 1, :] * buf_ref[off:off + tm, :]
    tail_ref[...] = cur[tm - SUBLANES:tm, :]
    return out


def _ffn_kernel(x_ref, g_ref, wup_ref, cw_ref, wdn_ref, *rest, ffn, tf, width, final):
    if final:
        fn_ref, out_ref, gbuf, ubuf, tail_ref, acc_ref = rest
    else:
        out_ref, gbuf, ubuf, tail_ref, acc_ref = rest
    tm = x_ref.shape[0]

    @pl.when(pl.program_id(1) == 0)
    def _():
        tail_ref[...] = jnp.zeros(tail_ref.shape, F32)

    x = x_ref[...]
    hn = _rmsnorm(x, g_ref[...]).astype(BF16)
    n_chunks = ffn // tf

    def down(c, act):
        y = jnp.dot(act, wdn_ref[c * tf:(c + 1) * tf, :], preferred_element_type=F32)
        if c == 0:
            acc_ref[...] = y
        else:
            acc_ref[...] += y

    act = None
    for c in range(n_chunks):
        lo = c * tf
        hi = ffn + lo
        gate = jnp.dot(hn, wup_ref[:, lo:lo + tf], preferred_element_type=F32)
        up = jnp.dot(hn, wup_ref[:, hi:hi + tf], preferred_element_type=F32)
        if act is not None:
            down(c - 1, act)
        gate = _causal_conv_rows(gbuf.at[c % 2], tail_ref.at[:, lo:lo + tf], gate,
                                 cw_ref[:, lo:lo + tf], width, tm)
        up = _causal_conv_rows(ubuf.at[c % 2], tail_ref.at[:, hi:hi + tf], up,
                               cw_ref[:, hi:hi + tf], width, tm)
        act = (_silu(gate) * up).astype(BF16)
    down(n_chunks - 1, act)
    res = x + acc_ref[...]
    if final:
        res = _rmsnorm(res, fn_ref[...])
    out_ref[...] = res


def conv_ffn_layer(x, g, w_up, conv_w, w_down, final_g, tm, tf):
    b, s, d = x.shape
    ffn = w_down.shape[0]
    width = conv_w.shape[0]
    final = final_g is not None
    row = pl.BlockSpec((None, tm, d), lambda bi, i: (bi, i, 0))
    in_specs = [row, _resident((1, d)), _resident(w_up.shape), _resident(conv_w.shape),
                _resident(w_down.shape)]
    args = [x, g.reshape(1, d), w_up, conv_w, w_down]
    if final:
        in_specs.append(_resident((1, d)))
        args.append(final_g.reshape(1, d))
    return pl.pallas_call(
        functools.partial(_ffn_kernel, ffn=ffn, tf=tf, width=width, final=final),
        grid=(b, s // tm),
        in_specs=in_specs,
        out_specs=row,
        out_shape=jax.ShapeDtypeStruct((b, s, d), F32),
        scratch_shapes=[pltpu.VMEM((2, tm + SUBLANES, tf), F32),
                        pltpu.VMEM((2, tm + SUBLANES, tf), F32),
                        pltpu.VMEM((SUBLANES, 2 * ffn), F32),
                        pltpu.VMEM((tm, d), F32)],
        compiler_params=_params("parallel", "arbitrary"),
        name="conv_ffn",
    )(*args)


def _sconv_kernel(x_ref, g_ref, win_ref, cw_ref, wout_ref, out_ref, cbuf, tail_ref, *, width):
    tm, d = x_ref.shape

    @pl.when(pl.program_id(1) == 0)
    def _():
        tail_ref[...] = jnp.zeros(tail_ref.shape, F32)

    x = x_ref[...]
    hn = _rmsnorm(x, g_ref[...]).astype(BF16)
    b_gate = jnp.dot(hn, win_ref[:, 0:d], preferred_element_type=F32)
    c_gate = jnp.dot(hn, win_ref[:, d:2 * d], preferred_element_type=F32)
    xv = jnp.dot(hn, win_ref[:, 2 * d:3 * d], preferred_element_type=F32)
    y = b_gate * _causal_conv_rows(cbuf, tail_ref, c_gate * xv, cw_ref[...], width, tm)
    out_ref[...] = x + jnp.dot(y.astype(BF16), wout_ref[...], preferred_element_type=F32)


def short_conv_layer(x, g, w_in, conv_w, w_out, tm):
    b, s, d = x.shape
    row = pl.BlockSpec((None, tm, d), lambda bi, i: (bi, i, 0))
    return pl.pallas_call(
        functools.partial(_sconv_kernel, width=conv_w.shape[0]),
        grid=(b, s // tm),
        in_specs=[row, _resident((1, d)), _resident(w_in.shape), _resident(conv_w.shape),
                  _resident(w_out.shape)],
        out_specs=row,
        out_shape=jax.ShapeDtypeStruct((b, s, d), F32),
        scratch_shapes=[pltpu.VMEM((tm + SUBLANES, d), F32), pltpu.VMEM((SUBLANES, d), F32)],
        compiler_params=_params("parallel", "arbitrary"),
        name="short_conv",
    )(x, g.reshape(1, d), w_in, conv_w, w_out)


def _moba_proj_kernel(x_ref, g_ref, w_ref, qk_ref, vt_ref, km_ref, *, blk, q_scale):
    tm, d = x_ref.shape
    n_heads, rows, _ = vt_ref.shape
    hd = d // n_heads
    hn = _rmsnorm(x_ref[...], g_ref[...]).astype(BF16)
    y = jnp.dot(hn, w_ref[...], preferred_element_type=F32)
    qk_ref[:, 0:d] = (y[:, 0:d] * q_scale).astype(BF16)
    qk_ref[:, d:2 * d] = y[:, d:2 * d].astype(BF16)
    vt_ref[:, 0:hd, :] = y[:, 2 * d:3 * d].T.reshape(n_heads, hd, tm).astype(BF16)
    vt_ref[:, hd:rows, :] = jnp.ones((n_heads, rows - hd, tm), BF16)
    km = jnp.mean(y[:, d:2 * d].reshape(tm // blk, blk, d), axis=1)
    km_ref[...] = km[:, None, :]


def moba_proj(x, g, w, n_heads, tm):
    b, s, d = x.shape
    blk = MOBA_BLOCK
    hd = d // n_heads
    rows = hd + BF16_SUBLANES
    return pl.pallas_call(
        functools.partial(_moba_proj_kernel, blk=blk, q_scale=hd ** -0.5 * LOG2E),
        grid=(b, s // tm),
        in_specs=[pl.BlockSpec((None, tm, d), lambda bi, i: (bi, i, 0)),
                  _resident((1, d)), _resident(w.shape)],
        out_specs=[pl.BlockSpec((None, tm, 2 * d), lambda bi, i: (bi, i, 0)),
                   pl.BlockSpec((None, n_heads, rows, tm), lambda bi, i: (bi, 0, 0, i)),
                   pl.BlockSpec((None, tm // blk, 1, d), lambda bi, i: (bi, i, 0, 0))],
        out_shape=[jax.ShapeDtypeStruct((b, s, 2 * d), BF16),
                   jax.ShapeDtypeStruct((b, n_heads, rows, s), BF16),
                   jax.ShapeDtypeStruct((b, s // blk, 1, d), F32)],
        compiler_params=_params("parallel", "parallel"),
        name="moba_proj",
    )(x, g.reshape(1, d), w)


def _moba_kernel(slopes_ref, q_ref, k_ref, vt_ref, km_ref, o_ref, m_ref, acc_ref,
                 u0_ref, u1_ref, top0_ref, top1_ref, *, blk, n_top, group):
    qi = pl.program_id(2)
    tq = q_ref.shape[0]
    n_heads = vt_ref.shape[0]
    hd = q_ref.shape[1] // n_heads
    n_blk = km_ref.shape[0]
    nt = (((1,), (1,)), ((), ()))
    n_groups = (qi + group) // group
    u_refs = (u0_ref, u1_ref)
    top_refs = (top0_ref, top1_ref)
    heads = range(n_heads)
    cols = [slice(a * hd, (a + 1) * hd) for a in heads]
    slope2 = [slopes_ref[pl.program_id(1) * n_heads + a] * LOG2E for a in heads]
    qs = [q_ref[:, cols[a]] for a in heads]

    kk = lax.broadcasted_iota(jnp.int32, (blk, tq), 0)
    qq = lax.broadcasted_iota(jnp.int32, (blk, tq), 1)
    rel = (kk - qq).astype(F32)
    bias = [slope2[a] * rel for a in heads]
    bias_own = [jnp.where(kk <= qq, bias[a], NEG_INF) for a in heads]

    def block_start(j):
        return pl.multiple_of(jnp.maximum(j, 0) * blk, blk)

    def score_group(g, slot):
        for t in range(group):
            j = qi - (g * group + t)
            for a in heads:
                s = lax.dot_general(k_ref[pl.ds(block_start(j), blk), cols[a]], qs[a], nt,
                                    preferred_element_type=F32)
                u = s + (jnp.where(g == 0, bias_own[a], bias[a]) if t == 0 else bias[a])
                u_refs[slot][a, t] = u
                top_refs[slot][a, t] = jnp.max(u, axis=0, keepdims=True)

    score_group(0, 0)
    for a in heads:
        m_ref[a] = jnp.full((1, tq), NEG_INF, F32)
        acc_ref[a] = jnp.zeros(acc_ref.shape[1:], F32)

    blk_ids = lax.broadcasted_iota(jnp.int32, (n_blk, tq), 0)
    sels = []
    for a in heads:
        gate = lax.dot_general(km_ref[:, cols[a]].astype(BF16), qs[a], nt,
                               preferred_element_type=F32)
        gate = jnp.where(blk_ids < qi, gate, NEG_INF)
        picks = []
        for _ in range(n_top):
            best = jnp.max(gate, axis=0, keepdims=True)
            idx = jnp.min(jnp.where(gate == best, blk_ids, n_blk), axis=0, keepdims=True)
            picks.append(jnp.where(idx < qi, idx, n_blk))
            gate = jnp.where(blk_ids == idx, -jnp.inf, gate)
        sels.append(picks)

    def consume_group(g, slot):
        for a in heads:
            m_old = m_ref[a]
            m_new = m_old
            hits, offs = [], []
            for t in range(group):
                j = qi - (g * group + t)
                hit = (sels[a][0] == j) | (j == qi)
                for sel in sels[a][1:]:
                    hit = hit | (sel == j)
                off = -slope2[a] * ((qi - j) * blk).astype(F32)
                m_new = jnp.maximum(m_new,
                                    jnp.where(hit, top_refs[slot][a, t] + off, NEG_INF))
                hits.append(hit); offs.append(off)
            acc = jnp.exp2(m_old - m_new) * acc_ref[a]
            for t in range(group):
                j = qi - (g * group + t)
                shift = jnp.where(hits[t], m_new - offs[t], -NEG_INF)
                p = jnp.exp2((u_refs[slot][a, t] - shift).astype(BF16))
                acc = acc + jnp.dot(vt_ref[a, :, pl.ds(block_start(j), blk)], p,
                                    preferred_element_type=F32)
            m_ref[a] = m_new
            acc_ref[a] = acc

    def group_body(g, carry):
        for slot in range(2):
            @pl.when(g % 2 == slot)
            def _():
                score_group(g + 1, 1 - slot)
                consume_group(g, slot)
        return carry

    lax.fori_loop(0, n_groups - 1, group_body, 0)
    for slot in range(2):
        @pl.when((n_groups - 1) % 2 == slot)
        def _():
            consume_group(n_groups - 1, slot)

    for a in heads:
        acc = acc_ref[a]
        o_ref[:, cols[a]] = (acc[0:hd] / acc[hd:hd + 1]).T.astype(o_ref.dtype)


def moba_attention(qk, vt, k_mean, n_heads):
    b, s, d2 = qk.shape
    d = d2 // 2
    hd = d // n_heads
    rows = vt.shape[2]
    blk = MOBA_BLOCK
    n_blk = s // blk
    n_top = min(MOBA_TOPK, n_blk)
    slopes = jnp.exp2(-8.0 * jnp.arange(1, n_heads + 1, dtype=F32) / n_heads)
    hps = MOBA_HEADS_PER_STEP
    wide = hps * hd
    once = pl.Buffered(1)
    grid_spec = pltpu.PrefetchScalarGridSpec(
        num_scalar_prefetch=1,
        grid=(b, n_heads // hps, n_blk),
        in_specs=[
            pl.BlockSpec((None, blk, wide), lambda bi, h, i, *_: (bi, i, h)),
            pl.BlockSpec((None, s, wide), lambda bi, h, i, *_: (bi, 0, n_heads // hps + h),
                         pipeline_mode=once),
            pl.BlockSpec((None, hps, rows, s), lambda bi, h, i, *_: (bi, h, 0, 0),
                         pipeline_mode=once),
            pl.BlockSpec((None, n_blk, wide), lambda bi, h, i, *_: (bi, 0, h)),
        ],
        out_specs=pl.BlockSpec((None, blk, wide), lambda bi, h, i, *_: (bi, i, h)),
        scratch_shapes=[pltpu.VMEM((hps, 1, blk), F32),
                        pltpu.VMEM((hps, rows, blk), F32),
                        pltpu.VMEM((hps, MOBA_GROUP, blk, blk), F32),
                        pltpu.VMEM((hps, MOBA_GROUP, blk, blk), F32),
                        pltpu.VMEM((hps, MOBA_GROUP, 1, blk), F32),
                        pltpu.VMEM((hps, MOBA_GROUP, 1, blk), F32)],
    )
    return pl.pallas_call(
        functools.partial(_moba_kernel, blk=blk, n_top=n_top, group=MOBA_GROUP),
        grid_spec=grid_spec,
        out_shape=jax.ShapeDtypeStruct((b, s, d), BF16),
        compiler_params=_params("parallel", "parallel", "arbitrary"),
        name="moba_attention",
    )(slopes, qk, qk, vt, k_mean)


def _dot_hi(a, b):
    return jnp.dot(a, b, preferred_element_type=F32, precision=lax.Precision.HIGHEST)


def _split_bf16(a):
    hi = a.astype(BF16)
    return hi, (a - hi.astype(F32)).astype(BF16)


def _dot_split(a, b):
    a_hi, a_lo = _split_bf16(a)
    b_hi, b_lo = _split_bf16(b)
    return (jnp.dot(a_hi, b_hi, preferred_element_type=F32)
            + jnp.dot(a_hi, b_lo, preferred_element_type=F32)
            + jnp.dot(a_lo, b_hi, preferred_element_type=F32))


def _gdn_gates_kernel(sm_ref, alog_ref, dt_ref, o_ref, *, cs, n_heads):
    small = sm_ref[...]
    tm = small.shape[0]
    sp = small + dt_ref[...]
    softplus = jnp.maximum(sp, 0.0) + jnp.log1p(jnp.exp(-jnp.abs(sp)))
    g = -jnp.exp(alog_ref[...]) * softplus
    ii = lax.broadcasted_iota(jnp.int32, (tm, tm), 0)
    jj = lax.broadcasted_iota(jnp.int32, (tm, tm), 1)
    same_chunk_prefix = ((ii // cs) == (jj // cs)) & (ii >= jj)
    gc = _dot_hi(same_chunk_prefix.astype(F32), g)
    lane = lax.broadcasted_iota(jnp.int32, small.shape, 1)
    o_ref[...] = jnp.where(lane < n_heads, jax.nn.sigmoid(small), gc)


def gdn_gates(proj2, a_log, dt_bias, n_heads, tm):
    t, n = proj2.shape
    last = n // LANES - 1
    pad = (n_heads, LANES - 2 * n_heads)
    alog_vec = jnp.pad(a_log.astype(F32), pad).reshape(1, LANES)
    dt_vec = jnp.pad(dt_bias.astype(F32), pad).reshape(1, LANES)
    return pl.pallas_call(
        functools.partial(_gdn_gates_kernel, cs=GDN_CHUNK, n_heads=n_heads),
        grid=(t // tm,),
        in_specs=[pl.BlockSpec((tm, LANES), lambda i: (i, last)),
                  _resident((1, LANES)), _resident((1, LANES))],
        out_specs=pl.BlockSpec((tm, LANES), lambda i: (i, 0)),
        out_shape=jax.ShapeDtypeStruct((t, LANES), F32),
        compiler_params=_params("parallel"),
        name="gdn_gates",
    )(proj2, alog_vec, dt_vec)


def _gdn_kernel(q_ref, k_ref, v_ref, z_ref, gate_ref, gcr_ref, cwq_ref, cwk_ref, cwv_ref,
                nw_ref, o_ref, qbuf, kbuf, vbuf, tail_ref, state_ref, *, cs, width, n_heads):
    ts = q_ref.shape[0]
    hps = state_ref.shape[0]
    dh = q_ref.shape[1] // hps
    nc = ts // cs
    heads = range(hps)
    cols = [slice(a * dh, (a + 1) * dh) for a in heads]
    chunks = [slice(c * cs, (c + 1) * cs) for c in range(nc)]
    pairs = [(a, c) for a in heads for c in range(nc)]
    head0 = pl.program_id(1) * hps

    @pl.when(pl.program_id(2) == 0)
    def _():
        tail_ref[...] = jnp.zeros(tail_ref.shape, F32)
        state_ref[...] = jnp.zeros(state_ref.shape, F32)

    q_all = _silu(_causal_conv_rows(qbuf, tail_ref.at[0], q_ref[...], cwq_ref[...], width, ts))
    k_all = _silu(_causal_conv_rows(kbuf, tail_ref.at[1], k_ref[...], cwk_ref[...], width, ts))
    v_all = _silu(_causal_conv_rows(vbuf, tail_ref.at[2], v_ref[...], cwv_ref[...], width, ts))

    gates = gate_ref[...]
    lane = lax.broadcasted_iota(jnp.int32, gates.shape, 1)

    ii = lax.broadcasted_iota(jnp.int32, (cs, cs), 0)
    jj = lax.broadcasted_iota(jnp.int32, (cs, cs), 1)
    incl = ii >= jj
    strict = ii > jj
    eye = (ii == jj).astype(F32)
    nt = (((1,), (1,)), ((), ()))

    ks, gcs, k16s, q16s, v_betas, kb_decs, q_decs, k_betas = [], [], [], [], [], [], [], []
    for a in heads:
        q = q_all[:, cols[a]]
        k = k_all[:, cols[a]]
        q = q * lax.rsqrt(jnp.sum(q * q, axis=-1, keepdims=True) + NORM_EPS) * (dh ** -0.5)
        k = k * lax.rsqrt(jnp.sum(k * k, axis=-1, keepdims=True) + NORM_EPS)
        beta = jnp.sum(jnp.where(lane == head0 + a, gates, 0.0), axis=1, keepdims=True)
        gc = jnp.sum(jnp.where(lane == n_heads + head0 + a, gates, 0.0), axis=1,
                     keepdims=True)
        e_gc = jnp.exp(gc)
        k_beta = k * beta
        ks.append(k); gcs.append(gc)
        v_betas.append((v_all[:, cols[a]] * beta).astype(BF16))
        kb_decs.append((k_beta * e_gc).astype(BF16))
        q_decs.append((q * e_gc).astype(BF16))
        k_betas.append(k_beta.astype(BF16))
        k16s.append(k.astype(BF16))
        q16s.append(q.astype(BF16))

    decays, negs = {}, {}
    for a, c in pairs:
        sl = chunks[c]
        diff = gcs[a][sl] - gcr_ref[a, c:c + 1, :]
        decay = jnp.where(incl, jnp.exp(jnp.where(incl, diff, 0.0)), 0.0)
        kk = lax.dot_general(k_betas[a][sl], k16s[a][sl], nt, preferred_element_type=F32)
        decays[a, c] = decay
        negs[a, c] = jnp.where(strict, -(kk * decay), 0.0)
    t_mats = {p: eye + negs[p] for p in pairs}
    powers = negs
    span = 2
    while span < cs:
        powers = {p: _dot_split(powers[p], powers[p]) for p in pairs}
        t_mats = {p: t_mats[p] + _dot_split(t_mats[p], powers[p]) for p in pairs}
        span *= 2

    us, w16s, kus, kws, wqs, attns, sds = {}, {}, {}, {}, {}, {}, {}
    for a, c in pairs:
        sl = chunks[c]
        t16 = t_mats[a, c].astype(BF16)
        us[a, c] = jnp.dot(t16, v_betas[a][sl], preferred_element_type=F32)
        w16s[a, c] = jnp.dot(t16, kb_decs[a][sl], preferred_element_type=F32).astype(BF16)
        qk = lax.dot_general(q16s[a][sl], k16s[a][sl], nt, preferred_element_type=F32)
        attns[a, c] = jnp.where(incl, qk * decays[a, c], 0.0).astype(BF16)
    for a, c in pairs:
        sl = chunks[c]
        gc = gcs[a][sl]
        g_last = gc[cs - 1:cs, :]
        kd_t = (ks[a][sl] * jnp.exp(g_last - gc)).T.astype(BF16)
        kws[a, c] = jnp.dot(kd_t, w16s[a, c], preferred_element_type=F32).astype(BF16)
        kus[a, c] = jnp.dot(kd_t, us[a, c].astype(BF16), preferred_element_type=F32)
        wqs[a, c] = jnp.concatenate([w16s[a, c], q_decs[a][sl]], axis=0)
        sds[a, c] = jnp.exp(g_last)

    def emit_output(c, a, inter, v_new):
        sl = chunks[c]
        o = inter + jnp.dot(attns[a, c], v_new.astype(BF16), preferred_element_type=F32)
        o = o * lax.rsqrt(jnp.mean(o * o, axis=-1, keepdims=True) + NORM_EPS) * nw_ref[...]
        o_ref[sl, cols[a]] = (o * _silu(z_ref[sl, cols[a]])).astype(o_ref.dtype)

    states = [state_ref[a] for a in heads]
    pending = None
    for c in range(nc):
        prods = []
        for a in heads:
            s16 = states[a].astype(BF16)
            prods.append((jnp.dot(kws[a, c], s16, preferred_element_type=F32),
                          jnp.dot(wqs[a, c], s16, preferred_element_type=F32)))
        if pending is not None:
            for a in heads:
                emit_output(c - 1, a, *pending[a])
        pending = []
        for a in heads:
            decayed, prod = prods[a]
            states[a] = states[a] * sds[a, c] - decayed + kus[a, c]
            pending.append((prod[cs:2 * cs], us[a, c] - prod[0:cs]))
    for a in heads:
        emit_output(nc - 1, a, *pending[a])
        state_ref[a] = states[a]


def gdn_core(proj, gates, conv_w, norm_w, n_heads, ts):
    b, s, _ = proj.shape
    dh = norm_w.shape[0]
    d = n_heads * dh
    width = conv_w.shape[0]
    cs = GDN_CHUNK
    gc_rows = gates[:, :, n_heads:2 * n_heads].reshape(b, s // cs, cs, n_heads)
    gc_rows = gc_rows.transpose(0, 3, 1, 2)

    hps = GDN_HEADS_PER_STEP
    wide = hps * dh
    groups = n_heads // hps

    def col(section):
        return pl.BlockSpec((None, ts, wide), lambda bi, h, i: (bi, i, section * groups + h))

    def cw(section):
        return pl.BlockSpec((width, wide), lambda bi, h, i: (0, section * groups + h))

    return pl.pallas_call(
        functools.partial(_gdn_kernel, cs=GDN_CHUNK, width=width, n_heads=n_heads),
        grid=(b, groups, s // ts),
        in_specs=[col(0), col(1), col(2), col(3),
                  pl.BlockSpec((None, ts, LANES), lambda bi, h, i: (bi, i, 0)),
                  pl.BlockSpec((None, hps, ts // cs, cs), lambda bi, h, i: (bi, h, i, 0)),
                  cw(0), cw(1), cw(2), _resident((1, dh))],
        out_specs=pl.BlockSpec((None, ts, wide), lambda bi, h, i: (bi, i, h)),
        out_shape=jax.ShapeDtypeStruct((b, s, d), BF16),
        scratch_shapes=[pltpu.VMEM((ts + SUBLANES, wide), F32)] * 3
                       + [pltpu.VMEM((3, SUBLANES, wide), F32),
                          pltpu.VMEM((hps, dh, dh), F32)],
        compiler_params=_params("parallel", "parallel", "arbitrary"),
        name="gdn_core",
    )(proj, proj, proj, proj, gates, gc_rows, conv_w, conv_w, conv_w,
      norm_w.astype(F32).reshape(1, dh))


def _row_tile(n, target):
    t = min(n, target)
    while n % t:
        t //= 2
    return t


def kernel(x, mix_norm, ffn_norm, final_norm, moba_w_qkv, moba_w_o, sconv_w_in, sconv_conv,
           sconv_w_out, gdn_w_in, gdn_conv, gdn_a_log, gdn_dt_bias, gdn_norm, gdn_w_o,
           ffn_w_up, ffn_conv, ffn_w_down):
    bsz, seq, d = x.shape
    depth = mix_norm.shape[0]
    t = bsz * seq
    tm = _row_tile(seq, 512)
    for i in range(depth):
        kind, j = i % N_MIXERS, i // N_MIXERS
        if kind == 0:
            qk, vt, k_mean = moba_proj(x, mix_norm[i], moba_w_qkv[j].astype(BF16), MOBA_HEADS,
                                       tm)
            o = moba_attention(qk, vt, k_mean.reshape(bsz, seq // MOBA_BLOCK, d), MOBA_HEADS)
            x = proj_residual(x.reshape(t, d), o.reshape(t, d), moba_w_o[j].astype(BF16),
                              tm).reshape(bsz, seq, d)
        elif kind == 1:
            x = short_conv_layer(x, mix_norm[i], sconv_w_in[j].astype(BF16), sconv_conv[j],
                                 sconv_w_out[j].astype(BF16), tm)
        else:
            n_in = gdn_w_in.shape[-1]
            w_in = jnp.pad(gdn_w_in[j], ((0, 0), (0, 4 * d + LANES - n_in))).astype(BF16)
            proj = norm_matmul(x.reshape(t, d), mix_norm[i], w_in, F32, tm)
            gates = gdn_gates(proj, gdn_a_log[j], gdn_dt_bias[j], GDN_HEADS, tm)
            o = gdn_core(proj.reshape(bsz, seq, 4 * d + LANES), gates.reshape(bsz, seq, LANES),
                         gdn_conv[j], gdn_norm[j], GDN_HEADS, _row_tile(seq, 512))
            x = proj_residual(x.reshape(t, d), o.reshape(t, d), gdn_w_o[j].astype(BF16),
                              tm).reshape(bsz, seq, d)
        x = conv_ffn_layer(x, ffn_norm[i], ffn_w_up[i].astype(BF16), ffn_conv[i],
                           ffn_w_down[i].astype(BF16),
                           final_norm if i == depth - 1 else None, tm, 256)
    return x
```

```python
import functools

import jax
import jax.numpy as jnp
from jax import lax
from jax.experimental import pallas as pl
from jax.experimental.pallas import tpu as pltpu

F32 = jnp.float32
BF16 = jnp.bfloat16

N_MIXERS = 3
MOBA_HEADS = 8
MOBA_BLOCK = 256
MOBA_TOPK = 3
GDN_HEADS = 8
GDN_CHUNK = 64
NORM_EPS = 1e-6
NEG_INF = -1e30
LOG2E = 1.4426950408889634
MOBA_GROUP = 4
MOBA_HEADS_PER_STEP = 4
GDN_HEADS_PER_STEP = 4

LANES = 128
SUBLANES = 8
BF16_SUBLANES = 16
VMEM_LIMIT = 56 * 1024 * 1024


def _params(*sem):
    return pltpu.CompilerParams(dimension_semantics=sem, vmem_limit_bytes=VMEM_LIMIT)


def _rmsnorm(x, g):
    return x * lax.rsqrt(jnp.mean(x * x, axis=-1, keepdims=True) + NORM_EPS) * g


def _silu(x):
    return x * jax.nn.sigmoid(x)


def _resident(shape):
    return pl.BlockSpec(shape, lambda *_: (0,) * len(shape))


def _norm_matmul_kernel(x_ref, g_ref, w_ref, o_ref, *km_ref, km_cols, km_rows):
    hn = _rmsnorm(x_ref[...], g_ref[...]).astype(BF16)
    y = jnp.dot(hn, w_ref[...], preferred_element_type=F32)
    o_ref[...] = y.astype(o_ref.dtype)
    if km_ref:
        lo, hi = km_cols
        yk = y[:, lo:hi]
        tm = yk.shape[0]
        km = jnp.mean(yk.reshape(tm // km_rows, km_rows, hi - lo), axis=1)
        km_ref[0][...] = km[:, None, :]


def norm_matmul(x2, g, w, out_dtype, tm, km_cols=None, km_rows=None):
    t, d = x2.shape
    n = w.shape[1]
    out_shape = [jax.ShapeDtypeStruct((t, n), out_dtype)]
    out_specs = [pl.BlockSpec((tm, n), lambda i: (i, 0))]
    if km_cols is not None:
        nk = km_cols[1] - km_cols[0]
        out_shape.append(jax.ShapeDtypeStruct((t // km_rows, 1, nk), F32))
        out_specs.append(pl.BlockSpec((tm // km_rows, 1, nk), lambda i: (i, 0, 0)))
    res = pl.pallas_call(
        functools.partial(_norm_matmul_kernel, km_cols=km_cols, km_rows=km_rows),
        grid=(t // tm,),
        in_specs=[pl.BlockSpec((tm, d), lambda i: (i, 0)),
                  _resident((1, d)),
                  _resident((d, n))],
        out_specs=out_specs,
        out_shape=out_shape,
        compiler_params=_params("parallel"),
        name="norm_matmul",
    )(x2, g.reshape(1, d), w)
    return res if km_cols is not None else res[0]


def _proj_residual_kernel(x_ref, o_ref, w_ref, out_ref):
    out_ref[...] = x_ref[...] + jnp.dot(o_ref[...], w_ref[...], preferred_element_type=F32)


def proj_residual(x2, o2, w, tm):
    t, d = x2.shape
    return pl.pallas_call(
        _proj_residual_kernel,
        grid=(t // tm,),
        in_specs=[pl.BlockSpec((tm, d), lambda i: (i, 0)),
                  pl.BlockSpec((tm, o2.shape[1]), lambda i: (i, 0)),
                  _resident(w.shape)],
        out_specs=pl.BlockSpec((tm, d), lambda i: (i, 0)),
        out_shape=jax.ShapeDtypeStruct((t, d), F32),
        compiler_params=_params("parallel"),
        name="proj_residual",
    )(x2, o2, w)


def _causal_conv_rows(buf_ref, tail_ref, cur, cw, width, tm):
    buf_ref[0:SUBLANES, :] = tail_ref[...]
    buf_ref[SUBLANES:SUBLANES + tm, :] = cur
    out = cw[width - 1:width, :] * cur
    for k in range(width - 1):
        off = SUBLANES - (width - 1) + k
        out = out + cw[k:k + 1, :] * buf_ref[off:off + tm, :]
    tail_ref[...] = cur[tm - SUBLANES:tm, :]
    return out


def _ffn_kernel(x_ref, g_ref, wup_ref, cw_ref, wdn_ref, *rest, ffn, tf, width, final):
    if final:
        fn_ref, out_ref, gbuf, ubuf, tail_ref, act_ref = rest
    else:
        out_ref, gbuf, ubuf, tail_ref, act_ref = rest
    tm = x_ref.shape[0]

    @pl.when(pl.program_id(1) == 0)
    def _():
        tail_ref[...] = jnp.zeros(tail_ref.shape, F32)

    x = x_ref[...]
    hn = _rmsnorm(x, g_ref[...]).astype(BF16)
    for c in range(ffn // tf):
        lo = c * tf
        hi = ffn + lo
        gate = jnp.dot(hn, wup_ref[:, lo:lo + tf], preferred_element_type=F32)
        up = jnp.dot(hn, wup_ref[:, hi:hi + tf], preferred_element_type=F32)
        gate = _causal_conv_rows(gbuf.at[c % 2], tail_ref.at[:, lo:lo + tf], gate,
                                 cw_ref[:, lo:lo + tf], width, tm)
        up = _causal_conv_rows(ubuf.at[c % 2], tail_ref.at[:, hi:hi + tf], up,
                               cw_ref[:, hi:hi + tf], width, tm)
        act_ref[:, lo:lo + tf] = (_silu(gate) * up).astype(BF16)
    res = x + jnp.dot(act_ref[...], wdn_ref[...], preferred_element_type=F32)
    if final:
        res = _rmsnorm(res, fn_ref[...])
    out_ref[...] = res


def conv_ffn_layer(x, g, w_up, conv_w, w_down, final_g, tm, tf):
    b, s, d = x.shape
    ffn = w_down.shape[0]
    width = conv_w.shape[0]
    final = final_g is not None
    row = pl.BlockSpec((None, tm, d), lambda bi, i: (bi, i, 0))
    in_specs = [row, _resident((1, d)), _resident(w_up.shape), _resident(conv_w.shape),
                _resident(w_down.shape)]
    args = [x, g.reshape(1, d), w_up, conv_w, w_down]
    if final:
        in_specs.append(_resident((1, d)))
        args.append(final_g.reshape(1, d))
    return pl.pallas_call(
        functools.partial(_ffn_kernel, ffn=ffn, tf=tf, width=width, final=final),
        grid=(b, s // tm),
        in_specs=in_specs,
        out_specs=row,
        out_shape=jax.ShapeDtypeStruct((b, s, d), F32),
        scratch_shapes=[pltpu.VMEM((2, tm + SUBLANES, tf), F32),
                        pltpu.VMEM((2, tm + SUBLANES, tf), F32),
                        pltpu.VMEM((SUBLANES, 2 * ffn), F32),
                        pltpu.VMEM((tm, ffn), BF16)],
        compiler_params=_params("parallel", "arbitrary"),
        name="conv_ffn",
    )(*args)


def _sconv_kernel(x_ref, g_ref, win_ref, cw_ref, wout_ref, out_ref, cbuf, tail_ref, *, width):
    tm, d = x_ref.shape

    @pl.when(pl.program_id(1) == 0)
    def _():
        tail_ref[...] = jnp.zeros(tail_ref.shape, F32)

    x = x_ref[...]
    hn = _rmsnorm(x, g_ref[...]).astype(BF16)
    b_gate = jnp.dot(hn, win_ref[:, 0:d], preferred_element_type=F32)
    c_gate = jnp.dot(hn, win_ref[:, d:2 * d], preferred_element_type=F32)
    xv = jnp.dot(hn, win_ref[:, 2 * d:3 * d], preferred_element_type=F32)
    y = b_gate * _causal_conv_rows(cbuf, tail_ref, c_gate * xv, cw_ref[...], width, tm)
    out_ref[...] = x + jnp.dot(y.astype(BF16), wout_ref[...], preferred_element_type=F32)


def short_conv_layer(x, g, w_in, conv_w, w_out, tm):
    b, s, d = x.shape
    row = pl.BlockSpec((None, tm, d), lambda bi, i: (bi, i, 0))
    return pl.pallas_call(
        functools.partial(_sconv_kernel, width=conv_w.shape[0]),
        grid=(b, s // tm),
        in_specs=[row, _resident((1, d)), _resident(w_in.shape), _resident(conv_w.shape),
                  _resident(w_out.shape)],
        out_specs=row,
        out_shape=jax.ShapeDtypeStruct((b, s, d), F32),
        scratch_shapes=[pltpu.VMEM((tm + SUBLANES, d), F32), pltpu.VMEM((SUBLANES, d), F32)],
        compiler_params=_params("parallel", "arbitrary"),
        name="short_conv",
    )(x, g.reshape(1, d), w_in, conv_w, w_out)


def _moba_proj_kernel(x_ref, g_ref, w_ref, qk_ref, vt_ref, km_ref, *, blk, q_scale):
    tm, d = x_ref.shape
    n_heads, rows, _ = vt_ref.shape
    hd = d // n_heads
    hn = _rmsnorm(x_ref[...], g_ref[...]).astype(BF16)
    y = jnp.dot(hn, w_ref[...], preferred_element_type=F32)
    qk_ref[:, 0:d] = (y[:, 0:d] * q_scale).astype(BF16)
    qk_ref[:, d:2 * d] = y[:, d:2 * d].astype(BF16)
    vt_ref[:, 0:hd, :] = y[:, 2 * d:3 * d].T.reshape(n_heads, hd, tm).astype(BF16)
    vt_ref[:, hd:rows, :] = jnp.ones((n_heads, rows - hd, tm), BF16)
    km = jnp.mean(y[:, d:2 * d].reshape(tm // blk, blk, d), axis=1)
    km_ref[...] = km[:, None, :]


def moba_proj(x, g, w, n_heads, tm):
    b, s, d = x.shape
    blk = MOBA_BLOCK
    hd = d // n_heads
    rows = hd + BF16_SUBLANES
    return pl.pallas_call(
        functools.partial(_moba_proj_kernel, blk=blk, q_scale=hd ** -0.5 * LOG2E),
        grid=(b, s // tm),
        in_specs=[pl.BlockSpec((None, tm, d), lambda bi, i: (bi, i, 0)),
                  _resident((1, d)), _resident(w.shape)],
        out_specs=[pl.BlockSpec((None, tm, 2 * d), lambda bi, i: (bi, i, 0)),
                   pl.BlockSpec((None, n_heads, rows, tm), lambda bi, i: (bi, 0, 0, i)),
                   pl.BlockSpec((None, tm // blk, 1, d), lambda bi, i: (bi, i, 0, 0))],
        out_shape=[jax.ShapeDtypeStruct((b, s, 2 * d), BF16),
                   jax.ShapeDtypeStruct((b, n_heads, rows, s), BF16),
                   jax.ShapeDtypeStruct((b, s // blk, 1, d), F32)],
        compiler_params=_params("parallel", "parallel"),
        name="moba_proj",
    )(x, g.reshape(1, d), w)


def _moba_kernel(slopes_ref, q_ref, k_ref, vt_ref, km_ref, o_ref, m_ref, acc_ref,
                 u0_ref, u1_ref, top0_ref, top1_ref, *, blk, n_top, group):
    qi = pl.program_id(2)
    tq = q_ref.shape[0]
    n_heads = vt_ref.shape[0]
    hd = q_ref.shape[1] // n_heads
    n_blk = km_ref.shape[0]
    nt = (((1,), (1,)), ((), ()))
    n_groups = (qi + group) // group
    u_refs = (u0_ref, u1_ref)
    top_refs = (top0_ref, top1_ref)
    heads = range(n_heads)
    cols = [slice(a * hd, (a + 1) * hd) for a in heads]
    slope2 = [slopes_ref[pl.program_id(1) * n_heads + a] * LOG2E for a in heads]
    qs = [q_ref[:, cols[a]] for a in heads]

    kk = lax.broadcasted_iota(jnp.int32, (blk, tq), 0)
    qq = lax.broadcasted_iota(jnp.int32, (blk, tq), 1)
    rel = (kk - qq).astype(F32)
    bias = [slope2[a] * rel for a in heads]
    bias_own = [jnp.where(kk <= qq, bias[a], NEG_INF) for a in heads]

    def block_start(j):
        return pl.multiple_of(jnp.maximum(j, 0) * blk, blk)

    def score_group(g, slot):
        for t in range(group):
            j = qi - (g * group + t)
            for a in heads:
                s = lax.dot_general(k_ref[pl.ds(block_start(j), blk), cols[a]], qs[a], nt,
                                    preferred_element_type=F32)
                u = s + (jnp.where(g == 0, bias_own[a], bias[a]) if t == 0 else bias[a])
                u_refs[slot][a, t] = u
                top_refs[slot][a, t] = jnp.max(u, axis=0, keepdims=True)

    score_group(0, 0)
    for a in heads:
        m_ref[a] = jnp.full((1, tq), NEG_INF, F32)
        acc_ref[a] = jnp.zeros(acc_ref.shape[1:], F32)

    blk_ids = lax.broadcasted_iota(jnp.int32, (n_blk, tq), 0)
    sels = []
    for a in heads:
        gate = lax.dot_general(km_ref[:, cols[a]].astype(BF16), qs[a], nt,
                               preferred_element_type=F32)
        gate = jnp.where(blk_ids < qi, gate, NEG_INF)
        picks = []
        for _ in range(n_top):
            best = jnp.max(gate, axis=0, keepdims=True)
            idx = jnp.min(jnp.where(gate == best, blk_ids, n_blk), axis=0, keepdims=True)
            picks.append(jnp.where(idx < qi, idx, n_blk))
            gate = jnp.where(blk_ids == idx, -jnp.inf, gate)
        sels.append(picks)

    def consume_group(g, slot):
        for a in heads:
            m_old = m_ref[a]
            m_new = m_old
            hits, offs = [], []
            for t in range(group):
                j = qi - (g * group + t)
                hit = (sels[a][0] == j) | (j == qi)
                for sel in sels[a][1:]:
                    hit = hit | (sel == j)
                off = -slope2[a] * ((qi - j) * blk).astype(F32)
                m_new = jnp.maximum(m_new,
                                    jnp.where(hit, top_refs[slot][a, t] + off, NEG_INF))
                hits.append(hit); offs.append(off)
            acc = jnp.exp2(m_old - m_new) * acc_ref[a]
            for t in range(group):
                j = qi - (g * group + t)
                shift = jnp.where(hits[t], m_new - offs[t], -NEG_INF)
                p = jnp.exp2((u_refs[slot][a, t] - shift).astype(BF16))
                acc = acc + jnp.dot(vt_ref[a, :, pl.ds(block_start(j), blk)], p,
                                    preferred_element_type=F32)
            m_ref[a] = m_new
            acc_ref[a] = acc

    def group_body(g, carry):
        for slot in range(2):
            @pl.when(g % 2 == slot)
            def _():
                score_group(g + 1, 1 - slot)
                consume_group(g, slot)
        return carry

    lax.fori_loop(0, n_groups - 1, group_body, 0)
    for slot in range(2):
        @pl.when((n_groups - 1) % 2 == slot)
        def _():
            consume_group(n_groups - 1, slot)

    for a in heads:
        acc = acc_ref[a]
        o_ref[:, cols[a]] = (acc[0:hd] / acc[hd:hd + 1]).T.astype(o_ref.dtype)


def moba_attention(qk, vt, k_mean, n_heads):
    b, s, d2 = qk.shape
    d = d2 // 2
    hd = d // n_heads
    rows = vt.shape[2]
    blk = MOBA_BLOCK
    n_blk = s // blk
    n_top = min(MOBA_TOPK, n_blk)
    slopes = jnp.exp2(-8.0 * jnp.arange(1, n_heads + 1, dtype=F32) / n_heads)
    hps = MOBA_HEADS_PER_STEP
    wide = hps * hd
    once = pl.Buffered(1)
    grid_spec = pltpu.PrefetchScalarGridSpec(
        num_scalar_prefetch=1,
        grid=(b, n_heads // hps, n_blk),
        in_specs=[
            pl.BlockSpec((None, blk, wide), lambda bi, h, i, *_: (bi, i, h)),
            pl.BlockSpec((None, s, wide), lambda bi, h, i, *_: (bi, 0, n_heads // hps + h),
                         pipeline_mode=once),
            pl.BlockSpec((None, hps, rows, s), lambda bi, h, i, *_: (bi, h, 0, 0),
                         pipeline_mode=once),
            pl.BlockSpec((None, n_blk, wide), lambda bi, h, i, *_: (bi, 0, h)),
        ],
        out_specs=pl.BlockSpec((None, blk, wide), lambda bi, h, i, *_: (bi, i, h)),
        scratch_shapes=[pltpu.VMEM((hps, 1, blk), F32),
                        pltpu.VMEM((hps, rows, blk), F32),
                        pltpu.VMEM((hps, MOBA_GROUP, blk, blk), F32),
                        pltpu.VMEM((hps, MOBA_GROUP, blk, blk), F32),
                        pltpu.VMEM((hps, MOBA_GROUP, 1, blk), F32),
                        pltpu.VMEM((hps, MOBA_GROUP, 1, blk), F32)],
    )
    return pl.pallas_call(
        functools.partial(_moba_kernel, blk=blk, n_top=n_top, group=MOBA_GROUP),
        grid_spec=grid_spec,
        out_shape=jax.ShapeDtypeStruct((b, s, d), BF16),
        compiler_params=_params("parallel", "parallel", "arbitrary"),
        name="moba_attention",
    )(slopes, qk, qk, vt, k_mean)


def _dot_hi(a, b):
    return jnp.dot(a, b, preferred_element_type=F32, precision=lax.Precision.HIGHEST)


def _split_bf16(a):
    hi = a.astype(BF16)
    return hi, (a - hi.astype(F32)).astype(BF16)


def _dot_split(a, b):
    a_hi, a_lo = _split_bf16(a)
    b_hi, b_lo = _split_bf16(b)
    return (jnp.dot(a_hi, b_hi, preferred_element_type=F32)
            + jnp.dot(a_hi, b_lo, preferred_element_type=F32)
            + jnp.dot(a_lo, b_hi, preferred_element_type=F32))


def _gdn_gates_kernel(sm_ref, alog_ref, dt_ref, o_ref, *, cs, n_heads):
    small = sm_ref[...]
    tm = small.shape[0]
    sp = small + dt_ref[...]
    softplus = jnp.maximum(sp, 0.0) + jnp.log1p(jnp.exp(-jnp.abs(sp)))
    g = -jnp.exp(alog_ref[...]) * softplus
    ii = lax.broadcasted_iota(jnp.int32, (tm, tm), 0)
    jj = lax.broadcasted_iota(jnp.int32, (tm, tm), 1)
    same_chunk_prefix = ((ii // cs) == (jj // cs)) & (ii >= jj)
    gc = _dot_hi(same_chunk_prefix.astype(F32), g)
    lane = lax.broadcasted_iota(jnp.int32, small.shape, 1)
    o_ref[...] = jnp.where(lane < n_heads, jax.nn.sigmoid(small), gc)


def gdn_gates(proj2, a_log, dt_bias, n_heads, tm):
    t, n = proj2.shape
    last = n // LANES - 1
    pad = (n_heads, LANES - 2 * n_heads)
    alog_vec = jnp.pad(a_log.astype(F32), pad).reshape(1, LANES)
    dt_vec = jnp.pad(dt_bias.astype(F32), pad).reshape(1, LANES)
    return pl.pallas_call(
        functools.partial(_gdn_gates_kernel, cs=GDN_CHUNK, n_heads=n_heads),
        grid=(t // tm,),
        in_specs=[pl.BlockSpec((tm, LANES), lambda i: (i, last)),
                  _resident((1, LANES)), _resident((1, LANES))],
        out_specs=pl.BlockSpec((tm, LANES), lambda i: (i, 0)),
        out_shape=jax.ShapeDtypeStruct((t, LANES), F32),
        compiler_params=_params("parallel"),
        name="gdn_gates",
    )(proj2, alog_vec, dt_vec)


def _gdn_kernel(q_ref, k_ref, v_ref, z_ref, gate_ref, gcr_ref, cwq_ref, cwk_ref, cwv_ref,
                nw_ref, o_ref, qbuf, kbuf, vbuf, tail_ref, state_ref, *, cs, width, n_heads):
    ts = q_ref.shape[0]
    hps = state_ref.shape[0]
    dh = q_ref.shape[1] // hps
    nc = ts // cs
    heads = range(hps)
    cols = [slice(a * dh, (a + 1) * dh) for a in heads]
    chunks = [slice(c * cs, (c + 1) * cs) for c in range(nc)]
    pairs = [(a, c) for a in heads for c in range(nc)]
    head0 = pl.program_id(1) * hps

    @pl.when(pl.program_id(2) == 0)
    def _():
        tail_ref[...] = jnp.zeros(tail_ref.shape, F32)
        state_ref[...] = jnp.zeros(state_ref.shape, F32)

    q_all = _silu(_causal_conv_rows(qbuf, tail_ref.at[0], q_ref[...], cwq_ref[...], width, ts))
    k_all = _silu(_causal_conv_rows(kbuf, tail_ref.at[1], k_ref[...], cwk_ref[...], width, ts))
    v_all = _silu(_causal_conv_rows(vbuf, tail_ref.at[2], v_ref[...], cwv_ref[...], width, ts))

    gates = gate_ref[...]
    lane = lax.broadcasted_iota(jnp.int32, gates.shape, 1)

    ii = lax.broadcasted_iota(jnp.int32, (cs, cs), 0)
    jj = lax.broadcasted_iota(jnp.int32, (cs, cs), 1)
    incl = ii >= jj
    strict = ii > jj
    eye = (ii == jj).astype(F32)
    nt = (((1,), (1,)), ((), ()))

    ks, gcs, k16s, q16s, v_betas, kb_decs, q_decs, k_betas = [], [], [], [], [], [], [], []
    for a in heads:
        q = q_all[:, cols[a]]
        k = k_all[:, cols[a]]
        q = q * lax.rsqrt(jnp.sum(q * q, axis=-1, keepdims=True) + NORM_EPS) * (dh ** -0.5)
        k = k * lax.rsqrt(jnp.sum(k * k, axis=-1, keepdims=True) + NORM_EPS)
        beta = jnp.sum(jnp.where(lane == head0 + a, gates, 0.0), axis=1, keepdims=True)
        gc = jnp.sum(jnp.where(lane == n_heads + head0 + a, gates, 0.0), axis=1,
                     keepdims=True)
        e_gc = jnp.exp(gc)
        k_beta = k * beta
        ks.append(k); gcs.append(gc)
        v_betas.append((v_all[:, cols[a]] * beta).astype(BF16))
        kb_decs.append((k_beta * e_gc).astype(BF16))
        q_decs.append((q * e_gc).astype(BF16))
        k_betas.append(k_beta.astype(BF16))
        k16s.append(k.astype(BF16))
        q16s.append(q.astype(BF16))

    decays, negs = {}, {}
    for a, c in pairs:
        sl = chunks[c]
        diff = gcs[a][sl] - gcr_ref[a, c:c + 1, :]
        decay = jnp.where(incl, jnp.exp(jnp.where(incl, diff, 0.0)), 0.0)
        kk = lax.dot_general(k_betas[a][sl], k16s[a][sl], nt, preferred_element_type=F32)
        decays[a, c] = decay
        negs[a, c] = jnp.where(strict, -(kk * decay), 0.0)
    t_mats = {p: eye + negs[p] for p in pairs}
    powers = negs
    span = 2
    while span < cs:
        powers = {p: _dot_split(powers[p], powers[p]) for p in pairs}
        t_mats = {p: t_mats[p] + _dot_split(t_mats[p], powers[p]) for p in pairs}
        span *= 2

    us, w16s, kus, kws, wqs, attns, sds = {}, {}, {}, {}, {}, {}, {}
    for a, c in pairs:
        sl = chunks[c]
        t16 = t_mats[a, c].astype(BF16)
        us[a, c] = jnp.dot(t16, v_betas[a][sl], preferred_element_type=F32)
        w16s[a, c] = jnp.dot(t16, kb_decs[a][sl], preferred_element_type=F32).astype(BF16)
        qk = lax.dot_general(q16s[a][sl], k16s[a][sl], nt, preferred_element_type=F32)
        attns[a, c] = jnp.where(incl, qk * decays[a, c], 0.0).astype(BF16)
    for a, c in pairs:
        sl = chunks[c]
        gc = gcs[a][sl]
        g_last = gc[cs - 1:cs, :]
        kd_t = (ks[a][sl] * jnp.exp(g_last - gc)).T.astype(BF16)
        kws[a, c] = jnp.dot(kd_t, w16s[a, c], preferred_element_type=F32).astype(BF16)
        kus[a, c] = jnp.dot(kd_t, us[a, c].astype(BF16), preferred_element_type=F32)
        wqs[a, c] = jnp.concatenate([w16s[a, c], q_decs[a][sl]], axis=0)
        sds[a, c] = jnp.exp(g_last)

    def emit_output(c, a, inter, v_new):
        sl = chunks[c]
        o = inter + jnp.dot(attns[a, c], v_new.astype(BF16), preferred_element_type=F32)
        o = o * lax.rsqrt(jnp.mean(o * o, axis=-1, keepdims=True) + NORM_EPS) * nw_ref[...]
        o_ref[sl, cols[a]] = (o * _silu(z_ref[sl, cols[a]])).astype(o_ref.dtype)

    states = [state_ref[a] for a in heads]
    pending = None
    for c in range(nc):
        prods = []
        for a in heads:
            s16 = states[a].astype(BF16)
            prods.append((jnp.dot(kws[a, c], s16, preferred_element_type=F32),
                          jnp.dot(wqs[a, c], s16, preferred_element_type=F32)))
        if pending is not None:
            for a in heads:
                emit_output(c - 1, a, *pending[a])
        pending = []
        for a in heads:
            decayed, prod = prods[a]
            states[a] = states[a] * sds[a, c] - decayed + kus[a, c]
            pending.append((prod[cs:2 * cs], us[a, c] - prod[0:cs]))
    for a in heads:
        emit_output(nc - 1, a, *pending[a])
        state_ref[a] = states[a]


def gdn_core(proj, gates, conv_w, norm_w, n_heads, ts):
    b, s, _ = proj.shape
    dh = norm_w.shape[0]
    d = n_heads * dh
    width = conv_w.shape[0]
    cs = GDN_CHUNK
    gc_rows = gates[:, :, n_heads:2 * n_heads].reshape(b, s // cs, cs, n_heads)
    gc_rows = gc_rows.transpose(0, 3, 1, 2)

    hps = GDN_HEADS_PER_STEP
    wide = hps * dh
    groups = n_heads // hps

    def col(section):
        return pl.BlockSpec((None, ts, wide), lambda bi, h, i: (bi, i, section * groups + h))

    def cw(section):
        return pl.BlockSpec((width, wide), lambda bi, h, i: (0, section * groups + h))

    return pl.pallas_call(
        functools.partial(_gdn_kernel, cs=GDN_CHUNK, width=width, n_heads=n_heads),
        grid=(b, groups, s // ts),
        in_specs=[col(0), col(1), col(2), col(3),
                  pl.BlockSpec((None, ts, LANES), lambda bi, h, i: (bi, i, 0)),
                  pl.BlockSpec((None, hps, ts // cs, cs), lambda bi, h, i: (bi, h, i, 0)),
                  cw(0), cw(1), cw(2), _resident((1, dh))],
        out_specs=pl.BlockSpec((None, ts, wide), lambda bi, h, i: (bi, i, h)),
        out_shape=jax.ShapeDtypeStruct((b, s, d), BF16),
        scratch_shapes=[pltpu.VMEM((ts + SUBLANES, wide), F32)] * 3
                       + [pltpu.VMEM((3, SUBLANES, wide), F32),
                          pltpu.VMEM((hps, dh, dh), F32)],
        compiler_params=_params("parallel", "parallel", "arbitrary"),
        name="gdn_core",
    )(proj, proj, proj, proj, gates, gc_rows, conv_w, conv_w, conv_w,
      norm_w.astype(F32).reshape(1, dh))


def _row_tile(n, target):
    t = min(n, target)
    while n % t:
        t //= 2
    return t


def kernel(x, mix_norm, ffn_norm, final_norm, moba_w_qkv, moba_w_o, sconv_w_in, sconv_conv,
           sconv_w_out, gdn_w_in, gdn_conv, gdn_a_log, gdn_dt_bias, gdn_norm, gdn_w_o,
           ffn_w_up, ffn_conv, ffn_w_down):
    bsz, seq, d = x.shape
    depth = mix_norm.shape[0]
    t = bsz * seq
    tm = _row_tile(seq, 512)
    for i in range(depth):
        kind, j = i % N_MIXERS, i // N_MIXERS
        if kind == 0:
            qk, vt, k_mean = moba_proj(x, mix_norm[i], moba_w_qkv[j].astype(BF16), MOBA_HEADS,
                                       tm)
            o = moba_attention(qk, vt, k_mean.reshape(bsz, seq // MOBA_BLOCK, d), MOBA_HEADS)
            x = proj_residual(x.reshape(t, d), o.reshape(t, d), moba_w_o[j].astype(BF16),
                              tm).reshape(bsz, seq, d)
        elif kind == 1:
            x = short_conv_layer(x, mix_norm[i], sconv_w_in[j].astype(BF16), sconv_conv[j],
                                 sconv_w_out[j].astype(BF16), tm)
        else:
            n_in = gdn_w_in.shape[-1]
            w_in = jnp.pad(gdn_w_in[j], ((0, 0), (0, 4 * d + LANES - n_in))).astype(BF16)
            proj = norm_matmul(x.reshape(t, d), mix_norm[i], w_in, F32, tm)
            gates = gdn_gates(proj, gdn_a_log[j], gdn_dt_bias[j], GDN_HEADS, tm)
            o = gdn_core(proj.reshape(bsz, seq, 4 * d + LANES), gates.reshape(bsz, seq, LANES),
                         gdn_conv[j], gdn_norm[j], GDN_HEADS, _row_tile(seq, 512))
            x = proj_residual(x.reshape(t, d), o.reshape(t, d), gdn_w_o[j].astype(BF16),
                              tm).reshape(bsz, seq, d)
        x = conv_ffn_layer(x, ffn_norm[i], ffn_w_up[i].astype(BF16), ffn_conv[i],
                           ffn_w_down[i].astype(BF16),
                           final_norm if i == depth - 1 else None, tm, 256)
    return x
```

```python
import functools

import jax
import jax.numpy as jnp
from jax import lax
from jax.experimental import pallas as pl
from jax.experimental.pallas import tpu as pltpu

F32 = jnp.float32
BF16 = jnp.bfloat16

N_MIXERS = 3
MOBA_HEADS = 8
MOBA_BLOCK = 256
MOBA_TOPK = 3
GDN_HEADS = 8
GDN_CHUNK = 64
NORM_EPS = 1e-6
NEG_INF = -1e30
LOG2E = 1.4426950408889634
MOBA_GROUP = 4
MOBA_HEADS_PER_STEP = 4
GDN_HEADS_PER_STEP = 4

LANES = 128
SUBLANES = 8
BF16_SUBLANES = 16
VMEM_LIMIT = 56 * 1024 * 1024


def _params(*sem):
    return pltpu.CompilerParams(dimension_semantics=sem, vmem_limit_bytes=VMEM_LIMIT)


def _rmsnorm(x, g):
    return x * lax.rsqrt(jnp.mean(x * x, axis=-1, keepdims=True) + NORM_EPS) * g


def _silu(x):
    return x * jax.nn.sigmoid(x)


def _resident(shape):
    return pl.BlockSpec(shape, lambda *_: (0,) * len(shape))


def _norm_matmul_kernel(x_ref, g_ref, w_ref, o_ref, *km_ref, km_cols, km_rows):
    hn = _rmsnorm(x_ref[...], g_ref[...]).astype(BF16)
    y = jnp.dot(hn, w_ref[...], preferred_element_type=F32)
    o_ref[...] = y.astype(o_ref.dtype)
    if km_ref:
        lo, hi = km_cols
        yk = y[:, lo:hi]
        tm = yk.shape[0]
        km = jnp.mean(yk.reshape(tm // km_rows, km_rows, hi - lo), axis=1)
        km_ref[0][...] = km[:, None, :]


def norm_matmul(x2, g, w, out_dtype, tm, km_cols=None, km_rows=None):
    t, d = x2.shape
    n = w.shape[1]
    out_shape = [jax.ShapeDtypeStruct((t, n), out_dtype)]
    out_specs = [pl.BlockSpec((tm, n), lambda i: (i, 0))]
    if km_cols is not None:
        nk = km_cols[1] - km_cols[0]
        out_shape.append(jax.ShapeDtypeStruct((t // km_rows, 1, nk), F32))
        out_specs.append(pl.BlockSpec((tm // km_rows, 1, nk), lambda i: (i, 0, 0)))
    res = pl.pallas_call(
        functools.partial(_norm_matmul_kernel, km_cols=km_cols, km_rows=km_rows),
        grid=(t // tm,),
        in_specs=[pl.BlockSpec((tm, d), lambda i: (i, 0)),
                  _resident((1, d)),
                  _resident((d, n))],
        out_specs=out_specs,
        out_shape=out_shape,
        compiler_params=_params("parallel"),
        name="norm_matmul",
    )(x2, g.reshape(1, d), w)
    return res if km_cols is not None else res[0]


def _proj_residual_kernel(x_ref, o_ref, w_ref, out_ref):
    out_ref[...] = x_ref[...] + jnp.dot(o_ref[...], w_ref[...], preferred_element_type=F32)


def proj_residual(x2, o2, w, tm):
    t, d = x2.shape
    return pl.pallas_call(
        _proj_residual_kernel,
        grid=(t // tm,),
        in_specs=[pl.BlockSpec((tm, d), lambda i: (i, 0)),
                  pl.BlockSpec((tm, o2.shape[1]), lambda i: (i, 0)),
                  _resident(w.shape)],
        out_specs=pl.BlockSpec((tm, d), lambda i: (i, 0)),
        out_shape=jax.ShapeDtypeStruct((t, d), F32),
        compiler_params=_params("parallel"),
        name="proj_residual",
    )(x2, o2, w)


def _causal_conv_rows(buf_ref, tail_ref, cur, cw, width, tm):
    buf_ref[0:SUBLANES, :] = tail_ref[...]
    buf_ref[SUBLANES:SUBLANES + tm, :] = cur
    out = cw[width - 1:width, :] * cur
    for k in range(width - 1):
        off = SUBLANES - (width - 1) + k
        out = out + cw[k:k + 1, :] * buf_ref[off:off + tm, :]
    tail_ref[...] = cur[tm - SUBLANES:tm, :]
    return out


def _ffn_kernel(x_ref, g_ref, wup_ref, cw_ref, wdn_ref, *rest, ffn, tf, width, final):
    if final:
        fn_ref, out_ref, gbuf, ubuf, tail_ref, act_ref = rest
    else:
        out_ref, gbuf, ubuf, tail_ref, act_ref = rest
    tm = x_ref.shape[0]

    @pl.when(pl.program_id(1) == 0)
    def _():
        tail_ref[...] = jnp.zeros(tail_ref.shape, F32)

    x = x_ref[...]
    hn = _rmsnorm(x, g_ref[...]).astype(BF16)
    for c in range(ffn // tf):
        lo = c * tf
        hi = ffn + lo
        gate = jnp.dot(hn, wup_ref[:, lo:lo + tf], preferred_element_type=F32)
        up = jnp.dot(hn, wup_ref[:, hi:hi + tf], preferred_element_type=F32)
        gate = _causal_conv_rows(gbuf.at[c % 2], tail_ref.at[:, lo:lo + tf], gate,
                                 cw_ref[:, lo:lo + tf], width, tm)
        up = _causal_conv_rows(ubuf.at[c % 2], tail_ref.at[:, hi:hi + tf], up,
                               cw_ref[:, hi:hi + tf], width, tm)
        act_ref[:, lo:lo + tf] = (_silu(gate) * up).astype(BF16)
    res = x + jnp.dot(act_ref[...], wdn_ref[...], preferred_element_type=F32)
    if final:
        res = _rmsnorm(res, fn_ref[...])
    out_ref[...] = res


def conv_ffn_layer(x, g, w_up, conv_w, w_down, final_g, tm, tf):
    b, s, d = x.shape
    ffn = w_down.shape[0]
    width = conv_w.shape[0]
    final = final_g is not None
    row = pl.BlockSpec((None, tm, d), lambda bi, i: (bi, i, 0))
    in_specs = [row, _resident((1, d)), _resident(w_up.shape), _resident(conv_w.shape),
                _resident(w_down.shape)]
    args = [x, g.reshape(1, d), w_up, conv_w, w_down]
    if final:
        in_specs.append(_resident((1, d)))
        args.append(final_g.reshape(1, d))
    return pl.pallas_call(
        functools.partial(_ffn_kernel, ffn=ffn, tf=tf, width=width, final=final),
        grid=(b, s // tm),
        in_specs=in_specs,
        out_specs=row,
        out_shape=jax.ShapeDtypeStruct((b, s, d), F32),
        scratch_shapes=[pltpu.VMEM((2, tm + SUBLANES, tf), F32),
                        pltpu.VMEM((2, tm + SUBLANES, tf), F32),
                        pltpu.VMEM((SUBLANES, 2 * ffn), F32),
                        pltpu.VMEM((tm, ffn), BF16)],
        compiler_params=_params("parallel", "arbitrary"),
        name="conv_ffn",
    )(*args)


def _sconv_kernel(x_ref, g_ref, win_ref, cw_ref, wout_ref, out_ref, cbuf, tail_ref, *, width):
    tm, d = x_ref.shape

    @pl.when(pl.program_id(1) == 0)
    def _():
        tail_ref[...] = jnp.zeros(tail_ref.shape, F32)

    x = x_ref[...]
    hn = _rmsnorm(x, g_ref[...]).astype(BF16)
    b_gate = jnp.dot(hn, win_ref[:, 0:d], preferred_element_type=F32)
    c_gate = jnp.dot(hn, win_ref[:, d:2 * d], preferred_element_type=F32)
    xv = jnp.dot(hn, win_ref[:, 2 * d:3 * d], preferred_element_type=F32)
    y = b_gate * _causal_conv_rows(cbuf, tail_ref, c_gate * xv, cw_ref[...], width, tm)
    out_ref[...] = x + jnp.dot(y.astype(BF16), wout_ref[...], preferred_element_type=F32)


def short_conv_layer(x, g, w_in, conv_w, w_out, tm):
    b, s, d = x.shape
    row = pl.BlockSpec((None, tm, d), lambda bi, i: (bi, i, 0))
    return pl.pallas_call(
        functools.partial(_sconv_kernel, width=conv_w.shape[0]),
        grid=(b, s // tm),
        in_specs=[row, _resident((1, d)), _resident(w_in.shape), _resident(conv_w.shape),
                  _resident(w_out.shape)],
        out_specs=row,
        out_shape=jax.ShapeDtypeStruct((b, s, d), F32),
        scratch_shapes=[pltpu.VMEM((tm + SUBLANES, d), F32), pltpu.VMEM((SUBLANES, d), F32)],
        compiler_params=_params("parallel", "arbitrary"),
        name="short_conv",
    )(x, g.reshape(1, d), w_in, conv_w, w_out)


def _moba_proj_kernel(x_ref, g_ref, w_ref, q_ref, k_ref, vt_ref, km_ref,
                      *, blk, q_scale, pad_tiles):
    step = pl.program_id(1)

    @pl.when(step < pad_tiles)
    def _():
        k_ref[...] = jnp.zeros(k_ref.shape, BF16)
        vt_ref[...] = jnp.zeros(vt_ref.shape, BF16)

    @pl.when(step >= pad_tiles)
    def _():
        tm, d = x_ref.shape
        n_heads, rows, _ = vt_ref.shape
        hd = d // n_heads
        hn = _rmsnorm(x_ref[...], g_ref[...]).astype(BF16)
        y = jnp.dot(hn, w_ref[...], preferred_element_type=F32)
        q_ref[...] = (y[:, 0:d] * q_scale).astype(BF16)
        k_ref[...] = y[:, d:2 * d].astype(BF16)
        vt_ref[:, 0:hd, :] = y[:, 2 * d:3 * d].T.reshape(n_heads, hd, tm).astype(BF16)
        vt_ref[:, hd:rows, :] = jnp.ones((n_heads, rows - hd, tm), BF16)
        km = jnp.mean(y[:, d:2 * d].reshape(tm // blk, blk, d), axis=1)
        km_ref[...] = km[:, None, :]


def moba_proj(x, g, w, n_heads, tm):
    b, s, d = x.shape
    blk = MOBA_BLOCK
    hd = d // n_heads
    rows = hd + BF16_SUBLANES
    pad = MOBA_GROUP * blk
    pad_tiles = pad // tm
    assert pad_tiles * tm == pad

    def real(i):
        return jnp.maximum(i - pad_tiles, 0)

    return pl.pallas_call(
        functools.partial(_moba_proj_kernel, blk=blk, q_scale=hd ** -0.5 * LOG2E,
                          pad_tiles=pad_tiles),
        grid=(b, s // tm + pad_tiles),
        in_specs=[pl.BlockSpec((None, tm, d), lambda bi, i: (bi, real(i), 0)),
                  _resident((1, d)), _resident(w.shape)],
        out_specs=[pl.BlockSpec((None, tm, d), lambda bi, i: (bi, real(i), 0)),
                   pl.BlockSpec((None, tm, d), lambda bi, i: (bi, i, 0)),
                   pl.BlockSpec((None, n_heads, rows, tm), lambda bi, i: (bi, 0, 0, i)),
                   pl.BlockSpec((None, tm // blk, 1, d), lambda bi, i: (bi, real(i), 0, 0))],
        out_shape=[jax.ShapeDtypeStruct((b, s, d), BF16),
                   jax.ShapeDtypeStruct((b, s + pad, d), BF16),
                   jax.ShapeDtypeStruct((b, n_heads, rows, s + pad), BF16),
                   jax.ShapeDtypeStruct((b, s // blk, 1, d), F32)],
        compiler_params=_params("parallel", "arbitrary"),
        name="moba_proj",
    )(x, g.reshape(1, d), w)


def _split3_bf16(x):
    hi = x.astype(BF16)
    r = x - hi.astype(F32)
    mid = r.astype(BF16)
    return hi, mid, (r - mid.astype(F32)).astype(BF16)


def _moba_kernel(slopes_ref, q_ref, k_ref, vt_ref, km_ref, kaug_ref, o_ref, m_ref, acc_ref,
                 u0_ref, u1_ref, top0_ref, top1_ref, *, blk, n_top, group):
    qi = pl.program_id(2)
    tq = q_ref.shape[0]
    n_heads = vt_ref.shape[0]
    hd = q_ref.shape[1] // n_heads
    n_blk = km_ref.shape[0]
    nt = (((1,), (1,)), ((), ()))
    n_groups = (qi + group) // group
    u_refs = (u0_ref, u1_ref)
    top_refs = (top0_ref, top1_ref)
    heads = range(n_heads)
    cols = [slice(a * hd, (a + 1) * hd) for a in heads]
    slope2 = [slopes_ref[pl.program_id(1) * n_heads + a] * LOG2E for a in heads]
    qs = [q_ref[:, cols[a]] for a in heads]

    kk = lax.broadcasted_iota(jnp.int32, (blk, tq), 0)
    qq = lax.broadcasted_iota(jnp.int32, (blk, tq), 1)
    causal = kk <= qq

    lane = lax.broadcasted_iota(jnp.int32, (tq, LANES), 1)
    q_pos = lax.broadcasted_iota(jnp.int32, (tq, LANES), 0).astype(F32)
    q_aug = []
    for a in heads:
        parts = (_split3_bf16(jnp.full((tq, LANES), slope2[a], F32))
                 + _split3_bf16(-q_pos * slope2[a]))
        extra = jnp.zeros((tq, LANES), F32)
        for e, part in enumerate(parts):
            extra = jnp.where(lane == e, part.astype(F32), extra)
        q_aug.append(jnp.concatenate([qs[a], extra.astype(BF16)], axis=1))

    def window_start(g):
        return pl.multiple_of((qi - g * group + 1) * blk, blk)

    def score_group(g, slot):
        keys = pl.ds(window_start(g), group * blk)
        for a in heads:
            lhs = jnp.concatenate([k_ref[keys, cols[a]], kaug_ref[...]], axis=1)
            s = lax.dot_general(lhs, q_aug[a], nt, preferred_element_type=F32)
            for t in range(group):
                u = s[t * blk:(t + 1) * blk]
                if t == group - 1:
                    u = jnp.where(causal | (g != 0), u, NEG_INF)
                u_refs[slot][a, t] = u
                top_refs[slot][a, t] = jnp.max(u, axis=0, keepdims=True)

    score_group(0, 0)
    for a in heads:
        m_ref[a] = jnp.full((1, tq), NEG_INF, F32)
        acc_ref[a] = jnp.zeros(acc_ref.shape[1:], F32)

    blk_ids = lax.broadcasted_iota(jnp.int32, (n_blk, tq), 0)
    sels = []
    for a in heads:
        gate = lax.dot_general(km_ref[:, cols[a]].astype(BF16), qs[a], nt,
                               preferred_element_type=F32)
        gate = jnp.where(blk_ids < qi, gate, NEG_INF)
        picks = []
        for _ in range(n_top):
            best = jnp.max(gate, axis=0, keepdims=True)
            idx = jnp.min(jnp.where(gate == best, blk_ids, n_blk), axis=0, keepdims=True)
            picks.append(jnp.where(idx < qi, idx, n_blk))
            gate = jnp.where(blk_ids == idx, -jnp.inf, gate)
        sels.append(picks)

    def consume_group(g, slot):
        keys = pl.ds(window_start(g), group * blk)
        for a in heads:
            m_old = m_ref[a]
            m_new = m_old
            hits, offs = [], []
            for t in range(group):
                j = qi - g * group - (group - 1) + t
                hit = (sels[a][0] == j) | (j == qi)
                for sel in sels[a][1:]:
                    hit = hit | (sel == j)
                off = -slope2[a] * ((qi - j) * blk).astype(F32)
                m_new = jnp.maximum(m_new,
                                    jnp.where(hit, top_refs[slot][a, t] + off, NEG_INF))
                hits.append(hit); offs.append(off)
            ps = []
            for t in range(group):
                shift = jnp.where(hits[t], m_new - offs[t], -NEG_INF)
                ps.append(jnp.exp2((u_refs[slot][a, t] - shift).astype(BF16)))
            acc_ref[a] = (jnp.exp2(m_old - m_new) * acc_ref[a]
                          + jnp.dot(vt_ref[a, :, keys], jnp.concatenate(ps, axis=0),
                                    preferred_element_type=F32))
            m_ref[a] = m_new

    def group_body(g, carry):
        for slot in range(2):
            @pl.when(g % 2 == slot)
            def _():
                score_group(g + 1, 1 - slot)
                consume_group(g, slot)
        return carry

    lax.fori_loop(0, n_groups - 1, group_body, 0)
    for slot in range(2):
        @pl.when((n_groups - 1) % 2 == slot)
        def _():
            consume_group(n_groups - 1, slot)

    for a in heads:
        acc = acc_ref[a]
        o_ref[:, cols[a]] = (acc[0:hd] / acc[hd:hd + 1]).T.astype(o_ref.dtype)


def moba_attention(q, k, vt, k_mean, n_heads):
    b, s, d = q.shape
    hd = d // n_heads
    rows = vt.shape[2]
    blk = MOBA_BLOCK
    s_pad = k.shape[1]
    assert s_pad == s + MOBA_GROUP * blk
    n_blk = s // blk
    n_top = min(MOBA_TOPK, n_blk)
    slopes = jnp.exp2(-8.0 * jnp.arange(1, n_heads + 1, dtype=F32) / n_heads)
    key_pos = (jnp.arange(MOBA_GROUP * blk, dtype=jnp.int32) % blk).astype(F32)[:, None]
    lane = jnp.arange(LANES, dtype=jnp.int32)[None, :]
    key_aug = jnp.where(lane < 3, key_pos, jnp.where(lane < 6, 1.0, 0.0)).astype(BF16)
    hps = MOBA_HEADS_PER_STEP
    wide = hps * hd
    once = pl.Buffered(1)
    grid_spec = pltpu.PrefetchScalarGridSpec(
        num_scalar_prefetch=1,
        grid=(b, n_heads // hps, n_blk),
        in_specs=[
            pl.BlockSpec((None, blk, wide), lambda bi, h, i, *_: (bi, i, h)),
            pl.BlockSpec((None, s_pad, wide), lambda bi, h, i, *_: (bi, 0, h),
                         pipeline_mode=once),
            pl.BlockSpec((None, hps, rows, s_pad), lambda bi, h, i, *_: (bi, h, 0, 0),
                         pipeline_mode=once),
            pl.BlockSpec((None, n_blk, wide), lambda bi, h, i, *_: (bi, 0, h)),
            pl.BlockSpec(key_aug.shape, lambda bi, h, i, *_: (0, 0), pipeline_mode=once),
        ],
        out_specs=pl.BlockSpec((None, blk, wide), lambda bi, h, i, *_: (bi, i, h)),
        scratch_shapes=[pltpu.VMEM((hps, 1, blk), F32),
                        pltpu.VMEM((hps, rows, blk), F32),
                        pltpu.VMEM((hps, MOBA_GROUP, blk, blk), F32),
                        pltpu.VMEM((hps, MOBA_GROUP, blk, blk), F32),
                        pltpu.VMEM((hps, MOBA_GROUP, 1, blk), F32),
                        pltpu.VMEM((hps, MOBA_GROUP, 1, blk), F32)],
    )
    return pl.pallas_call(
        functools.partial(_moba_kernel, blk=blk, n_top=n_top, group=MOBA_GROUP),
        grid_spec=grid_spec,
        out_shape=jax.ShapeDtypeStruct((b, s, d), BF16),
        compiler_params=_params("parallel", "parallel", "arbitrary"),
        name="moba_attention",
    )(slopes, q, k, vt, k_mean, key_aug)


def _dot_hi(a, b):
    return jnp.dot(a, b, preferred_element_type=F32, precision=lax.Precision.HIGHEST)


def _split_bf16(a):
    hi = a.astype(BF16)
    return hi, (a - hi.astype(F32)).astype(BF16)


def _dot_split_twin(a2, b2):
    a_hi, a_lo = _split_bf16(a2)
    b_hi, b_lo = _split_bf16(b2)
    lhs = jnp.concatenate([a_hi, a_lo], axis=1)
    rhs = jnp.concatenate([b_hi, b_lo, b_hi, jnp.zeros_like(b_hi)], axis=0)
    return jnp.dot(lhs, rhs, preferred_element_type=F32)


def _gdn_gates_kernel(sm_ref, alog_ref, dt_ref, o_ref, *, cs, n_heads):
    small = sm_ref[...]
    tm = small.shape[0]
    sp = small + dt_ref[...]
    softplus = jnp.maximum(sp, 0.0) + jnp.log1p(jnp.exp(-jnp.abs(sp)))
    g = -jnp.exp(alog_ref[...]) * softplus
    ii = lax.broadcasted_iota(jnp.int32, (tm, tm), 0)
    jj = lax.broadcasted_iota(jnp.int32, (tm, tm), 1)
    same_chunk_prefix = ((ii // cs) == (jj // cs)) & (ii >= jj)
    gc = _dot_hi(same_chunk_prefix.astype(F32), g)
    lane = lax.broadcasted_iota(jnp.int32, small.shape, 1)
    o_ref[...] = jnp.where(lane < n_heads, jax.nn.sigmoid(small), gc)


def gdn_gates(proj2, a_log, dt_bias, n_heads, tm):
    t, n = proj2.shape
    last = n // LANES - 1
    pad = (n_heads, LANES - 2 * n_heads)
    alog_vec = jnp.pad(a_log.astype(F32), pad).reshape(1, LANES)
    dt_vec = jnp.pad(dt_bias.astype(F32), pad).reshape(1, LANES)
    return pl.pallas_call(
        functools.partial(_gdn_gates_kernel, cs=GDN_CHUNK, n_heads=n_heads),
        grid=(t // tm,),
        in_specs=[pl.BlockSpec((tm, LANES), lambda i: (i, last)),
                  _resident((1, LANES)), _resident((1, LANES))],
        out_specs=pl.BlockSpec((tm, LANES), lambda i: (i, 0)),
        out_shape=jax.ShapeDtypeStruct((t, LANES), F32),
        compiler_params=_params("parallel"),
        name="gdn_gates",
    )(proj2, alog_vec, dt_vec)


def _gdn_kernel(q_ref, k_ref, v_ref, z_ref, gate_ref, gcr_ref, cwq_ref, cwk_ref, cwv_ref,
                nw_ref, o_ref, qbuf, kbuf, vbuf, tail_ref, state_ref, *, cs, width, n_heads):
    ts = q_ref.shape[0]
    hps = state_ref.shape[0]
    dh = q_ref.shape[1] // hps
    nc = ts // cs
    heads = range(hps)
    cols = [slice(a * dh, (a + 1) * dh) for a in heads]
    chunks = [slice(c * cs, (c + 1) * cs) for c in range(nc)]
    pairs = [(a, c) for a in heads for c in range(nc)]
    head0 = pl.program_id(1) * hps

    @pl.when(pl.program_id(2) == 0)
    def _():
        tail_ref[...] = jnp.zeros(tail_ref.shape, F32)
        state_ref[...] = jnp.zeros(state_ref.shape, F32)

    q_all = _silu(_causal_conv_rows(qbuf, tail_ref.at[0], q_ref[...], cwq_ref[...], width, ts))
    k_all = _silu(_causal_conv_rows(kbuf, tail_ref.at[1], k_ref[...], cwk_ref[...], width, ts))
    v_all = _silu(_causal_conv_rows(vbuf, tail_ref.at[2], v_ref[...], cwv_ref[...], width, ts))

    gates = gate_ref[...]
    lane = lax.broadcasted_iota(jnp.int32, gates.shape, 1)

    ii = lax.broadcasted_iota(jnp.int32, (cs, 2 * cs), 0)
    jj = lax.broadcasted_iota(jnp.int32, (cs, 2 * cs), 1) % cs
    incl = ii >= jj
    strict = ii > jj
    eye = (ii == jj).astype(F32)
    nt = (((1,), (1,)), ((), ()))

    ks, gcs, k16s, q16s, v_betas, kb_decs, q_decs, k_betas = [], [], [], [], [], [], [], []
    for a in heads:
        q = q_all[:, cols[a]]
        k = k_all[:, cols[a]]
        q = q * lax.rsqrt(jnp.sum(q * q, axis=-1, keepdims=True) + NORM_EPS) * (dh ** -0.5)
        k = k * lax.rsqrt(jnp.sum(k * k, axis=-1, keepdims=True) + NORM_EPS)
        beta = jnp.sum(jnp.where(lane == head0 + a, gates, 0.0), axis=1, keepdims=True)
        gc = jnp.sum(jnp.where(lane == n_heads + head0 + a, gates, 0.0), axis=1,
                     keepdims=True)
        e_gc = jnp.exp(gc)
        k_beta = k * beta
        ks.append(k); gcs.append(gc)
        v_betas.append((v_all[:, cols[a]] * beta).astype(BF16))
        kb_decs.append((k_beta * e_gc).astype(BF16))
        q_decs.append((q * e_gc).astype(BF16))
        k_betas.append(k_beta.astype(BF16))
        k16s.append(k.astype(BF16))
        q16s.append(q.astype(BF16))

    decays, negs = {}, {}
    for a, c in pairs:
        sl = chunks[c]
        diff = gcs[a][sl] - gcr_ref[a, c:c + 1, :]
        decay = jnp.where(incl, jnp.exp(jnp.where(incl, diff, 0.0)), 0.0)
        k_twin = jnp.concatenate([k16s[a][sl], k16s[a][sl]], axis=0)
        kk = lax.dot_general(k_betas[a][sl], k_twin, nt, preferred_element_type=F32)
        decays[a, c] = decay
        negs[a, c] = jnp.where(strict, -(kk * decay), 0.0)
    t_mats = {p: eye + negs[p] for p in pairs}
    powers = negs
    span = 2
    while span < cs:
        powers = {p: _dot_split_twin(powers[p], powers[p]) for p in pairs}
        t_mats = {p: t_mats[p] + _dot_split_twin(t_mats[p], powers[p]) for p in pairs}
        span *= 2

    us, w16s, kus, kws, wqs, attns, sds = {}, {}, {}, {}, {}, {}, {}
    for a, c in pairs:
        sl = chunks[c]
        t16 = t_mats[a, c][:, 0:cs].astype(BF16)
        us[a, c] = jnp.dot(t16, v_betas[a][sl], preferred_element_type=F32)
        w16s[a, c] = jnp.dot(t16, kb_decs[a][sl], preferred_element_type=F32).astype(BF16)
        qk = lax.dot_general(q16s[a][sl], k16s[a][sl], nt, preferred_element_type=F32)
        attns[a, c] = (qk * decays[a, c][:, 0:cs]).astype(BF16)
    for a, c in pairs:
        sl = chunks[c]
        gc = gcs[a][sl]
        g_last = gc[cs - 1:cs, :]
        kd_t = (ks[a][sl] * jnp.exp(g_last - gc)).T.astype(BF16)
        kws[a, c] = jnp.dot(kd_t, w16s[a, c], preferred_element_type=F32).astype(BF16)
        kus[a, c] = jnp.dot(kd_t, us[a, c].astype(BF16), preferred_element_type=F32)
        wqs[a, c] = jnp.concatenate([w16s[a, c], q_decs[a][sl]], axis=0)
        sds[a, c] = jnp.exp(g_last)

    def emit_output(c, a, inter, v_new):
        sl = chunks[c]
        o = inter + jnp.dot(attns[a, c], v_new.astype(BF16), preferred_element_type=F32)
        o = o * lax.rsqrt(jnp.mean(o * o, axis=-1, keepdims=True) + NORM_EPS) * nw_ref[...]
        o_ref[sl, cols[a]] = (o * _silu(z_ref[sl, cols[a]])).astype(o_ref.dtype)

    states = [state_ref[a] for a in heads]
    pending = None
    for c in range(nc):
        prods = []
        for a in heads:
            s16 = states[a].astype(BF16)
            prods.append((jnp.dot(kws[a, c], s16, preferred_element_type=F32),
                          jnp.dot(wqs[a, c], s16, preferred_element_type=F32)))
        if pending is not None:
            for a in heads:
                emit_output(c - 1, a, *pending[a])
        pending = []
        for a in heads:
            decayed, prod = prods[a]
            states[a] = states[a] * sds[a, c] - decayed + kus[a, c]
            pending.append((prod[cs:2 * cs], us[a, c] - prod[0:cs]))
    for a in heads:
        emit_output(nc - 1, a, *pending[a])
        state_ref[a] = states[a]


def gdn_core(proj, gates, conv_w, norm_w, n_heads, ts):
    b, s, _ = proj.shape
    dh = norm_w.shape[0]
    d = n_heads * dh
    width = conv_w.shape[0]
    cs = GDN_CHUNK
    gc_rows = gates[:, :, n_heads:2 * n_heads].reshape(b, s // cs, cs, n_heads)
    gc_rows = gc_rows.transpose(0, 3, 1, 2)
    gc_rows = jnp.concatenate([gc_rows, gc_rows], axis=-1)

    hps = GDN_HEADS_PER_STEP
    wide = hps * dh
    groups = n_heads // hps

    def col(section):
        return pl.BlockSpec((None, ts, wide), lambda bi, h, i: (bi, i, section * groups + h))

    def cw(section):
        return pl.BlockSpec((width, wide), lambda bi, h, i: (0, section * groups + h))

    return pl.pallas_call(
        functools.partial(_gdn_kernel, cs=GDN_CHUNK, width=width, n_heads=n_heads),
        grid=(b, groups, s // ts),
        in_specs=[col(0), col(1), col(2), col(3),
                  pl.BlockSpec((None, ts, LANES), lambda bi, h, i: (bi, i, 0)),
                  pl.BlockSpec((None, hps, ts // cs, 2 * cs), lambda bi, h, i: (bi, h, i, 0)),
                  cw(0), cw(1), cw(2), _resident((1, dh))],
        out_specs=pl.BlockSpec((None, ts, wide), lambda bi, h, i: (bi, i, h)),
        out_shape=jax.ShapeDtypeStruct((b, s, d), BF16),
        scratch_shapes=[pltpu.VMEM((ts + SUBLANES, wide), F32)] * 3
                       + [pltpu.VMEM((3, SUBLANES, wide), F32),
                          pltpu.VMEM((hps, dh, dh), F32)],
        compiler_params=_params("parallel", "parallel", "arbitrary"),
        name="gdn_core",
    )(proj, proj, proj, proj, gates, gc_rows, conv_w, conv_w, conv_w,
      norm_w.astype(F32).reshape(1, dh))


def _row_tile(n, target):
    t = min(n, target)
    while n % t:
        t //= 2
    return t


def kernel(x, mix_norm, ffn_norm, final_norm, moba_w_qkv, moba_w_o, sconv_w_in, sconv_conv,
           sconv_w_out, gdn_w_in, gdn_conv, gdn_a_log, gdn_dt_bias, gdn_norm, gdn_w_o,
           ffn_w_up, ffn_conv, ffn_w_down):
    bsz, seq, d = x.shape
    depth = mix_norm.shape[0]
    t = bsz * seq
    tm = _row_tile(seq, 512)
    for i in range(depth):
        kind, j = i % N_MIXERS, i // N_MIXERS
        if kind == 0:
            q, k, vt, k_mean = moba_proj(x, mix_norm[i], moba_w_qkv[j].astype(BF16),
                                         MOBA_HEADS, tm)
            o = moba_attention(q, k, vt, k_mean.reshape(bsz, seq // MOBA_BLOCK, d),
                               MOBA_HEADS)
            x = proj_residual(x.reshape(t, d), o.reshape(t, d), moba_w_o[j].astype(BF16),
                              tm).reshape(bsz, seq, d)
        elif kind == 1:
            x = short_conv_layer(x, mix_norm[i], sconv_w_in[j].astype(BF16), sconv_conv[j],
                                 sconv_w_out[j].astype(BF16), tm)
        else:
            n_in = gdn_w_in.shape[-1]
            w_in = jnp.pad(gdn_w_in[j], ((0, 0), (0, 4 * d + LANES - n_in))).astype(BF16)
            proj = norm_matmul(x.reshape(t, d), mix_norm[i], w_in, F32, tm)
            gates = gdn_gates(proj, gdn_a_log[j], gdn_dt_bias[j], GDN_HEADS, tm)
            o = gdn_core(proj.reshape(bsz, seq, 4 * d + LANES), gates.reshape(bsz, seq, LANES),
                         gdn_conv[j], gdn_norm[j], GDN_HEADS, _row_tile(seq, 512))
            x = proj_residual(x.reshape(t, d), o.reshape(t, d), gdn_w_o[j].astype(BF16),
                              tm).reshape(bsz, seq, d)
        x = conv_ffn_layer(x, ffn_norm[i], ffn_w_up[i].astype(BF16), ffn_conv[i],
                           ffn_w_down[i].astype(BF16),
                           final_norm if i == depth - 1 else None, tm, 256)
    return x
```

```python
import functools

import jax
import jax.numpy as jnp
from jax import lax
from jax.experimental import pallas as pl
from jax.experimental.pallas import tpu as pltpu

F32 = jnp.float32
BF16 = jnp.bfloat16

N_MIXERS = 3
MOBA_HEADS = 8
MOBA_BLOCK = 256
MOBA_TOPK = 3
GDN_HEADS = 8
GDN_CHUNK = 64
NORM_EPS = 1e-6
NEG_INF = -1e30
LOG2E = 1.4426950408889634
MOBA_GROUP = 4
MOBA_HEADS_PER_STEP = 4
GDN_HEADS_PER_STEP = 4
ROW_TILE = 512
FFN_ROW_TILE = 1024
FFN_COL_TILE = 256
GDN_ROW_TILE = 512

LANES = 128
SUBLANES = 8
BF16_SUBLANES = 16
VMEM_LIMIT = 56 * 1024 * 1024


def _params(*sem):
    return pltpu.CompilerParams(dimension_semantics=sem, vmem_limit_bytes=VMEM_LIMIT)


def _rmsnorm(x, g):
    return x * lax.rsqrt(jnp.mean(x * x, axis=-1, keepdims=True) + NORM_EPS) * g


def _silu(x):
    return x * jax.nn.sigmoid(x)


def _resident(shape):
    return pl.BlockSpec(shape, lambda *_: (0,) * len(shape), pipeline_mode=pl.Buffered(1))


def _norm_matmul_kernel(x_ref, g_ref, w_ref, o_ref, *km_ref, km_cols, km_rows):
    hn = _rmsnorm(x_ref[...], g_ref[...]).astype(BF16)
    y = jnp.dot(hn, w_ref[...], preferred_element_type=F32)
    o_ref[...] = y.astype(o_ref.dtype)
    if km_ref:
        lo, hi = km_cols
        yk = y[:, lo:hi]
        tm = yk.shape[0]
        km = jnp.mean(yk.reshape(tm // km_rows, km_rows, hi - lo), axis=1)
        km_ref[0][...] = km[:, None, :]


def norm_matmul(x2, g, w, out_dtype, tm, km_cols=None, km_rows=None):
    t, d = x2.shape
    n = w.shape[1]
    out_shape = [jax.ShapeDtypeStruct((t, n), out_dtype)]
    out_specs = [pl.BlockSpec((tm, n), lambda i: (i, 0))]
    if km_cols is not None:
        nk = km_cols[1] - km_cols[0]
        out_shape.append(jax.ShapeDtypeStruct((t // km_rows, 1, nk), F32))
        out_specs.append(pl.BlockSpec((tm // km_rows, 1, nk), lambda i: (i, 0, 0)))
    res = pl.pallas_call(
        functools.partial(_norm_matmul_kernel, km_cols=km_cols, km_rows=km_rows),
        grid=(t // tm,),
        in_specs=[pl.BlockSpec((tm, d), lambda i: (i, 0)),
                  _resident((1, d)),
                  _resident((d, n))],
        out_specs=out_specs,
        out_shape=out_shape,
        compiler_params=_params("parallel"),
        name="norm_matmul",
    )(x2, g.reshape(1, d), w)
    return res if km_cols is not None else res[0]


def _proj_residual_kernel(x_ref, o_ref, w_ref, out_ref):
    out_ref[...] = x_ref[...] + jnp.dot(o_ref[...], w_ref[...], preferred_element_type=F32)


def proj_residual(x2, o2, w, tm):
    t, d = x2.shape
    return pl.pallas_call(
        _proj_residual_kernel,
        grid=(t // tm,),
        in_specs=[pl.BlockSpec((tm, d), lambda i: (i, 0)),
                  pl.BlockSpec((tm, o2.shape[1]), lambda i: (i, 0)),
                  _resident(w.shape)],
        out_specs=pl.BlockSpec((tm, d), lambda i: (i, 0)),
        out_shape=jax.ShapeDtypeStruct((t, d), F32),
        compiler_params=_params("parallel"),
        name="proj_residual",
    )(x2, o2, w)


def _causal_conv_rows(buf_ref, tail_ref, cur, cw, width, tm):
    buf_ref[0:SUBLANES, :] = tail_ref[...]
    buf_ref[SUBLANES:SUBLANES + tm, :] = cur
    out = cw[width - 1:width, :] * cur
    for k in range(width - 1):
        off = SUBLANES - (width - 1) + k
        out = out + cw[k:k + 1, :] * buf_ref[off:off + tm, :]
    tail_ref[...] = cur[tm - SUBLANES:tm, :]
    return out


def _ffn_kernel(x_ref, g_ref, wup_ref, cw_ref, wdn_ref, *rest, ffn, tf, width, final):
    if final:
        fn_ref, out_ref, gbuf, ubuf, tail_ref, act_ref = rest
    else:
        out_ref, gbuf, ubuf, tail_ref, act_ref = rest
    tm = x_ref.shape[0]

    @pl.when(pl.program_id(1) == 0)
    def _():
        tail_ref[...] = jnp.zeros(tail_ref.shape, F32)

    x = x_ref[...]
    hn = _rmsnorm(x, g_ref[...]).astype(BF16)
    for c in range(ffn // tf):
        lo = c * tf
        hi = ffn + lo
        gate = jnp.dot(hn, wup_ref[:, lo:lo + tf], preferred_element_type=F32)
        up = jnp.dot(hn, wup_ref[:, hi:hi + tf], preferred_element_type=F32)
        gate = _causal_conv_rows(gbuf.at[c % 2], tail_ref.at[:, lo:lo + tf], gate,
                                 cw_ref[:, lo:lo + tf], width, tm)
        up = _causal_conv_rows(ubuf.at[c % 2], tail_ref.at[:, hi:hi + tf], up,
                               cw_ref[:, hi:hi + tf], width, tm)
        act_ref[:, lo:lo + tf] = (_silu(gate) * up).astype(BF16)
    res = x + jnp.dot(act_ref[...], wdn_ref[...], preferred_element_type=F32)
    if final:
        res = _rmsnorm(res, fn_ref[...])
    out_ref[...] = res


def conv_ffn_layer(x, g, w_up, conv_w, w_down, final_g, tm, tf):
    b, s, d = x.shape
    ffn = w_down.shape[0]
    width = conv_w.shape[0]
    final = final_g is not None
    row = pl.BlockSpec((None, tm, d), lambda bi, i: (bi, i, 0))
    in_specs = [row, _resident((1, d)), _resident(w_up.shape), _resident(conv_w.shape),
                _resident(w_down.shape)]
    args = [x, g.reshape(1, d), w_up, conv_w, w_down]
    if final:
        in_specs.append(_resident((1, d)))
        args.append(final_g.reshape(1, d))
    return pl.pallas_call(
        functools.partial(_ffn_kernel, ffn=ffn, tf=tf, width=width, final=final),
        grid=(b, s // tm),
        in_specs=in_specs,
        out_specs=row,
        out_shape=jax.ShapeDtypeStruct((b, s, d), F32),
        scratch_shapes=[pltpu.VMEM((2, tm + SUBLANES, tf), F32),
                        pltpu.VMEM((2, tm + SUBLANES, tf), F32),
                        pltpu.VMEM((SUBLANES, 2 * ffn), F32),
                        pltpu.VMEM((tm, ffn), BF16)],
        compiler_params=_params("parallel", "arbitrary"),
        name="conv_ffn",
    )(*args)


def _sconv_kernel(x_ref, g_ref, win_ref, cw_ref, wout_ref, out_ref, cbuf, tail_ref, *, width):
    tm, d = x_ref.shape

    @pl.when(pl.program_id(1) == 0)
    def _():
        tail_ref[...] = jnp.zeros(tail_ref.shape, F32)

    x = x_ref[...]
    hn = _rmsnorm(x, g_ref[...]).astype(BF16)
    b_gate = jnp.dot(hn, win_ref[:, 0:d], preferred_element_type=F32)
    c_gate = jnp.dot(hn, win_ref[:, d:2 * d], preferred_element_type=F32)
    xv = jnp.dot(hn, win_ref[:, 2 * d:3 * d], preferred_element_type=F32)
    y = b_gate * _causal_conv_rows(cbuf, tail_ref, c_gate * xv, cw_ref[...], width, tm)
    out_ref[...] = x + jnp.dot(y.astype(BF16), wout_ref[...], preferred_element_type=F32)


def short_conv_layer(x, g, w_in, conv_w, w_out, tm):
    b, s, d = x.shape
    row = pl.BlockSpec((None, tm, d), lambda bi, i: (bi, i, 0))
    return pl.pallas_call(
        functools.partial(_sconv_kernel, width=conv_w.shape[0]),
        grid=(b, s // tm),
        in_specs=[row, _resident((1, d)), _resident(w_in.shape), _resident(conv_w.shape),
                  _resident(w_out.shape)],
        out_specs=row,
        out_shape=jax.ShapeDtypeStruct((b, s, d), F32),
        scratch_shapes=[pltpu.VMEM((tm + SUBLANES, d), F32), pltpu.VMEM((SUBLANES, d), F32)],
        compiler_params=_params("parallel", "arbitrary"),
        name="short_conv",
    )(x, g.reshape(1, d), w_in, conv_w, w_out)


def _moba_proj_kernel(x_ref, g_ref, w_ref, q_ref, k_ref, vt_ref, km_ref,
                      *, blk, q_scale, pad_tiles):
    step = pl.program_id(1)

    @pl.when(step < pad_tiles)
    def _():
        k_ref[...] = jnp.zeros(k_ref.shape, BF16)
        vt_ref[...] = jnp.zeros(vt_ref.shape, BF16)

    @pl.when(step >= pad_tiles)
    def _():
        tm, d = x_ref.shape
        n_heads, rows, _ = vt_ref.shape
        hd = d // n_heads
        hn = _rmsnorm(x_ref[...], g_ref[...]).astype(BF16)
        y = jnp.dot(hn, w_ref[...], preferred_element_type=F32)
        q_ref[...] = (y[:, 0:d] * q_scale).astype(BF16)
        k_ref[...] = y[:, d:2 * d].astype(BF16)
        vt_ref[:, 0:hd, :] = y[:, 2 * d:3 * d].T.reshape(n_heads, hd, tm).astype(BF16)
        vt_ref[:, hd:rows, :] = jnp.ones((n_heads, rows - hd, tm), BF16)
        km = jnp.mean(y[:, d:2 * d].reshape(tm // blk, blk, d), axis=1)
        km_ref[...] = km[:, None, :]


def moba_proj(x, g, w, n_heads, tm):
    b, s, d = x.shape
    blk = MOBA_BLOCK
    hd = d // n_heads
    rows = hd + BF16_SUBLANES
    pad = MOBA_GROUP * blk
    pad_tiles = pad // tm
    assert pad_tiles * tm == pad

    def real(i):
        return jnp.maximum(i - pad_tiles, 0)

    return pl.pallas_call(
        functools.partial(_moba_proj_kernel, blk=blk, q_scale=hd ** -0.5 * LOG2E,
                          pad_tiles=pad_tiles),
        grid=(b, s // tm + pad_tiles),
        in_specs=[pl.BlockSpec((None, tm, d), lambda bi, i: (bi, real(i), 0)),
                  _resident((1, d)), _resident(w.shape)],
        out_specs=[pl.BlockSpec((None, tm, d), lambda bi, i: (bi, real(i), 0)),
                   pl.BlockSpec((None, tm, d), lambda bi, i: (bi, i, 0)),
                   pl.BlockSpec((None, n_heads, rows, tm), lambda bi, i: (bi, 0, 0, i)),
                   pl.BlockSpec((None, tm // blk, 1, d), lambda bi, i: (bi, real(i), 0, 0))],
        out_shape=[jax.ShapeDtypeStruct((b, s, d), BF16),
                   jax.ShapeDtypeStruct((b, s + pad, d), BF16),
                   jax.ShapeDtypeStruct((b, n_heads, rows, s + pad), BF16),
                   jax.ShapeDtypeStruct((b, s // blk, 1, d), F32)],
        compiler_params=_params("parallel", "arbitrary"),
        name="moba_proj",
    )(x, g.reshape(1, d), w)


def _split3_bf16(x):
    hi = x.astype(BF16)
    r = x - hi.astype(F32)
    mid = r.astype(BF16)
    return hi, mid, (r - mid.astype(F32)).astype(BF16)


def _moba_kernel(slopes_ref, q_ref, k_ref, vt_ref, km_ref, kaug_ref, o_ref, m_ref, acc_ref,
                 u0_ref, u1_ref, top0_ref, top1_ref, *, blk, n_top, group):
    qi = pl.program_id(2)
    tq = q_ref.shape[0]
    n_heads = vt_ref.shape[0]
    hd = q_ref.shape[1] // n_heads
    n_blk = km_ref.shape[0]
    nt = (((1,), (1,)), ((), ()))
    n_groups = (qi + group) // group
    u_refs = (u0_ref, u1_ref)
    top_refs = (top0_ref, top1_ref)
    heads = range(n_heads)
    cols = [slice(a * hd, (a + 1) * hd) for a in heads]
    slope2 = [slopes_ref[pl.program_id(1) * n_heads + a] * LOG2E for a in heads]
    qs = [q_ref[:, cols[a]] for a in heads]

    kk = lax.broadcasted_iota(jnp.int32, (blk, tq), 0)
    qq = lax.broadcasted_iota(jnp.int32, (blk, tq), 1)
    causal = kk <= qq

    lane = lax.broadcasted_iota(jnp.int32, (tq, LANES), 1)
    q_pos = lax.broadcasted_iota(jnp.int32, (tq, LANES), 0).astype(F32)
    q_aug = []
    for a in heads:
        parts = (_split3_bf16(jnp.full((tq, LANES), slope2[a], F32))
                 + _split3_bf16(-q_pos * slope2[a]))
        extra = jnp.zeros((tq, LANES), F32)
        for e, part in enumerate(parts):
            extra = jnp.where(lane == e, part.astype(F32), extra)
        q_aug.append(jnp.concatenate([qs[a], extra.astype(BF16)], axis=1))

    def window_start(g):
        return pl.multiple_of((qi - g * group + 1) * blk, blk)

    def score_group(g, slot):
        keys = pl.ds(window_start(g), group * blk)
        for a in heads:
            lhs = jnp.concatenate([k_ref[keys, cols[a]], kaug_ref[...]], axis=1)
            s = lax.dot_general(lhs, q_aug[a], nt, preferred_element_type=F32)
            for t in range(group):
                u = s[t * blk:(t + 1) * blk]
                if t == group - 1:
                    u = jnp.where(causal | (g != 0), u, NEG_INF)
                u_refs[slot][a, t] = u
                top_refs[slot][a, t] = jnp.max(u, axis=0, keepdims=True)

    score_group(0, 0)
    for a in heads:
        m_ref[a] = jnp.full((1, tq), NEG_INF, F32)
        acc_ref[a] = jnp.zeros(acc_ref.shape[1:], F32)

    blk_ids = lax.broadcasted_iota(jnp.int32, (n_blk, tq), 0)
    sels = []
    for a in heads:
        gate = lax.dot_general(km_ref[:, cols[a]].astype(BF16), qs[a], nt,
                               preferred_element_type=F32)
        gate = jnp.where(blk_ids < qi, gate, NEG_INF)
        picks = []
        for _ in range(n_top):
            best = jnp.max(gate, axis=0, keepdims=True)
            idx = jnp.min(jnp.where(gate == best, blk_ids, n_blk), axis=0, keepdims=True)
            picks.append(jnp.where(idx < qi, idx, n_blk))
            gate = jnp.where(blk_ids == idx, -jnp.inf, gate)
        sels.append(picks)

    def consume_group(g, slot):
        keys = pl.ds(window_start(g), group * blk)
        for a in heads:
            m_old = m_ref[a]
            m_new = m_old
            hits, offs = [], []
            for t in range(group):
                j = qi - g * group - (group - 1) + t
                hit = (sels[a][0] == j) | (j == qi)
                for sel in sels[a][1:]:
                    hit = hit | (sel == j)
                off = -slope2[a] * ((qi - j) * blk).astype(F32)
                m_new = jnp.maximum(m_new,
                                    jnp.where(hit, top_refs[slot][a, t] + off, NEG_INF))
                hits.append(hit); offs.append(off)
            ps = []
            for t in range(group):
                shift = jnp.where(hits[t], m_new - offs[t], -NEG_INF)
                ps.append(jnp.exp2((u_refs[slot][a, t] - shift).astype(BF16)))
            acc_ref[a] = (jnp.exp2(m_old - m_new) * acc_ref[a]
                          + jnp.dot(vt_ref[a, :, keys], jnp.concatenate(ps, axis=0),
                                    preferred_element_type=F32))
            m_ref[a] = m_new

    def group_body(g, carry):
        for slot in range(2):
            @pl.when(g % 2 == slot)
            def _():
                score_group(g + 1, 1 - slot)
                consume_group(g, slot)
        return carry

    lax.fori_loop(0, n_groups - 1, group_body, 0)
    for slot in range(2):
        @pl.when((n_groups - 1) % 2 == slot)
        def _():
            consume_group(n_groups - 1, slot)

    for a in heads:
        acc = acc_ref[a]
        o_ref[:, cols[a]] = (acc[0:hd] / acc[hd:hd + 1]).T.astype(o_ref.dtype)


def moba_attention(q, k, vt, k_mean, n_heads):
    b, s, d = q.shape
    hd = d // n_heads
    rows = vt.shape[2]
    blk = MOBA_BLOCK
    s_pad = k.shape[1]
    assert s_pad == s + MOBA_GROUP * blk
    n_blk = s // blk
    n_top = min(MOBA_TOPK, n_blk)
    slopes = jnp.exp2(-8.0 * jnp.arange(1, n_heads + 1, dtype=F32) / n_heads)
    key_pos = (jnp.arange(MOBA_GROUP * blk, dtype=jnp.int32) % blk).astype(F32)[:, None]
    lane = jnp.arange(LANES, dtype=jnp.int32)[None, :]
    key_aug = jnp.where(lane < 3, key_pos, jnp.where(lane < 6, 1.0, 0.0)).astype(BF16)
    hps = MOBA_HEADS_PER_STEP
    wide = hps * hd
    once = pl.Buffered(1)
    grid_spec = pltpu.PrefetchScalarGridSpec(
        num_scalar_prefetch=1,
        grid=(b, n_heads // hps, n_blk),
        in_specs=[
            pl.BlockSpec((None, blk, wide), lambda bi, h, i, *_: (bi, i, h)),
            pl.BlockSpec((None, s_pad, wide), lambda bi, h, i, *_: (bi, 0, h),
                         pipeline_mode=once),
            pl.BlockSpec((None, hps, rows, s_pad), lambda bi, h, i, *_: (bi, h, 0, 0),
                         pipeline_mode=once),
            pl.BlockSpec((None, n_blk, wide), lambda bi, h, i, *_: (bi, 0, h)),
            pl.BlockSpec(key_aug.shape, lambda bi, h, i, *_: (0, 0), pipeline_mode=once),
        ],
        out_specs=pl.BlockSpec((None, blk, wide), lambda bi, h, i, *_: (bi, i, h)),
        scratch_shapes=[pltpu.VMEM((hps, 1, blk), F32),
                        pltpu.VMEM((hps, rows, blk), F32),
                        pltpu.VMEM((hps, MOBA_GROUP, blk, blk), F32),
                        pltpu.VMEM((hps, MOBA_GROUP, blk, blk), F32),
                        pltpu.VMEM((hps, MOBA_GROUP, 1, blk), F32),
                        pltpu.VMEM((hps, MOBA_GROUP, 1, blk), F32)],
    )
    return pl.pallas_call(
        functools.partial(_moba_kernel, blk=blk, n_top=n_top, group=MOBA_GROUP),
        grid_spec=grid_spec,
        out_shape=jax.ShapeDtypeStruct((b, s, d), BF16),
        compiler_params=_params("parallel", "parallel", "arbitrary"),
        name="moba_attention",
    )(slopes, q, k, vt, k_mean, key_aug)


def _dot_hi(a, b):
    return jnp.dot(a, b, preferred_element_type=F32, precision=lax.Precision.HIGHEST)


def _split_bf16(a):
    hi = a.astype(BF16)
    return hi, (a - hi.astype(F32)).astype(BF16)


def _dot_split(a, b):
    a_hi, a_lo = _split_bf16(a)
    b_hi, b_lo = _split_bf16(b)
    return (jnp.dot(a_hi, b_hi, preferred_element_type=F32)
            + jnp.dot(a_hi, b_lo, preferred_element_type=F32)
            + jnp.dot(a_lo, b_hi, preferred_element_type=F32))


def _gdn_gates_kernel(sm_ref, alog_ref, dt_ref, o_ref, *, cs, n_heads):
    small = sm_ref[...]
    tm = small.shape[0]
    sp = small + dt_ref[...]
    softplus = jnp.maximum(sp, 0.0) + jnp.log1p(jnp.exp(-jnp.abs(sp)))
    g = -jnp.exp(alog_ref[...]) * softplus
    ii = lax.broadcasted_iota(jnp.int32, (tm, tm), 0)
    jj = lax.broadcasted_iota(jnp.int32, (tm, tm), 1)
    same_chunk_prefix = ((ii // cs) == (jj // cs)) & (ii >= jj)
    gc = _dot_hi(same_chunk_prefix.astype(F32), g)
    lane = lax.broadcasted_iota(jnp.int32, small.shape, 1)
    o_ref[...] = jnp.where(lane < n_heads, jax.nn.sigmoid(small), gc)


def gdn_gates(proj2, a_log, dt_bias, n_heads, tm):
    t, n = proj2.shape
    last = n // LANES - 1
    pad = (n_heads, LANES - 2 * n_heads)
    alog_vec = jnp.pad(a_log.astype(F32), pad).reshape(1, LANES)
    dt_vec = jnp.pad(dt_bias.astype(F32), pad).reshape(1, LANES)
    return pl.pallas_call(
        functools.partial(_gdn_gates_kernel, cs=GDN_CHUNK, n_heads=n_heads),
        grid=(t // tm,),
        in_specs=[pl.BlockSpec((tm, LANES), lambda i: (i, last)),
                  _resident((1, LANES)), _resident((1, LANES))],
        out_specs=pl.BlockSpec((tm, LANES), lambda i: (i, 0)),
        out_shape=jax.ShapeDtypeStruct((t, LANES), F32),
        compiler_params=_params("parallel"),
        name="gdn_gates",
    )(proj2, alog_vec, dt_vec)


def _gdn_kernel(q_ref, k_ref, v_ref, z_ref, gate_ref, gcr_ref, cwq_ref, cwk_ref, cwv_ref,
                nw_ref, o_ref, qbuf, kbuf, vbuf, tail_ref, state_ref, *, cs, width, n_heads):
    ts = q_ref.shape[0]
    hps = state_ref.shape[0]
    dh = q_ref.shape[1] // hps
    nc = ts // cs
    heads = range(hps)
    cols = [slice(a * dh, (a + 1) * dh) for a in heads]
    chunks = [slice(c * cs, (c + 1) * cs) for c in range(nc)]
    pairs = [(a, c) for a in heads for c in range(nc)]
    head0 = pl.program_id(1) * hps

    @pl.when(pl.program_id(2) == 0)
    def _():
        tail_ref[...] = jnp.zeros(tail_ref.shape, F32)
        state_ref[...] = jnp.zeros(state_ref.shape, F32)

    q_all = _silu(_causal_conv_rows(qbuf, tail_ref.at[0], q_ref[...], cwq_ref[...], width, ts))
    k_all = _silu(_causal_conv_rows(kbuf, tail_ref.at[1], k_ref[...], cwk_ref[...], width, ts))
    v_all = _silu(_causal_conv_rows(vbuf, tail_ref.at[2], v_ref[...], cwv_ref[...], width, ts))

    gates = gate_ref[...]
    lane = lax.broadcasted_iota(jnp.int32, gates.shape, 1)

    ii = lax.broadcasted_iota(jnp.int32, (cs, cs), 0)
    jj = lax.broadcasted_iota(jnp.int32, (cs, cs), 1)
    incl = ii >= jj
    strict = ii > jj
    eye = (ii == jj).astype(F32)
    nt = (((1,), (1,)), ((), ()))

    ks, gcs, k16s, kbq16s, vks, q_decs = [], [], [], [], [], []
    for a in heads:
        q = q_all[:, cols[a]]
        k = k_all[:, cols[a]]
        q = q * lax.rsqrt(jnp.sum(q * q, axis=-1, keepdims=True) + NORM_EPS) * (dh ** -0.5)
        k = k * lax.rsqrt(jnp.sum(k * k, axis=-1, keepdims=True) + NORM_EPS)
        beta = jnp.sum(jnp.where(lane == head0 + a, gates, 0.0), axis=1, keepdims=True)
        gc = jnp.sum(jnp.where(lane == n_heads + head0 + a, gates, 0.0), axis=1,
                     keepdims=True)
        e_gc = jnp.exp(gc)
        k_beta = k * beta
        ks.append(k); gcs.append(gc)
        vks.append(jnp.concatenate([(v_all[:, cols[a]] * beta).astype(BF16),
                                    (k_beta * e_gc).astype(BF16)], axis=1))
        q_decs.append((q * e_gc).astype(BF16))
        kbq16s.append((k_beta.astype(BF16), q.astype(BF16)))
        k16s.append(k.astype(BF16))

    negs, attns = {}, {}
    for a, c in pairs:
        sl = chunks[c]
        diff = gcs[a][sl] - gcr_ref[a, c:c + 1, :]
        decay = jnp.where(incl, jnp.exp(jnp.where(incl, diff, 0.0)), 0.0)
        k_beta16, q16 = kbq16s[a]
        both = lax.dot_general(jnp.concatenate([k_beta16[sl], q16[sl]], axis=0), k16s[a][sl],
                               nt, preferred_element_type=F32)
        negs[a, c] = jnp.where(strict, -(both[0:cs] * decay), 0.0)
        attns[a, c] = (both[cs:2 * cs] * decay).astype(BF16)
    powers = {p: _dot_split(negs[p], negs[p]) for p in pairs}
    t_mats = {p: eye + negs[p] for p in pairs}
    span = 4
    while span <= cs:
        last = span == cs
        prods = {p: _dot_split(t_mats[p] if last else
                               jnp.concatenate([t_mats[p], powers[p]], axis=0), powers[p])
                 for p in pairs}
        t_mats = {p: t_mats[p] + prods[p][0:cs] for p in pairs}
        if not last:
            powers = {p: prods[p][cs:2 * cs] for p in pairs}
        span *= 2

    us, uws, kus, stacks, sds = {}, {}, {}, {}, {}
    for a, c in pairs:
        uws[a, c] = jnp.dot(t_mats[a, c].astype(BF16), vks[a][chunks[c]],
                            preferred_element_type=F32)
    for a, c in pairs:
        sl = chunks[c]
        gc = gcs[a][sl]
        g_last = gc[cs - 1:cs, :]
        kd_t = (ks[a][sl] * jnp.exp(g_last - gc)).T.astype(BF16)
        uw16 = uws[a, c].astype(BF16)
        kuw = jnp.dot(kd_t, uw16, preferred_element_type=F32)
        us[a, c] = uws[a, c][:, 0:dh]
        kus[a, c] = kuw[:, 0:dh]
        stacks[a, c] = jnp.concatenate([kuw[:, dh:2 * dh].astype(BF16), uw16[:, dh:2 * dh],
                                        q_decs[a][sl]], axis=0)
        sds[a, c] = jnp.exp(g_last)

    def emit_output(c, a, inter, v_new):
        sl = chunks[c]
        o = inter + jnp.dot(attns[a, c], v_new.astype(BF16), preferred_element_type=F32)
        o = o * lax.rsqrt(jnp.mean(o * o, axis=-1, keepdims=True) + NORM_EPS) * nw_ref[...]
        o_ref[sl, cols[a]] = (o * _silu(z_ref[sl, cols[a]])).astype(o_ref.dtype)

    states = [state_ref[a] for a in heads]
    pending = None
    for c in range(nc):
        prods = [jnp.dot(stacks[a, c], states[a].astype(BF16), preferred_element_type=F32)
                 for a in heads]
        if pending is not None:
            for a in heads:
                emit_output(c - 1, a, *pending[a])
        pending = []
        for a in heads:
            prod = prods[a]
            states[a] = states[a] * sds[a, c] - prod[0:dh] + kus[a, c]
            pending.append((prod[dh + cs:dh + 2 * cs], us[a, c] - prod[dh:dh + cs]))
    for a in heads:
        emit_output(nc - 1, a, *pending[a])
        state_ref[a] = states[a]


def gdn_core(proj, gates, conv_w, norm_w, n_heads, ts):
    b, s, _ = proj.shape
    dh = norm_w.shape[0]
    d = n_heads * dh
    width = conv_w.shape[0]
    cs = GDN_CHUNK
    gc_rows = gates[:, :, n_heads:2 * n_heads].reshape(b, s // cs, cs, n_heads)
    gc_rows = gc_rows.transpose(0, 3, 1, 2)

    hps = GDN_HEADS_PER_STEP
    wide = hps * dh
    groups = n_heads // hps

    def col(section):
        return pl.BlockSpec((None, ts, wide), lambda bi, h, i: (bi, i, section * groups + h))

    def cw(section):
        return pl.BlockSpec((width, wide), lambda bi, h, i: (0, section * groups + h))

    return pl.pallas_call(
        functools.partial(_gdn_kernel, cs=GDN_CHUNK, width=width, n_heads=n_heads),
        grid=(b, groups, s // ts),
        in_specs=[col(0), col(1), col(2), col(3),
                  pl.BlockSpec((None, ts, LANES), lambda bi, h, i: (bi, i, 0)),
                  pl.BlockSpec((None, hps, ts // cs, cs), lambda bi, h, i: (bi, h, i, 0)),
                  cw(0), cw(1), cw(2), _resident((1, dh))],
        out_specs=pl.BlockSpec((None, ts, wide), lambda bi, h, i: (bi, i, h)),
        out_shape=jax.ShapeDtypeStruct((b, s, d), BF16),
        scratch_shapes=[pltpu.VMEM((ts + SUBLANES, wide), F32)] * 3
                       + [pltpu.VMEM((3, SUBLANES, wide), F32),
                          pltpu.VMEM((hps, dh, dh), F32)],
        compiler_params=_params("parallel", "parallel", "arbitrary"),
        name="gdn_core",
    )(proj, proj, proj, proj, gates, gc_rows, conv_w, conv_w, conv_w,
      norm_w.astype(F32).reshape(1, dh))


def _row_tile(n, target):
    t = min(n, target)
    while n % t:
        t //= 2
    return t


def kernel(x, mix_norm, ffn_norm, final_norm, moba_w_qkv, moba_w_o, sconv_w_in, sconv_conv,
           sconv_w_out, gdn_w_in, gdn_conv, gdn_a_log, gdn_dt_bias, gdn_norm, gdn_w_o,
           ffn_w_up, ffn_conv, ffn_w_down):
    bsz, seq, d = x.shape
    depth = mix_norm.shape[0]
    t = bsz * seq
    tm = _row_tile(seq, ROW_TILE)
    ffn_tm = _row_tile(seq, FFN_ROW_TILE)
    for i in range(depth):
        kind, j = i % N_MIXERS, i // N_MIXERS
        if kind == 0:
            q, k, vt, k_mean = moba_proj(x, mix_norm[i], moba_w_qkv[j].astype(BF16),
                                         MOBA_HEADS, tm)
            o = moba_attention(q, k, vt, k_mean.reshape(bsz, seq // MOBA_BLOCK, d),
                               MOBA_HEADS)
            x = proj_residual(x.reshape(t, d), o.reshape(t, d), moba_w_o[j].astype(BF16),
                              tm).reshape(bsz, seq, d)
        elif kind == 1:
            x = short_conv_layer(x, mix_norm[i], sconv_w_in[j].astype(BF16), sconv_conv[j],
                                 sconv_w_out[j].astype(BF16), tm)
        else:
            n_in = gdn_w_in.shape[-1]
            w_in = jnp.pad(gdn_w_in[j], ((0, 0), (0, 4 * d + LANES - n_in))).astype(BF16)
            proj = norm_matmul(x.reshape(t, d), mix_norm[i], w_in, F32, tm)
            gates = gdn_gates(proj, gdn_a_log[j], gdn_dt_bias[j], GDN_HEADS, tm)
            o = gdn_core(proj.reshape(bsz, seq, 4 * d + LANES), gates.reshape(bsz, seq, LANES),
                         gdn_conv[j], gdn_norm[j], GDN_HEADS, _row_tile(seq, GDN_ROW_TILE))
            x = proj_residual(x.reshape(t, d), o.reshape(t, d), gdn_w_o[j].astype(BF16),
                              tm).reshape(bsz, seq, d)
        x = conv_ffn_layer(x, ffn_norm[i], ffn_w_up[i].astype(BF16), ffn_conv[i],
                           ffn_w_down[i].astype(BF16),
                           final_norm if i == depth - 1 else None, ffn_tm, FFN_COL_TILE)
    return x
```

```python
import functools

import jax
import jax.numpy as jnp
from jax import lax
from jax.experimental import pallas as pl
from jax.experimental.pallas import tpu as pltpu

F32 = jnp.float32
BF16 = jnp.bfloat16

N_MIXERS = 3
MOBA_HEADS = 8
MOBA_BLOCK = 256
MOBA_TOPK = 3
GDN_HEADS = 8
GDN_CHUNK = 64
NORM_EPS = 1e-6
NEG_INF = -1e30
LOG2E = 1.4426950408889634
MOBA_GROUP = 4
MOBA_HEADS_PER_STEP = 4
GDN_HEADS_PER_STEP = 4
ROW_TILE = 512
FFN_ROW_TILE = 512
FFN_COL_TILE = 256
GDN_ROW_TILE = 512

LANES = 128
SUBLANES = 8
BF16_SUBLANES = 16
VMEM_LIMIT = 56 * 1024 * 1024


def _params(*sem):
    return pltpu.CompilerParams(dimension_semantics=sem, vmem_limit_bytes=VMEM_LIMIT)


def _rmsnorm(x, g):
    return x * lax.rsqrt(jnp.mean(x * x, axis=-1, keepdims=True) + NORM_EPS) * g


def _silu(x):
    return x * jax.nn.sigmoid(x)


def _resident(shape):
    return pl.BlockSpec(shape, lambda *_: (0,) * len(shape), pipeline_mode=pl.Buffered(1))


def _gdn_proj_kernel(x_ref, g_ref, w_ref, main_ref, gate_ref):
    hn = _rmsnorm(x_ref[...], g_ref[...]).astype(BF16)
    y = jnp.dot(hn, w_ref[...], preferred_element_type=F32)
    n_main = main_ref.shape[1]
    main_ref[...] = y[:, 0:n_main].astype(main_ref.dtype)
    gate_ref[...] = y[:, n_main:]


def gdn_proj(x2, g, w, tm):
    t, d = x2.shape
    n = w.shape[1]
    n_main = n - LANES
    return pl.pallas_call(
        _gdn_proj_kernel,
        grid=(t // tm,),
        in_specs=[pl.BlockSpec((tm, d), lambda i: (i, 0)), _resident((1, d)),
                  _resident((d, n))],
        out_specs=[pl.BlockSpec((tm, n_main), lambda i: (i, 0)),
                   pl.BlockSpec((tm, LANES), lambda i: (i, 0))],
        out_shape=[jax.ShapeDtypeStruct((t, n_main), BF16),
                   jax.ShapeDtypeStruct((t, LANES), F32)],
        compiler_params=_params("parallel"),
        name="gdn_proj",
    )(x2, g.reshape(1, d), w)


def _causal_conv_rows(buf_ref, tail_ref, cur, cw, width, tm):
    buf_ref[0:SUBLANES, :] = tail_ref[...]
    buf_ref[SUBLANES:SUBLANES + tm, :] = cur
    out = cw[width - 1:width, :] * cur
    for k in range(width - 1):
        off = SUBLANES - (width - 1) + k
        out = out + cw[k:k + 1, :] * buf_ref[off:off + tm, :]
    tail_ref[...] = cur[tm - SUBLANES:tm, :]
    return out


def _ffn_kernel(x_ref, g_ref, wup_ref, cw_ref, wdn_ref, *rest, ffn, tf, width, final, mixed):
    rest = list(rest)
    mix_ref, wo_ref = (rest.pop(0), rest.pop(0)) if mixed else (None, None)
    fn_ref = rest.pop(0) if final else None
    out_ref, gbuf, ubuf, tail_ref, act_ref = rest
    tm = x_ref.shape[0]

    @pl.when(pl.program_id(1) == 0)
    def _():
        tail_ref[...] = jnp.zeros(tail_ref.shape, F32)

    x = x_ref[...]
    if mixed:
        x = x + jnp.dot(mix_ref[...], wo_ref[...], preferred_element_type=F32)
    hn = _rmsnorm(x, g_ref[...]).astype(BF16)
    for c in range(ffn // tf):
        lo = c * tf
        hi = ffn + lo
        gate = jnp.dot(hn, wup_ref[:, lo:lo + tf], preferred_element_type=F32)
        up = jnp.dot(hn, wup_ref[:, hi:hi + tf], preferred_element_type=F32)
        gate = _causal_conv_rows(gbuf.at[c % 2], tail_ref.at[:, lo:lo + tf], gate,
                                 cw_ref[:, lo:lo + tf], width, tm)
        up = _causal_conv_rows(ubuf.at[c % 2], tail_ref.at[:, hi:hi + tf], up,
                               cw_ref[:, hi:hi + tf], width, tm)
        act_ref[:, lo:lo + tf] = (_silu(gate) * up).astype(BF16)
    res = x + jnp.dot(act_ref[...], wdn_ref[...], preferred_element_type=F32)
    if final:
        res = _rmsnorm(res, fn_ref[...])
    out_ref[...] = res


def conv_ffn_layer(x, g, w_up, conv_w, w_down, final_g, tm, tf, mix=None):
    b, s, d = x.shape
    ffn = w_down.shape[0]
    width = conv_w.shape[0]
    final = final_g is not None
    row = pl.BlockSpec((None, tm, d), lambda bi, i: (bi, i, 0))
    in_specs = [row, _resident((1, d)), _resident(w_up.shape), _resident(conv_w.shape),
                _resident(w_down.shape)]
    args = [x, g.reshape(1, d), w_up, conv_w, w_down]
    if mix is not None:
        in_specs += [row, _resident(mix[1].shape)]
        args += list(mix)
    if final:
        in_specs.append(_resident((1, d)))
        args.append(final_g.reshape(1, d))
    return pl.pallas_call(
        functools.partial(_ffn_kernel, ffn=ffn, tf=tf, width=width, final=final,
                          mixed=mix is not None),
        grid=(b, s // tm),
        in_specs=in_specs,
        out_specs=row,
        out_shape=jax.ShapeDtypeStruct((b, s, d), F32),
        scratch_shapes=[pltpu.VMEM((2, tm + SUBLANES, tf), F32),
                        pltpu.VMEM((2, tm + SUBLANES, tf), F32),
                        pltpu.VMEM((SUBLANES, 2 * ffn), F32),
                        pltpu.VMEM((tm, ffn), BF16)],
        compiler_params=_params("parallel", "arbitrary"),
        name="conv_ffn",
    )(*args)


def _sconv_kernel(x_ref, g_ref, win_ref, cw_ref, wout_ref, out_ref, cbuf, tail_ref, *, width):
    tm, d = x_ref.shape

    @pl.when(pl.program_id(1) == 0)
    def _():
        tail_ref[...] = jnp.zeros(tail_ref.shape, F32)

    x = x_ref[...]
    hn = _rmsnorm(x, g_ref[...]).astype(BF16)
    b_gate = jnp.dot(hn, win_ref[:, 0:d], preferred_element_type=F32)
    c_gate = jnp.dot(hn, win_ref[:, d:2 * d], preferred_element_type=F32)
    xv = jnp.dot(hn, win_ref[:, 2 * d:3 * d], preferred_element_type=F32)
    y = b_gate * _causal_conv_rows(cbuf, tail_ref, c_gate * xv, cw_ref[...], width, tm)
    out_ref[...] = x + jnp.dot(y.astype(BF16), wout_ref[...], preferred_element_type=F32)


def short_conv_layer(x, g, w_in, conv_w, w_out, tm):
    b, s, d = x.shape
    row = pl.BlockSpec((None, tm, d), lambda bi, i: (bi, i, 0))
    return pl.pallas_call(
        functools.partial(_sconv_kernel, width=conv_w.shape[0]),
        grid=(b, s // tm),
        in_specs=[row, _resident((1, d)), _resident(w_in.shape), _resident(conv_w.shape),
                  _resident(w_out.shape)],
        out_specs=row,
        out_shape=jax.ShapeDtypeStruct((b, s, d), F32),
        scratch_shapes=[pltpu.VMEM((tm + SUBLANES, d), F32), pltpu.VMEM((SUBLANES, d), F32)],
        compiler_params=_params("parallel", "arbitrary"),
        name="short_conv",
    )(x, g.reshape(1, d), w_in, conv_w, w_out)


def _moba_proj_kernel(x_ref, g_ref, w_ref, q_ref, k_ref, vt_ref, km_ref,
                      *, blk, q_scale, pad_tiles):
    step = pl.program_id(1)

    @pl.when(step < pad_tiles)
    def _():
        k_ref[...] = jnp.zeros(k_ref.shape, BF16)
        vt_ref[...] = jnp.zeros(vt_ref.shape, BF16)

    @pl.when(step >= pad_tiles)
    def _():
        tm, d = x_ref.shape
        n_heads, rows, _ = vt_ref.shape
        hd = d // n_heads
        hn = _rmsnorm(x_ref[...], g_ref[...]).astype(BF16)
        y = jnp.dot(hn, w_ref[...], preferred_element_type=F32)
        q_ref[...] = (y[:, 0:d] * q_scale).astype(BF16)
        k_ref[...] = y[:, d:2 * d].astype(BF16)
        vt_ref[:, 0:hd, :] = y[:, 2 * d:3 * d].T.reshape(n_heads, hd, tm).astype(BF16)
        vt_ref[:, hd:rows, :] = jnp.ones((n_heads, rows - hd, tm), BF16)
        km = jnp.mean(y[:, d:2 * d].reshape(tm // blk, blk, d), axis=1)
        km_ref[...] = km[:, None, :]


def moba_proj(x, g, w, n_heads, tm):
    b, s, d = x.shape
    blk = MOBA_BLOCK
    hd = d // n_heads
    rows = hd + BF16_SUBLANES
    pad = MOBA_GROUP * blk
    pad_tiles = pad // tm
    assert pad_tiles * tm == pad

    def real(i):
        return jnp.maximum(i - pad_tiles, 0)

    return pl.pallas_call(
        functools.partial(_moba_proj_kernel, blk=blk, q_scale=hd ** -0.5 * LOG2E,
                          pad_tiles=pad_tiles),
        grid=(b, s // tm + pad_tiles),
        in_specs=[pl.BlockSpec((None, tm, d), lambda bi, i: (bi, real(i), 0)),
                  _resident((1, d)), _resident(w.shape)],
        out_specs=[pl.BlockSpec((None, tm, d), lambda bi, i: (bi, real(i), 0)),
                   pl.BlockSpec((None, tm, d), lambda bi, i: (bi, i, 0)),
                   pl.BlockSpec((None, n_heads, rows, tm), lambda bi, i: (bi, 0, 0, i)),
                   pl.BlockSpec((None, tm // blk, 1, d), lambda bi, i: (bi, real(i), 0, 0))],
        out_shape=[jax.ShapeDtypeStruct((b, s, d), BF16),
                   jax.ShapeDtypeStruct((b, s + pad, d), BF16),
                   jax.ShapeDtypeStruct((b, n_heads, rows, s + pad), BF16),
                   jax.ShapeDtypeStruct((b, s // blk, 1, d), F32)],
        compiler_params=_params("parallel", "arbitrary"),
        name="moba_proj",
    )(x, g.reshape(1, d), w)


def _split3_bf16(x):
    hi = x.astype(BF16)
    r = x - hi.astype(F32)
    mid = r.astype(BF16)
    return hi, mid, (r - mid.astype(F32)).astype(BF16)


def _moba_kernel(slopes_ref, q_ref, k_ref, vt_ref, km_ref, kaug_ref, o_ref, m_ref, acc_ref,
                 u0_ref, u1_ref, top0_ref, top1_ref, *, blk, n_top, group):
    qi = pl.program_id(2)
    tq = q_ref.shape[0]
    n_heads = vt_ref.shape[0]
    hd = q_ref.shape[1] // n_heads
    n_blk = km_ref.shape[0]
    nt = (((1,), (1,)), ((), ()))
    n_groups = (qi + group) // group
    u_refs = (u0_ref, u1_ref)
    top_refs = (top0_ref, top1_ref)
    heads = range(n_heads)
    cols = [slice(a * hd, (a + 1) * hd) for a in heads]
    slope2 = [slopes_ref[pl.program_id(1) * n_heads + a] * LOG2E for a in heads]
    qs = [q_ref[:, cols[a]] for a in heads]

    kk = lax.broadcasted_iota(jnp.int32, (blk, tq), 0)
    qq = lax.broadcasted_iota(jnp.int32, (blk, tq), 1)
    causal = kk <= qq

    lane = lax.broadcasted_iota(jnp.int32, (tq, LANES), 1)
    q_pos = lax.broadcasted_iota(jnp.int32, (tq, LANES), 0).astype(F32)
    q_aug = []
    for a in heads:
        parts = (_split3_bf16(jnp.full((tq, LANES), slope2[a], F32))
                 + _split3_bf16(-q_pos * slope2[a]))
        extra = jnp.zeros((tq, LANES), F32)
        for e, part in enumerate(parts):
            extra = jnp.where(lane == e, part.astype(F32), extra)
        q_aug.append(jnp.concatenate([qs[a], extra.astype(BF16)], axis=1))

    def window_start(g):
        return pl.multiple_of((qi - g * group + 1) * blk, blk)

    def score_group(g, slot):
        keys = pl.ds(window_start(g), group * blk)
        for a in heads:
            lhs = jnp.concatenate([k_ref[keys, cols[a]], kaug_ref[...]], axis=1)
            s = lax.dot_general(lhs, q_aug[a], nt, preferred_element_type=F32)
            for t in range(group):
                u = s[t * blk:(t + 1) * blk]
                if t == group - 1:
                    u = jnp.where(causal | (g != 0), u, NEG_INF)
                u_refs[slot][a, t] = u
                top_refs[slot][a, t] = jnp.max(u, axis=0, keepdims=True)

    score_group(0, 0)
    for a in heads:
        m_ref[a] = jnp.full((1, tq), NEG_INF, F32)
        acc_ref[a] = jnp.zeros(acc_ref.shape[1:], F32)

    blk_ids = lax.broadcasted_iota(jnp.int32, (n_blk, tq), 0)
    sels = []
    for a in heads:
        gate = lax.dot_general(km_ref[:, cols[a]].astype(BF16), qs[a], nt,
                               preferred_element_type=F32)
        gate = jnp.where(blk_ids < qi, gate, NEG_INF)
        picks = []
        for _ in range(n_top):
            best = jnp.max(gate, axis=0, keepdims=True)
            idx = jnp.min(jnp.where(gate == best, blk_ids, n_blk), axis=0, keepdims=True)
            picks.append(jnp.where(idx < qi, idx, n_blk))
            gate = jnp.where(blk_ids == idx, -jnp.inf, gate)
        sels.append(picks)

    def consume_group(g, slot):
        keys = pl.ds(window_start(g), group * blk)
        for a in heads:
            m_old = m_ref[a]
            m_new = m_old
            hits, offs = [], []
            for t in range(group):
                j = qi - g * group - (group - 1) + t
                hit = (sels[a][0] == j) | (j == qi)
                for sel in sels[a][1:]:
                    hit = hit | (sel == j)
                off = -slope2[a] * ((qi - j) * blk).astype(F32)
                m_new = jnp.maximum(m_new,
                                    jnp.where(hit, top_refs[slot][a, t] + off, NEG_INF))
                hits.append(hit); offs.append(off)
            ps = []
            for t in range(group):
                shift = jnp.where(hits[t], m_new - offs[t], -NEG_INF)
                ps.append(jnp.exp2((u_refs[slot][a, t] - shift).astype(BF16)))
            acc_ref[a] = (jnp.exp2(m_old - m_new) * acc_ref[a]
                          + jnp.dot(vt_ref[a, :, keys], jnp.concatenate(ps, axis=0),
                                    preferred_element_type=F32))
            m_ref[a] = m_new

    def group_body(g, carry):
        for slot in range(2):
            @pl.when(g % 2 == slot)
            def _():
                score_group(g + 1, 1 - slot)
                consume_group(g, slot)
        return carry

    lax.fori_loop(0, n_groups - 1, group_body, 0)
    for slot in range(2):
        @pl.when((n_groups - 1) % 2 == slot)
        def _():
            consume_group(n_groups - 1, slot)

    for a in heads:
        acc = acc_ref[a]
        o_ref[:, cols[a]] = (acc[0:hd] / acc[hd:hd + 1]).T.astype(o_ref.dtype)


def moba_attention(q, k, vt, k_mean, n_heads):
    b, s, d = q.shape
    hd = d // n_heads
    rows = vt.shape[2]
    blk = MOBA_BLOCK
    s_pad = k.shape[1]
    assert s_pad == s + MOBA_GROUP * blk
    n_blk = s // blk
    n_top = min(MOBA_TOPK, n_blk)
    slopes = jnp.exp2(-8.0 * jnp.arange(1, n_heads + 1, dtype=F32) / n_heads)
    key_pos = (jnp.arange(MOBA_GROUP * blk, dtype=jnp.int32) % blk).astype(F32)[:, None]
    lane = jnp.arange(LANES, dtype=jnp.int32)[None, :]
    key_aug = jnp.where(lane < 3, key_pos, jnp.where(lane < 6, 1.0, 0.0)).astype(BF16)
    hps = MOBA_HEADS_PER_STEP
    wide = hps * hd
    once = pl.Buffered(1)
    grid_spec = pltpu.PrefetchScalarGridSpec(
        num_scalar_prefetch=1,
        grid=(b, n_heads // hps, n_blk),
        in_specs=[
            pl.BlockSpec((None, blk, wide), lambda bi, h, i, *_: (bi, i, h)),
            pl.BlockSpec((None, s_pad, wide), lambda bi, h, i, *_: (bi, 0, h),
                         pipeline_mode=once),
            pl.BlockSpec((None, hps, rows, s_pad), lambda bi, h, i, *_: (bi, h, 0, 0),
                         pipeline_mode=once),
            pl.BlockSpec((None, n_blk, wide), lambda bi, h, i, *_: (bi, 0, h)),
            pl.BlockSpec(key_aug.shape, lambda bi, h, i, *_: (0, 0), pipeline_mode=once),
        ],
        out_specs=pl.BlockSpec((None, blk, wide), lambda bi, h, i, *_: (bi, i, h)),
        scratch_shapes=[pltpu.VMEM((hps, 1, blk), F32),
                        pltpu.VMEM((hps, rows, blk), F32),
                        pltpu.VMEM((hps, MOBA_GROUP, blk, blk), F32),
                        pltpu.VMEM((hps, MOBA_GROUP, blk, blk), F32),
                        pltpu.VMEM((hps, MOBA_GROUP, 1, blk), F32),
                        pltpu.VMEM((hps, MOBA_GROUP, 1, blk), F32)],
    )
    return pl.pallas_call(
        functools.partial(_moba_kernel, blk=blk, n_top=n_top, group=MOBA_GROUP),
        grid_spec=grid_spec,
        out_shape=jax.ShapeDtypeStruct((b, s, d), BF16),
        compiler_params=_params("parallel", "parallel", "arbitrary"),
        name="moba_attention",
    )(slopes, q, k, vt, k_mean, key_aug)


def _dot_hi(a, b):
    return jnp.dot(a, b, preferred_element_type=F32, precision=lax.Precision.HIGHEST)


def _split_bf16(a):
    hi = a.astype(BF16)
    return hi, (a - hi.astype(F32)).astype(BF16)


def _dot_split(a, b):
    a_hi, a_lo = _split_bf16(a)
    b_hi, b_lo = _split_bf16(b)
    return (jnp.dot(a_hi, b_hi, preferred_element_type=F32)
            + jnp.dot(a_hi, b_lo, preferred_element_type=F32)
            + jnp.dot(a_lo, b_hi, preferred_element_type=F32))


def _gdn_gates_kernel(sm_ref, alog_ref, dt_ref, o_ref, *, cs, n_heads):
    small = sm_ref[...]
    tm = small.shape[0]
    sp = small + dt_ref[...]
    softplus = jnp.maximum(sp, 0.0) + jnp.log1p(jnp.exp(-jnp.abs(sp)))
    g = -jnp.exp(alog_ref[...]) * softplus
    ii = lax.broadcasted_iota(jnp.int32, (tm, tm), 0)
    jj = lax.broadcasted_iota(jnp.int32, (tm, tm), 1)
    same_chunk_prefix = ((ii // cs) == (jj // cs)) & (ii >= jj)
    gc = _dot_hi(same_chunk_prefix.astype(F32), g)
    lane = lax.broadcasted_iota(jnp.int32, small.shape, 1)
    o_ref[...] = jnp.where(lane < n_heads, jax.nn.sigmoid(small), gc)


def gdn_gates(raw, a_log, dt_bias, n_heads, tm):
    t = raw.shape[0]
    pad = (n_heads, LANES - 2 * n_heads)
    alog_vec = jnp.pad(a_log.astype(F32), pad).reshape(1, LANES)
    dt_vec = jnp.pad(dt_bias.astype(F32), pad).reshape(1, LANES)
    return pl.pallas_call(
        functools.partial(_gdn_gates_kernel, cs=GDN_CHUNK, n_heads=n_heads),
        grid=(t // tm,),
        in_specs=[pl.BlockSpec((tm, LANES), lambda i: (i, 0)),
                  _resident((1, LANES)), _resident((1, LANES))],
        out_specs=pl.BlockSpec((tm, LANES), lambda i: (i, 0)),
        out_shape=jax.ShapeDtypeStruct((t, LANES), F32),
        compiler_params=_params("parallel"),
        name="gdn_gates",
    )(raw, alog_vec, dt_vec)


def _gdn_kernel(q_ref, k_ref, v_ref, z_ref, gate_ref, gcr_ref, cwq_ref, cwk_ref, cwv_ref,
                nw_ref, o_ref, qbuf, kbuf, vbuf, tail_ref, state_ref, *, cs, width, n_heads):
    ts = q_ref.shape[0]
    hps = state_ref.shape[0]
    dh = q_ref.shape[1] // hps
    nc = ts // cs
    heads = range(hps)
    cols = [slice(a * dh, (a + 1) * dh) for a in heads]
    chunks = [slice(c * cs, (c + 1) * cs) for c in range(nc)]
    pairs = [(a, c) for a in heads for c in range(nc)]
    head0 = pl.program_id(1) * hps

    @pl.when(pl.program_id(2) == 0)
    def _():
        tail_ref[...] = jnp.zeros(tail_ref.shape, F32)
        state_ref[...] = jnp.zeros(state_ref.shape, F32)

    def conv_silu(buf, n, ref, cw_ref):
        return _silu(_causal_conv_rows(buf, tail_ref.at[n], ref[...].astype(F32), cw_ref[...],
                                       width, ts))

    q_all = conv_silu(qbuf, 0, q_ref, cwq_ref)
    k_all = conv_silu(kbuf, 1, k_ref, cwk_ref)
    v_all = conv_silu(vbuf, 2, v_ref, cwv_ref)

    gates = gate_ref[...]
    lane = lax.broadcasted_iota(jnp.int32, gates.shape, 1)

    ii = lax.broadcasted_iota(jnp.int32, (cs, cs), 0)
    jj = lax.broadcasted_iota(jnp.int32, (cs, cs), 1)
    incl = ii >= jj
    strict = ii > jj
    eye = (ii == jj).astype(F32)
    nt = (((1,), (1,)), ((), ()))

    ks, gcs, k16s, kbq16s, vks, q_decs = [], [], [], [], [], []
    for a in heads:
        q = q_all[:, cols[a]]
        k = k_all[:, cols[a]]
        q = q * lax.rsqrt(jnp.sum(q * q, axis=-1, keepdims=True) + NORM_EPS) * (dh ** -0.5)
        k = k * lax.rsqrt(jnp.sum(k * k, axis=-1, keepdims=True) + NORM_EPS)
        beta = jnp.sum(jnp.where(lane == head0 + a, gates, 0.0), axis=1, keepdims=True)
        gc = jnp.sum(jnp.where(lane == n_heads + head0 + a, gates, 0.0), axis=1,
                     keepdims=True)
        e_gc = jnp.exp(gc)
        k_beta = k * beta
        ks.append(k); gcs.append(gc)
        vks.append(jnp.concatenate([(v_all[:, cols[a]] * beta).astype(BF16),
                                    (k_beta * e_gc).astype(BF16)], axis=1))
        q_decs.append((q * e_gc).astype(BF16))
        kbq16s.append((k_beta.astype(BF16), q.astype(BF16)))
        k16s.append(k.astype(BF16))

    negs, attns = {}, {}
    for a, c in pairs:
        sl = chunks[c]
        diff = gcs[a][sl] - gcr_ref[a, c:c + 1, :]
        decay = jnp.where(incl, jnp.exp(jnp.where(incl, diff, 0.0)), 0.0)
        k_beta16, q16 = kbq16s[a]
        both = lax.dot_general(jnp.concatenate([k_beta16[sl], q16[sl]], axis=0), k16s[a][sl],
                               nt, preferred_element_type=F32)
        negs[a, c] = jnp.where(strict, -(both[0:cs] * decay), 0.0)
        attns[a, c] = (both[cs:2 * cs] * decay).astype(BF16)
    powers = {p: _dot_split(negs[p], negs[p]) for p in pairs}
    t_mats = {p: eye + negs[p] for p in pairs}
    span = 4
    while span <= cs:
        last = span == cs
        prods = {p: _dot_split(t_mats[p] if last else
                               jnp.concatenate([t_mats[p], powers[p]], axis=0), powers[p])
                 for p in pairs}
        t_mats = {p: t_mats[p] + prods[p][0:cs] for p in pairs}
        if not last:
            powers = {p: prods[p][cs:2 * cs] for p in pairs}
        span *= 2

    us, uws, kus, stacks, sds = {}, {}, {}, {}, {}
    for a, c in pairs:
        uws[a, c] = jnp.dot(t_mats[a, c].astype(BF16), vks[a][chunks[c]],
                            preferred_element_type=F32)
    for a, c in pairs:
        sl = chunks[c]
        gc = gcs[a][sl]
        g_last = gc[cs - 1:cs, :]
        kd_t = (ks[a][sl] * jnp.exp(g_last - gc)).T.astype(BF16)
        uw16 = uws[a, c].astype(BF16)
        kuw = jnp.dot(kd_t, uw16, preferred_element_type=F32)
        us[a, c] = uws[a, c][:, 0:dh]
        kus[a, c] = kuw[:, 0:dh]
        stacks[a, c] = jnp.concatenate([kuw[:, dh:2 * dh].astype(BF16), uw16[:, dh:2 * dh],
                                        q_decs[a][sl]], axis=0)
        sds[a, c] = jnp.exp(g_last)

    def emit_output(c, a, inter, v_new):
        sl = chunks[c]
        o = inter + jnp.dot(attns[a, c], v_new.astype(BF16), preferred_element_type=F32)
        o = o * lax.rsqrt(jnp.mean(o * o, axis=-1, keepdims=True) + NORM_EPS) * nw_ref[...]
        o_ref[sl, cols[a]] = (o * _silu(z_ref[sl, cols[a]].astype(F32))).astype(o_ref.dtype)

    states = [state_ref[a] for a in heads]
    pending = None
    for c in range(nc):
        prods = [jnp.dot(stacks[a, c], states[a].astype(BF16), preferred_element_type=F32)
                 for a in heads]
        if pending is not None:
            for a in heads:
                emit_output(c - 1, a, *pending[a])
        pending = []
        for a in heads:
            prod = prods[a]
            states[a] = states[a] * sds[a, c] - prod[0:dh] + kus[a, c]
            pending.append((prod[dh + cs:dh + 2 * cs], us[a, c] - prod[dh:dh + cs]))
    for a in heads:
        emit_output(nc - 1, a, *pending[a])
        state_ref[a] = states[a]


def gdn_core(proj, gates, conv_w, norm_w, n_heads, ts):
    b, s, _ = proj.shape
    dh = norm_w.shape[0]
    d = n_heads * dh
    width = conv_w.shape[0]
    cs = GDN_CHUNK
    gc_rows = gates[:, :, n_heads:2 * n_heads].reshape(b, s // cs, cs, n_heads)
    gc_rows = gc_rows.transpose(0, 3, 1, 2)

    hps = GDN_HEADS_PER_STEP
    wide = hps * dh
    groups = n_heads // hps

    def col(section):
        return pl.BlockSpec((None, ts, wide), lambda bi, h, i: (bi, i, section * groups + h))

    def cw(section):
        return pl.BlockSpec((width, wide), lambda bi, h, i: (0, section * groups + h))

    return pl.pallas_call(
        functools.partial(_gdn_kernel, cs=GDN_CHUNK, width=width, n_heads=n_heads),
        grid=(b, groups, s // ts),
        in_specs=[col(0), col(1), col(2), col(3),
                  pl.BlockSpec((None, ts, LANES), lambda bi, h, i: (bi, i, 0)),
                  pl.BlockSpec((None, hps, ts // cs, cs), lambda bi, h, i: (bi, h, i, 0)),
                  cw(0), cw(1), cw(2), _resident((1, dh))],
        out_specs=pl.BlockSpec((None, ts, wide), lambda bi, h, i: (bi, i, h)),
        out_shape=jax.ShapeDtypeStruct((b, s, d), BF16),
        scratch_shapes=[pltpu.VMEM((ts + SUBLANES, wide), F32)] * 3
                       + [pltpu.VMEM((3, SUBLANES, wide), F32),
                          pltpu.VMEM((hps, dh, dh), F32)],
        compiler_params=_params("parallel", "parallel", "arbitrary"),
        name="gdn_core",
    )(proj, proj, proj, proj, gates, gc_rows, conv_w, conv_w, conv_w,
      norm_w.astype(F32).reshape(1, dh))


def _row_tile(n, target):
    t = min(n, target)
    while n % t:
        t //= 2
    return t


def kernel(x, mix_norm, ffn_norm, final_norm, moba_w_qkv, moba_w_o, sconv_w_in, sconv_conv,
           sconv_w_out, gdn_w_in, gdn_conv, gdn_a_log, gdn_dt_bias, gdn_norm, gdn_w_o,
           ffn_w_up, ffn_conv, ffn_w_down):
    bsz, seq, d = x.shape
    depth = mix_norm.shape[0]
    t = bsz * seq
    tm = _row_tile(seq, ROW_TILE)
    ffn_tm = _row_tile(seq, FFN_ROW_TILE)
    for i in range(depth):
        kind, j = i % N_MIXERS, i // N_MIXERS
        mix = None
        if kind == 0:
            q, k, vt, k_mean = moba_proj(x, mix_norm[i], moba_w_qkv[j].astype(BF16),
                                         MOBA_HEADS, tm)
            o = moba_attention(q, k, vt, k_mean.reshape(bsz, seq // MOBA_BLOCK, d),
                               MOBA_HEADS)
            mix = (o, moba_w_o[j].astype(BF16))
        elif kind == 1:
            x = short_conv_layer(x, mix_norm[i], sconv_w_in[j].astype(BF16), sconv_conv[j],
                                 sconv_w_out[j].astype(BF16), tm)
        else:
            n_in = gdn_w_in.shape[-1]
            w_in = jnp.pad(gdn_w_in[j], ((0, 0), (0, 4 * d + LANES - n_in))).astype(BF16)
            proj, raw_gates = gdn_proj(x.reshape(t, d), mix_norm[i], w_in, tm)
            gates = gdn_gates(raw_gates, gdn_a_log[j], gdn_dt_bias[j], GDN_HEADS, tm)
            o = gdn_core(proj.reshape(bsz, seq, 4 * d), gates.reshape(bsz, seq, LANES),
                         gdn_conv[j], gdn_norm[j], GDN_HEADS, _row_tile(seq, GDN_ROW_TILE))
            mix = (o, gdn_w_o[j].astype(BF16))
        x = conv_ffn_layer(x, ffn_norm[i], ffn_w_up[i].astype(BF16), ffn_conv[i],
                           ffn_w_down[i].astype(BF16),
                           final_norm if i == depth - 1 else None, ffn_tm, FFN_COL_TILE, mix)
    return x
```

```python
import functools

import jax
import jax.numpy as jnp
from jax import lax
from jax.experimental import pallas as pl
from jax.experimental.pallas import tpu as pltpu

F32 = jnp.float32
BF16 = jnp.bfloat16

N_MIXERS = 3
MOBA_HEADS = 8
MOBA_BLOCK = 256
MOBA_TOPK = 3
GDN_HEADS = 8
GDN_CHUNK = 64
NORM_EPS = 1e-6
NEG_INF = -1e30
LOG2E = 1.4426950408889634
MOBA_GROUP = 4
MOBA_HEADS_PER_STEP = 4
GDN_HEADS_PER_STEP = 4
ROW_TILE = 512
FFN_ROW_TILE = 512
FFN_COL_TILE = 256
GDN_ROW_TILE = 512

LANES = 128
SUBLANES = 8
BF16_SUBLANES = 16
VMEM_LIMIT = 56 * 1024 * 1024


def _params(*sem):
    return pltpu.CompilerParams(dimension_semantics=sem, vmem_limit_bytes=VMEM_LIMIT)


def _rmsnorm(x, g):
    return x * lax.rsqrt(jnp.mean(x * x, axis=-1, keepdims=True) + NORM_EPS) * g


def _silu(x):
    return x * jax.nn.sigmoid(x)


def _resident(shape):
    return pl.BlockSpec(shape, lambda *_: (0,) * len(shape), pipeline_mode=pl.Buffered(1))


def _gdn_proj_kernel(x_ref, g_ref, w_ref, main_ref, gate_ref):
    hn = _rmsnorm(x_ref[...], g_ref[...]).astype(BF16)
    y = jnp.dot(hn, w_ref[...], preferred_element_type=F32)
    n_main = main_ref.shape[1]
    main_ref[...] = y[:, 0:n_main].astype(main_ref.dtype)
    gate_ref[...] = y[:, n_main:]


def gdn_proj(x2, g, w, tm):
    t, d = x2.shape
    n = w.shape[1]
    n_main = n - LANES
    return pl.pallas_call(
        _gdn_proj_kernel,
        grid=(t // tm,),
        in_specs=[pl.BlockSpec((tm, d), lambda i: (i, 0)), _resident((1, d)),
                  _resident((d, n))],
        out_specs=[pl.BlockSpec((tm, n_main), lambda i: (i, 0)),
                   pl.BlockSpec((tm, LANES), lambda i: (i, 0))],
        out_shape=[jax.ShapeDtypeStruct((t, n_main), BF16),
                   jax.ShapeDtypeStruct((t, LANES), F32)],
        compiler_params=_params("parallel"),
        name="gdn_proj",
    )(x2, g.reshape(1, d), w)


def _causal_conv_rows(buf_ref, tail_ref, cur, cw, width, tm):
    buf_ref[0:SUBLANES, :] = tail_ref[...]
    buf_ref[SUBLANES:SUBLANES + tm, :] = cur
    out = cw[width - 1:width, :] * cur
    for k in range(width - 1):
        off = SUBLANES - (width - 1) + k
        out = out + cw[k:k + 1, :] * buf_ref[off:off + tm, :]
    tail_ref[...] = cur[tm - SUBLANES:tm, :]
    return out


def _ffn_kernel(x_ref, g_ref, wup_ref, cw_ref, wdn_ref, *rest, ffn, tf, width, final, mixed):
    rest = list(rest)
    mix_ref, wo_ref = (rest.pop(0), rest.pop(0)) if mixed else (None, None)
    fn_ref = rest.pop(0) if final else None
    out_ref, gbuf, ubuf, tail_ref, act_ref = rest
    tm = x_ref.shape[0]

    @pl.when(pl.program_id(1) == 0)
    def _():
        tail_ref[...] = jnp.zeros(tail_ref.shape, F32)

    x = x_ref[...]
    if mixed:
        x = x + jnp.dot(mix_ref[...], wo_ref[...], preferred_element_type=F32)
    hn = _rmsnorm(x, g_ref[...]).astype(BF16)
    for c in range(ffn // tf):
        lo = c * tf
        hi = ffn + lo
        gate = jnp.dot(hn, wup_ref[:, lo:lo + tf], preferred_element_type=F32)
        up = jnp.dot(hn, wup_ref[:, hi:hi + tf], preferred_element_type=F32)
        gate = _causal_conv_rows(gbuf.at[c % 2], tail_ref.at[:, lo:lo + tf], gate,
                                 cw_ref[:, lo:lo + tf], width, tm)
        up = _causal_conv_rows(ubuf.at[c % 2], tail_ref.at[:, hi:hi + tf], up,
                               cw_ref[:, hi:hi + tf], width, tm)
        act_ref[:, lo:lo + tf] = (_silu(gate) * up).astype(BF16)
    res = x + jnp.dot(act_ref[...], wdn_ref[...], preferred_element_type=F32)
    if final:
        res = _rmsnorm(res, fn_ref[...])
    out_ref[...] = res


def conv_ffn_layer(x, g, w_up, conv_w, w_down, final_g, tm, tf, mix=None):
    b, s, d = x.shape
    ffn = w_down.shape[0]
    width = conv_w.shape[0]
    final = final_g is not None
    row = pl.BlockSpec((None, tm, d), lambda bi, i: (bi, i, 0))
    in_specs = [row, _resident((1, d)), _resident(w_up.shape), _resident(conv_w.shape),
                _resident(w_down.shape)]
    args = [x, g.reshape(1, d), w_up, conv_w, w_down]
    if mix is not None:
        in_specs += [row, _resident(mix[1].shape)]
        args += list(mix)
    if final:
        in_specs.append(_resident((1, d)))
        args.append(final_g.reshape(1, d))
    return pl.pallas_call(
        functools.partial(_ffn_kernel, ffn=ffn, tf=tf, width=width, final=final,
                          mixed=mix is not None),
        grid=(b, s // tm),
        in_specs=in_specs,
        out_specs=row,
        out_shape=jax.ShapeDtypeStruct((b, s, d), F32),
        scratch_shapes=[pltpu.VMEM((2, tm + SUBLANES, tf), F32),
                        pltpu.VMEM((2, tm + SUBLANES, tf), F32),
                        pltpu.VMEM((SUBLANES, 2 * ffn), F32),
                        pltpu.VMEM((tm, ffn), BF16)],
        compiler_params=_params("parallel", "arbitrary"),
        name="conv_ffn",
    )(*args)


def _sconv_kernel(x_ref, g_ref, win_ref, cw_ref, wout_ref, out_ref, cbuf, tail_ref, *, width):
    tm, d = x_ref.shape

    @pl.when(pl.program_id(1) == 0)
    def _():
        tail_ref[...] = jnp.zeros(tail_ref.shape, F32)

    x = x_ref[...]
    hn = _rmsnorm(x, g_ref[...]).astype(BF16)
    b_gate = jnp.dot(hn, win_ref[:, 0:d], preferred_element_type=F32)
    c_gate = jnp.dot(hn, win_ref[:, d:2 * d], preferred_element_type=F32)
    xv = jnp.dot(hn, win_ref[:, 2 * d:3 * d], preferred_element_type=F32)
    y = b_gate * _causal_conv_rows(cbuf, tail_ref, c_gate * xv, cw_ref[...], width, tm)
    out_ref[...] = x + jnp.dot(y.astype(BF16), wout_ref[...], preferred_element_type=F32)


def short_conv_layer(x, g, w_in, conv_w, w_out, tm):
    b, s, d = x.shape
    row = pl.BlockSpec((None, tm, d), lambda bi, i: (bi, i, 0))
    return pl.pallas_call(
        functools.partial(_sconv_kernel, width=conv_w.shape[0]),
        grid=(b, s // tm),
        in_specs=[row, _resident((1, d)), _resident(w_in.shape), _resident(conv_w.shape),
                  _resident(w_out.shape)],
        out_specs=row,
        out_shape=jax.ShapeDtypeStruct((b, s, d), F32),
        scratch_shapes=[pltpu.VMEM((tm + SUBLANES, d), F32), pltpu.VMEM((SUBLANES, d), F32)],
        compiler_params=_params("parallel", "arbitrary"),
        name="short_conv",
    )(x, g.reshape(1, d), w_in, conv_w, w_out)


def _moba_proj_kernel(x_ref, g_ref, w_ref, q_ref, k_ref, vt_ref, km_ref,
                      *, blk, q_scale, pad_tiles):
    step = pl.program_id(1)

    @pl.when(step < pad_tiles)
    def _():
        k_ref[...] = jnp.zeros(k_ref.shape, BF16)
        vt_ref[...] = jnp.zeros(vt_ref.shape, BF16)

    @pl.when(step >= pad_tiles)
    def _():
        tm, d = x_ref.shape
        n_heads, rows, _ = vt_ref.shape
        hd = d // n_heads
        hn = _rmsnorm(x_ref[...], g_ref[...]).astype(BF16)
        y = jnp.dot(hn, w_ref[...], preferred_element_type=F32)
        q_ref[...] = (y[:, 0:d] * q_scale).astype(BF16)
        k_ref[...] = y[:, d:2 * d].astype(BF16)
        vt_ref[:, 0:hd, :] = y[:, 2 * d:3 * d].T.reshape(n_heads, hd, tm).astype(BF16)
        vt_ref[:, hd:rows, :] = jnp.ones((n_heads, rows - hd, tm), BF16)
        km = jnp.mean(y[:, d:2 * d].reshape(tm // blk, blk, d), axis=1)
        km_ref[...] = km[:, None, :]


def moba_proj(x, g, w, n_heads, tm):
    b, s, d = x.shape
    blk = MOBA_BLOCK
    hd = d // n_heads
    rows = hd + BF16_SUBLANES
    pad = MOBA_GROUP * blk
    pad_tiles = pad // tm
    assert pad_tiles * tm == pad

    def real(i):
        return jnp.maximum(i - pad_tiles, 0)

    return pl.pallas_call(
        functools.partial(_moba_proj_kernel, blk=blk, q_scale=hd ** -0.5 * LOG2E,
                          pad_tiles=pad_tiles),
        grid=(b, s // tm + pad_tiles),
        in_specs=[pl.BlockSpec((None, tm, d), lambda bi, i: (bi, real(i), 0)),
                  _resident((1, d)), _resident(w.shape)],
        out_specs=[pl.BlockSpec((None, tm, d), lambda bi, i: (bi, real(i), 0)),
                   pl.BlockSpec((None, tm, d), lambda bi, i: (bi, i, 0)),
                   pl.BlockSpec((None, n_heads, rows, tm), lambda bi, i: (bi, 0, 0, i)),
                   pl.BlockSpec((None, tm // blk, 1, d), lambda bi, i: (bi, real(i), 0, 0))],
        out_shape=[jax.ShapeDtypeStruct((b, s, d), BF16),
                   jax.ShapeDtypeStruct((b, s + pad, d), BF16),
                   jax.ShapeDtypeStruct((b, n_heads, rows, s + pad), BF16),
                   jax.ShapeDtypeStruct((b, s // blk, 1, d), F32)],
        compiler_params=_params("parallel", "arbitrary"),
        name="moba_proj",
    )(x, g.reshape(1, d), w)


def _split3_bf16(x):
    hi = x.astype(BF16)
    r = x - hi.astype(F32)
    mid = r.astype(BF16)
    return hi, mid, (r - mid.astype(F32)).astype(BF16)


def _moba_kernel(slopes_ref, q_ref, k_ref, vt_ref, km_ref, kaug_ref, o_ref, m_ref, acc_ref,
                 u0_ref, u1_ref, top0_ref, top1_ref, *, blk, n_top, group):
    qi = pl.program_id(2)
    tq = q_ref.shape[0]
    n_heads = vt_ref.shape[0]
    hd = q_ref.shape[1] // n_heads
    n_blk = km_ref.shape[0]
    nt = (((1,), (1,)), ((), ()))
    n_groups = (qi + group) // group
    u_refs = (u0_ref, u1_ref)
    top_refs = (top0_ref, top1_ref)
    heads = range(n_heads)
    cols = [slice(a * hd, (a + 1) * hd) for a in heads]
    slope2 = [slopes_ref[pl.program_id(1) * n_heads + a] * LOG2E for a in heads]
    qs = [q_ref[:, cols[a]] for a in heads]

    kk = lax.broadcasted_iota(jnp.int32, (blk, tq), 0)
    qq = lax.broadcasted_iota(jnp.int32, (blk, tq), 1)
    causal = kk <= qq

    lane = lax.broadcasted_iota(jnp.int32, (tq, LANES), 1)
    q_pos = lax.broadcasted_iota(jnp.int32, (tq, LANES), 0).astype(F32)
    q_aug = []
    for a in heads:
        parts = (_split3_bf16(jnp.full((tq, LANES), slope2[a], F32))
                 + _split3_bf16(-q_pos * slope2[a]))
        extra = jnp.zeros((tq, LANES), F32)
        for e, part in enumerate(parts):
            extra = jnp.where(lane == e, part.astype(F32), extra)
        q_aug.append(jnp.concatenate([qs[a], extra.astype(BF16)], axis=1))

    def window_start(g):
        return pl.multiple_of((qi - g * group + 1) * blk, blk)

    def score_group(g, slot, first=False):
        keys = pl.ds(window_start(g), group * blk)
        for a in heads:
            lhs = jnp.concatenate([k_ref[keys, cols[a]], kaug_ref[...]], axis=1)
            s = lax.dot_general(lhs, q_aug[a], nt, preferred_element_type=F32)
            for t in range(group):
                u = s[t * blk:(t + 1) * blk]
                if first and t == group - 1:
                    u = jnp.where(causal, u, NEG_INF)
                u_refs[slot][a, t] = u
                top_refs[slot][a, t] = jnp.max(u, axis=0, keepdims=True)

    score_group(0, 0, first=True)
    for a in heads:
        m_ref[a] = jnp.full((1, tq), NEG_INF, F32)
        acc_ref[a] = jnp.zeros(acc_ref.shape[1:], F32)

    blk_ids = lax.broadcasted_iota(jnp.int32, (n_blk, tq), 0)
    sels = []
    for a in heads:
        gate = lax.dot_general(km_ref[:, cols[a]].astype(BF16), qs[a], nt,
                               preferred_element_type=F32)
        gate = jnp.where(blk_ids < qi, gate, NEG_INF)
        picks = []
        for _ in range(n_top):
            best = jnp.max(gate, axis=0, keepdims=True)
            idx = jnp.min(jnp.where(gate == best, blk_ids, n_blk), axis=0, keepdims=True)
            picks.append(jnp.where(idx < qi, idx, n_blk))
            gate = jnp.where(blk_ids == idx, -jnp.inf, gate)
        sels.append(picks)

    def consume_group(g, slot):
        keys = pl.ds(window_start(g), group * blk)
        for a in heads:
            m_old = m_ref[a]
            m_new = m_old
            hits, offs = [], []
            for t in range(group):
                j = qi - g * group - (group - 1) + t
                hit = (sels[a][0] == j) | (j == qi)
                for sel in sels[a][1:]:
                    hit = hit | (sel == j)
                off = -slope2[a] * ((qi - j) * blk).astype(F32)
                m_new = jnp.maximum(m_new,
                                    jnp.where(hit, top_refs[slot][a, t] + off, NEG_INF))
                hits.append(hit); offs.append(off)
            ps = []
            for t in range(group):
                shift = jnp.where(hits[t], m_new - offs[t], -NEG_INF)
                ps.append(jnp.exp2((u_refs[slot][a, t] - shift).astype(BF16)))
            acc_ref[a] = (jnp.exp2(m_old - m_new) * acc_ref[a]
                          + jnp.dot(vt_ref[a, :, keys], jnp.concatenate(ps, axis=0),
                                    preferred_element_type=F32))
            m_ref[a] = m_new

    def group_body(g, carry):
        for slot in range(2):
            @pl.when(g % 2 == slot)
            def _():
                score_group(g + 1, 1 - slot)
                consume_group(g, slot)
        return carry

    lax.fori_loop(0, n_groups - 1, group_body, 0)
    for slot in range(2):
        @pl.when((n_groups - 1) % 2 == slot)
        def _():
            consume_group(n_groups - 1, slot)

    for a in heads:
        acc = acc_ref[a]
        o_ref[:, cols[a]] = (acc[0:hd] / acc[hd:hd + 1]).T.astype(o_ref.dtype)


def moba_attention(q, k, vt, k_mean, n_heads):
    b, s, d = q.shape
    hd = d // n_heads
    rows = vt.shape[2]
    blk = MOBA_BLOCK
    s_pad = k.shape[1]
    assert s_pad == s + MOBA_GROUP * blk
    n_blk = s // blk
    n_top = min(MOBA_TOPK, n_blk)
    slopes = jnp.exp2(-8.0 * jnp.arange(1, n_heads + 1, dtype=F32) / n_heads)
    key_pos = (jnp.arange(MOBA_GROUP * blk, dtype=jnp.int32) % blk).astype(F32)[:, None]
    lane = jnp.arange(LANES, dtype=jnp.int32)[None, :]
    key_aug = jnp.where(lane < 3, key_pos, jnp.where(lane < 6, 1.0, 0.0)).astype(BF16)
    hps = MOBA_HEADS_PER_STEP
    wide = hps * hd
    once = pl.Buffered(1)
    grid_spec = pltpu.PrefetchScalarGridSpec(
        num_scalar_prefetch=1,
        grid=(b, n_heads // hps, n_blk),
        in_specs=[
            pl.BlockSpec((None, blk, wide), lambda bi, h, i, *_: (bi, i, h)),
            pl.BlockSpec((None, s_pad, wide), lambda bi, h, i, *_: (bi, 0, h),
                         pipeline_mode=once),
            pl.BlockSpec((None, hps, rows, s_pad), lambda bi, h, i, *_: (bi, h, 0, 0),
                         pipeline_mode=once),
            pl.BlockSpec((None, n_blk, wide), lambda bi, h, i, *_: (bi, 0, h)),
            pl.BlockSpec(key_aug.shape, lambda bi, h, i, *_: (0, 0), pipeline_mode=once),
        ],
        out_specs=pl.BlockSpec((None, blk, wide), lambda bi, h, i, *_: (bi, i, h)),
        scratch_shapes=[pltpu.VMEM((hps, 1, blk), F32),
                        pltpu.VMEM((hps, rows, blk), F32),
                        pltpu.VMEM((hps, MOBA_GROUP, blk, blk), F32),
                        pltpu.VMEM((hps, MOBA_GROUP, blk, blk), F32),
                        pltpu.VMEM((hps, MOBA_GROUP, 1, blk), F32),
                        pltpu.VMEM((hps, MOBA_GROUP, 1, blk), F32)],
    )
    return pl.pallas_call(
        functools.partial(_moba_kernel, blk=blk, n_top=n_top, group=MOBA_GROUP),
        grid_spec=grid_spec,
        out_shape=jax.ShapeDtypeStruct((b, s, d), BF16),
        compiler_params=_params("parallel", "parallel", "arbitrary"),
        name="moba_attention",
    )(slopes, q, k, vt, k_mean, key_aug)


def _dot_hi(a, b):
    return jnp.dot(a, b, preferred_element_type=F32, precision=lax.Precision.HIGHEST)


def _split_bf16(a):
    hi = a.astype(BF16)
    return hi, (a - hi.astype(F32)).astype(BF16)


def _dot_split(a, b):
    a_hi, a_lo = _split_bf16(a)
    b_hi, b_lo = _split_bf16(b)
    return jnp.dot(jnp.concatenate([a_hi, a_hi, a_lo], axis=1),
                   jnp.concatenate([b_hi, b_lo, b_hi], axis=0), preferred_element_type=F32)


def _gdn_gates_kernel(sm_ref, alog_ref, dt_ref, o_ref, *, cs, n_heads):
    small = sm_ref[...]
    tm = small.shape[0]
    sp = small + dt_ref[...]
    softplus = jnp.maximum(sp, 0.0) + jnp.log1p(jnp.exp(-jnp.abs(sp)))
    g = -jnp.exp(alog_ref[...]) * softplus
    ii = lax.broadcasted_iota(jnp.int32, (tm, tm), 0)
    jj = lax.broadcasted_iota(jnp.int32, (tm, tm), 1)
    same_chunk_prefix = ((ii // cs) == (jj // cs)) & (ii >= jj)
    gc = _dot_hi(same_chunk_prefix.astype(F32), g)
    lane = lax.broadcasted_iota(jnp.int32, small.shape, 1)
    o_ref[...] = jnp.where(lane < n_heads, jax.nn.sigmoid(small), gc)


def gdn_gates(raw, a_log, dt_bias, n_heads, tm):
    t = raw.shape[0]
    pad = (n_heads, LANES - 2 * n_heads)
    alog_vec = jnp.pad(a_log.astype(F32), pad).reshape(1, LANES)
    dt_vec = jnp.pad(dt_bias.astype(F32), pad).reshape(1, LANES)
    return pl.pallas_call(
        functools.partial(_gdn_gates_kernel, cs=GDN_CHUNK, n_heads=n_heads),
        grid=(t // tm,),
        in_specs=[pl.BlockSpec((tm, LANES), lambda i: (i, 0)),
                  _resident((1, LANES)), _resident((1, LANES))],
        out_specs=pl.BlockSpec((tm, LANES), lambda i: (i, 0)),
        out_shape=jax.ShapeDtypeStruct((t, LANES), F32),
        compiler_params=_params("parallel"),
        name="gdn_gates",
    )(raw, alog_vec, dt_vec)


def _gdn_kernel(q_ref, k_ref, v_ref, z_ref, gate_ref, gcr_ref, cwq_ref, cwk_ref, cwv_ref,
                nw_ref, o_ref, qbuf, kbuf, vbuf, tail_ref, state_ref, *, cs, width, n_heads):
    ts = q_ref.shape[0]
    hps = state_ref.shape[0]
    dh = q_ref.shape[1] // hps
    nc = ts // cs
    heads = range(hps)
    cols = [slice(a * dh, (a + 1) * dh) for a in heads]
    chunks = [slice(c * cs, (c + 1) * cs) for c in range(nc)]
    pairs = [(a, c) for a in heads for c in range(nc)]
    head0 = pl.program_id(1) * hps

    @pl.when(pl.program_id(2) == 0)
    def _():
        tail_ref[...] = jnp.zeros(tail_ref.shape, F32)
        state_ref[...] = jnp.zeros(state_ref.shape, F32)

    def conv_silu(buf, n, ref, cw_ref):
        return _silu(_causal_conv_rows(buf, tail_ref.at[n], ref[...].astype(F32), cw_ref[...],
                                       width, ts))

    q_all = conv_silu(qbuf, 0, q_ref, cwq_ref)
    k_all = conv_silu(kbuf, 1, k_ref, cwk_ref)
    v_all = conv_silu(vbuf, 2, v_ref, cwv_ref)

    gates = gate_ref[...]
    lane = lax.broadcasted_iota(jnp.int32, gates.shape, 1)

    ii = lax.broadcasted_iota(jnp.int32, (cs, cs), 0)
    jj = lax.broadcasted_iota(jnp.int32, (cs, cs), 1)
    incl = ii >= jj
    strict = ii > jj
    eye = (ii == jj).astype(F32)
    nt = (((1,), (1,)), ((), ()))

    ks, gcs, k16s, kbq16s, vks, q_decs = [], [], [], [], [], []
    for a in heads:
        q = q_all[:, cols[a]]
        k = k_all[:, cols[a]]
        q = q * lax.rsqrt(jnp.sum(q * q, axis=-1, keepdims=True) + NORM_EPS) * (dh ** -0.5)
        k = k * lax.rsqrt(jnp.sum(k * k, axis=-1, keepdims=True) + NORM_EPS)
        beta = jnp.sum(jnp.where(lane == head0 + a, gates, 0.0), axis=1, keepdims=True)
        gc = jnp.sum(jnp.where(lane == n_heads + head0 + a, gates, 0.0), axis=1,
                     keepdims=True)
        e_gc = jnp.exp(gc)
        k_beta = k * beta
        ks.append(k); gcs.append(gc)
        vks.append(jnp.concatenate([(v_all[:, cols[a]] * beta).astype(BF16),
                                    (k_beta * e_gc).astype(BF16)], axis=1))
        q_decs.append((q * e_gc).astype(BF16))
        kbq16s.append((k_beta.astype(BF16), q.astype(BF16)))
        k16s.append(k.astype(BF16))

    negs, attns = {}, {}
    for a, c in pairs:
        sl = chunks[c]
        diff = gcs[a][sl] - gcr_ref[a, c:c + 1, :]
        decay = jnp.where(incl, jnp.exp(jnp.where(incl, diff, 0.0)), 0.0)
        k_beta16, q16 = kbq16s[a]
        both = lax.dot_general(jnp.concatenate([k_beta16[sl], q16[sl]], axis=0), k16s[a][sl],
                               nt, preferred_element_type=F32)
        negs[a, c] = jnp.where(strict, -(both[0:cs] * decay), 0.0)
        attns[a, c] = (both[cs:2 * cs] * decay).astype(BF16)
    powers = {p: _dot_split(negs[p], negs[p]) for p in pairs}
    t_mats = {p: eye + negs[p] for p in pairs}
    span = 4
    while span <= cs:
        last = span == cs
        prods = {p: _dot_split(t_mats[p] if last else
                               jnp.concatenate([t_mats[p], powers[p]], axis=0), powers[p])
                 for p in pairs}
        t_mats = {p: t_mats[p] + prods[p][0:cs] for p in pairs}
        if not last:
            powers = {p: prods[p][cs:2 * cs] for p in pairs}
        span *= 2

    us, uws, kus, stacks, sds = {}, {}, {}, {}, {}
    for a, c in pairs:
        uws[a, c] = jnp.dot(t_mats[a, c].astype(BF16), vks[a][chunks[c]],
                            preferred_element_type=F32)
    for a, c in pairs:
        sl = chunks[c]
        gc = gcs[a][sl]
        g_last = gc[cs - 1:cs, :]
        kd_t = (ks[a][sl] * jnp.exp(g_last - gc)).T.astype(BF16)
        uw16 = uws[a, c].astype(BF16)
        kuw = jnp.dot(kd_t, uw16, preferred_element_type=F32)
        us[a, c] = uws[a, c][:, 0:dh]
        kus[a, c] = kuw[:, 0:dh]
        stacks[a, c] = jnp.concatenate([kuw[:, dh:2 * dh].astype(BF16), uw16[:, dh:2 * dh],
                                        q_decs[a][sl]], axis=0)
        sds[a, c] = jnp.exp(g_last)

    def emit_output(c, a, inter, v_new):
        sl = chunks[c]
        o = inter + jnp.dot(attns[a, c], v_new.astype(BF16), preferred_element_type=F32)
        o = o * lax.rsqrt(jnp.mean(o * o, axis=-1, keepdims=True) + NORM_EPS) * nw_ref[...]
        o_ref[sl, cols[a]] = (o * _silu(z_ref[sl, cols[a]].astype(F32))).astype(o_ref.dtype)

    states = [state_ref[a] for a in heads]
    pending = None
    for c in range(nc):
        prods = [jnp.dot(stacks[a, c], states[a].astype(BF16), preferred_element_type=F32)
                 for a in heads]
        if pending is not None:
            for a in heads:
                emit_output(c - 1, a, *pending[a])
        pending = []
        for a in heads:
            prod = prods[a]
            states[a] = states[a] * sds[a, c] - prod[0:dh] + kus[a, c]
            pending.append((prod[dh + cs:dh + 2 * cs], us[a, c] - prod[dh:dh + cs]))
    for a in heads:
        emit_output(nc - 1, a, *pending[a])
        state_ref[a] = states[a]


def gdn_core(proj, gates, conv_w, norm_w, n_heads, ts):
    b, s, _ = proj.shape
    dh = norm_w.shape[0]
    d = n_heads * dh
    width = conv_w.shape[0]
    cs = GDN_CHUNK
    gc_rows = gates[:, :, n_heads:2 * n_heads].reshape(b, s // cs, cs, n_heads)
    gc_rows = gc_rows.transpose(0, 3, 1, 2)

    hps = GDN_HEADS_PER_STEP
    wide = hps * dh
    groups = n_heads // hps

    def col(section):
        return pl.BlockSpec((None, ts, wide), lambda bi, h, i: (bi, i, section * groups + h))

    def cw(section):
        return pl.BlockSpec((width, wide), lambda bi, h, i: (0, section * groups + h))

    return pl.pallas_call(
        functools.partial(_gdn_kernel, cs=GDN_CHUNK, width=width, n_heads=n_heads),
        grid=(b, groups, s // ts),
        in_specs=[col(0), col(1), col(2), col(3),
                  pl.BlockSpec((None, ts, LANES), lambda bi, h, i: (bi, i, 0)),
                  pl.BlockSpec((None, hps, ts // cs, cs), lambda bi, h, i: (bi, h, i, 0)),
                  cw(0), cw(1), cw(2), _resident((1, dh))],
        out_specs=pl.BlockSpec((None, ts, wide), lambda bi, h, i: (bi, i, h)),
        out_shape=jax.ShapeDtypeStruct((b, s, d), BF16),
        scratch_shapes=[pltpu.VMEM((ts + SUBLANES, wide), F32)] * 3
                       + [pltpu.VMEM((3, SUBLANES, wide), F32),
                          pltpu.VMEM((hps, dh, dh), F32)],
        compiler_params=_params("parallel", "parallel", "arbitrary"),
        name="gdn_core",
    )(proj, proj, proj, proj, gates, gc_rows, conv_w, conv_w, conv_w,
      norm_w.astype(F32).reshape(1, dh))


def _row_tile(n, target):
    t = min(n, target)
    while n % t:
        t //= 2
    return t


def kernel(x, mix_norm, ffn_norm, final_norm, moba_w_qkv, moba_w_o, sconv_w_in, sconv_conv,
           sconv_w_out, gdn_w_in, gdn_conv, gdn_a_log, gdn_dt_bias, gdn_norm, gdn_w_o,
           ffn_w_up, ffn_conv, ffn_w_down):
    bsz, seq, d = x.shape
    depth = mix_norm.shape[0]
    t = bsz * seq
    tm = _row_tile(seq, ROW_TILE)
    ffn_tm = _row_tile(seq, FFN_ROW_TILE)
    for i in range(depth):
        kind, j = i % N_MIXERS, i // N_MIXERS
        mix = None
        if kind == 0:
            q, k, vt, k_mean = moba_proj(x, mix_norm[i], moba_w_qkv[j].astype(BF16),
                                         MOBA_HEADS, tm)
            o = moba_attention(q, k, vt, k_mean.reshape(bsz, seq // MOBA_BLOCK, d),
                               MOBA_HEADS)
            mix = (o, moba_w_o[j].astype(BF16))
        elif kind == 1:
            x = short_conv_layer(x, mix_norm[i], sconv_w_in[j].astype(BF16), sconv_conv[j],
                                 sconv_w_out[j].astype(BF16), tm)
        else:
            n_in = gdn_w_in.shape[-1]
            w_in = jnp.pad(gdn_w_in[j], ((0, 0), (0, 4 * d + LANES - n_in))).astype(BF16)
            proj, raw_gates = gdn_proj(x.reshape(t, d), mix_norm[i], w_in, tm)
            gates = gdn_gates(raw_gates, gdn_a_log[j], gdn_dt_bias[j], GDN_HEADS, tm)
            o = gdn_core(proj.reshape(bsz, seq, 4 * d), gates.reshape(bsz, seq, LANES),
                         gdn_conv[j], gdn_norm[j], GDN_HEADS, _row_tile(seq, GDN_ROW_TILE))
            mix = (o, gdn_w_o[j].astype(BF16))
        x = conv_ffn_layer(x, ffn_norm[i], ffn_w_up[i].astype(BF16), ffn_conv[i],
                           ffn_w_down[i].astype(BF16),
                           final_norm if i == depth - 1 else None, ffn_tm, FFN_COL_TILE, mix)
    return x
```

```python
import functools

import jax
import jax.numpy as jnp
from jax import lax
from jax.experimental import pallas as pl
from jax.experimental.pallas import tpu as pltpu

F32 = jnp.float32
BF16 = jnp.bfloat16

N_MIXERS = 3
MOBA_HEADS = 8
MOBA_BLOCK = 256
MOBA_TOPK = 3
GDN_HEADS = 8
GDN_CHUNK = 64
NORM_EPS = 1e-6
NEG_INF = -1e30
LOG2E = 1.4426950408889634
MOBA_GROUP = 4
MOBA_HEADS_PER_STEP = 4
GDN_HEADS_PER_STEP = 4
ROW_TILE = 512
FFN_ROW_TILE = 512
FFN_COL_TILE = 256
GDN_ROW_TILE = 512

LANES = 128
SUBLANES = 8
BF16_SUBLANES = 16
VMEM_LIMIT = 56 * 1024 * 1024


def _params(*sem):
    return pltpu.CompilerParams(dimension_semantics=sem, vmem_limit_bytes=VMEM_LIMIT)


def _rmsnorm(x, g):
    return x * lax.rsqrt(jnp.mean(x * x, axis=-1, keepdims=True) + NORM_EPS) * g


def _silu(x):
    return x * jax.nn.sigmoid(x)


def _resident(shape):
    return pl.BlockSpec(shape, lambda *_: (0,) * len(shape), pipeline_mode=pl.Buffered(1))


def _gdn_proj_kernel(x_ref, g_ref, w_ref, main_ref, gate_ref):
    hn = _rmsnorm(x_ref[...], g_ref[...]).astype(BF16)
    y = jnp.dot(hn, w_ref[...], preferred_element_type=F32)
    n_main = main_ref.shape[1]
    main_ref[...] = y[:, 0:n_main].astype(main_ref.dtype)
    gate_ref[...] = y[:, n_main:]


def gdn_proj(x2, g, w, tm):
    t, d = x2.shape
    n = w.shape[1]
    n_main = n - LANES
    return pl.pallas_call(
        _gdn_proj_kernel,
        grid=(t // tm,),
        in_specs=[pl.BlockSpec((tm, d), lambda i: (i, 0)), _resident((1, d)),
                  _resident((d, n))],
        out_specs=[pl.BlockSpec((tm, n_main), lambda i: (i, 0)),
                   pl.BlockSpec((tm, LANES), lambda i: (i, 0))],
        out_shape=[jax.ShapeDtypeStruct((t, n_main), BF16),
                   jax.ShapeDtypeStruct((t, LANES), F32)],
        compiler_params=_params("parallel"),
        name="gdn_proj",
    )(x2, g.reshape(1, d), w)


def _causal_conv_rows(buf_ref, tail_ref, cur, cw, width, tm):
    buf_ref[0:SUBLANES, :] = tail_ref[...]
    buf_ref[SUBLANES:SUBLANES + tm, :] = cur
    out = cw[width - 1:width, :] * cur
    for k in range(width - 1):
        off = SUBLANES - (width - 1) + k
        out = out + cw[k:k + 1, :] * buf_ref[off:off + tm, :]
    tail_ref[...] = cur[tm - SUBLANES:tm, :]
    return out


def _ffn_kernel(x_ref, g_ref, wup_ref, cw_ref, wdn_ref, *rest, ffn, tf, width, final, mixed):
    rest = list(rest)
    mix_ref, wo_ref = (rest.pop(0), rest.pop(0)) if mixed else (None, None)
    fn_ref = rest.pop(0) if final else None
    out_ref, gbuf, ubuf, tail_ref, act_ref = rest
    tm = x_ref.shape[0]

    @pl.when(pl.program_id(1) == 0)
    def _():
        tail_ref[...] = jnp.zeros(tail_ref.shape, F32)

    x = x_ref[...]
    if mixed:
        x = x + jnp.dot(mix_ref[...], wo_ref[...], preferred_element_type=F32)
    hn = _rmsnorm(x, g_ref[...]).astype(BF16)
    for c in range(ffn // tf):
        lo = c * tf
        hi = ffn + lo
        gate = jnp.dot(hn, wup_ref[:, lo:lo + tf], preferred_element_type=F32)
        up = jnp.dot(hn, wup_ref[:, hi:hi + tf], preferred_element_type=F32)
        gate = _causal_conv_rows(gbuf.at[c % 2], tail_ref.at[:, lo:lo + tf], gate,
                                 cw_ref[:, lo:lo + tf], width, tm)
        up = _causal_conv_rows(ubuf.at[c % 2], tail_ref.at[:, hi:hi + tf], up,
                               cw_ref[:, hi:hi + tf], width, tm)
        act_ref[:, lo:lo + tf] = (_silu(gate) * up).astype(BF16)
    res = x + jnp.dot(act_ref[...], wdn_ref[...], preferred_element_type=F32)
    if final:
        res = _rmsnorm(res, fn_ref[...])
    out_ref[...] = res


def conv_ffn_layer(x, g, w_up, conv_w, w_down, final_g, tm, tf, mix=None):
    b, s, d = x.shape
    ffn = w_down.shape[0]
    width = conv_w.shape[0]
    final = final_g is not None
    row = pl.BlockSpec((None, tm, d), lambda bi, i: (bi, i, 0))
    in_specs = [row, _resident((1, d)), _resident(w_up.shape), _resident(conv_w.shape),
                _resident(w_down.shape)]
    args = [x, g.reshape(1, d), w_up, conv_w, w_down]
    if mix is not None:
        in_specs += [row, _resident(mix[1].shape)]
        args += list(mix)
    if final:
        in_specs.append(_resident((1, d)))
        args.append(final_g.reshape(1, d))
    return pl.pallas_call(
        functools.partial(_ffn_kernel, ffn=ffn, tf=tf, width=width, final=final,
                          mixed=mix is not None),
        grid=(b, s // tm),
        in_specs=in_specs,
        out_specs=row,
        out_shape=jax.ShapeDtypeStruct((b, s, d), F32),
        scratch_shapes=[pltpu.VMEM((2, tm + SUBLANES, tf), F32),
                        pltpu.VMEM((2, tm + SUBLANES, tf), F32),
                        pltpu.VMEM((SUBLANES, 2 * ffn), F32),
                        pltpu.VMEM((tm, ffn), BF16)],
        compiler_params=_params("parallel", "arbitrary"),
        name="conv_ffn",
    )(*args)


def _sconv_kernel(x_ref, g_ref, win_ref, cw_ref, wout_ref, out_ref, cbuf, tail_ref, *, width):
    tm, d = x_ref.shape

    @pl.when(pl.program_id(1) == 0)
    def _():
        tail_ref[...] = jnp.zeros(tail_ref.shape, F32)

    x = x_ref[...]
    hn = _rmsnorm(x, g_ref[...]).astype(BF16)
    b_gate = jnp.dot(hn, win_ref[:, 0:d], preferred_element_type=F32)
    c_gate = jnp.dot(hn, win_ref[:, d:2 * d], preferred_element_type=F32)
    xv = jnp.dot(hn, win_ref[:, 2 * d:3 * d], preferred_element_type=F32)
    y = b_gate * _causal_conv_rows(cbuf, tail_ref, c_gate * xv, cw_ref[...], width, tm)
    out_ref[...] = x + jnp.dot(y.astype(BF16), wout_ref[...], preferred_element_type=F32)


def short_conv_layer(x, g, w_in, conv_w, w_out, tm):
    b, s, d = x.shape
    row = pl.BlockSpec((None, tm, d), lambda bi, i: (bi, i, 0))
    return pl.pallas_call(
        functools.partial(_sconv_kernel, width=conv_w.shape[0]),
        grid=(b, s // tm),
        in_specs=[row, _resident((1, d)), _resident(w_in.shape), _resident(conv_w.shape),
                  _resident(w_out.shape)],
        out_specs=row,
        out_shape=jax.ShapeDtypeStruct((b, s, d), F32),
        scratch_shapes=[pltpu.VMEM((tm + SUBLANES, d), F32), pltpu.VMEM((SUBLANES, d), F32)],
        compiler_params=_params("parallel", "arbitrary"),
        name="short_conv",
    )(x, g.reshape(1, d), w_in, conv_w, w_out)


def _moba_proj_kernel(x_ref, g_ref, w_ref, q_ref, k_ref, vt_ref, km_ref,
                      *, blk, q_scale, pad_tiles):
    step = pl.program_id(1)

    @pl.when(step < pad_tiles)
    def _():
        k_ref[...] = jnp.zeros(k_ref.shape, BF16)
        vt_ref[...] = jnp.zeros(vt_ref.shape, BF16)

    @pl.when(step >= pad_tiles)
    def _():
        tm, d = x_ref.shape
        n_heads, rows, _ = vt_ref.shape
        hd = d // n_heads
        hn = _rmsnorm(x_ref[...], g_ref[...]).astype(BF16)
        y = jnp.dot(hn, w_ref[...], preferred_element_type=F32)
        q_ref[...] = (y[:, 0:d] * q_scale).astype(BF16)
        k_ref[...] = y[:, d:2 * d].astype(BF16)
        vt_ref[:, 0:hd, :] = y[:, 2 * d:3 * d].T.reshape(n_heads, hd, tm).astype(BF16)
        vt_ref[:, hd:rows, :] = jnp.ones((n_heads, rows - hd, tm), BF16)
        km = jnp.mean(y[:, d:2 * d].reshape(tm // blk, blk, d), axis=1)
        km_ref[...] = km[:, None, :]


def moba_proj(x, g, w, n_heads, tm):
    b, s, d = x.shape
    blk = MOBA_BLOCK
    hd = d // n_heads
    rows = hd + BF16_SUBLANES
    pad = MOBA_GROUP * blk
    pad_tiles = pad // tm
    assert pad_tiles * tm == pad

    def real(i):
        return jnp.maximum(i - pad_tiles, 0)

    return pl.pallas_call(
        functools.partial(_moba_proj_kernel, blk=blk, q_scale=hd ** -0.5 * LOG2E,
                          pad_tiles=pad_tiles),
        grid=(b, s // tm + pad_tiles),
        in_specs=[pl.BlockSpec((None, tm, d), lambda bi, i: (bi, real(i), 0)),
                  _resident((1, d)), _resident(w.shape)],
        out_specs=[pl.BlockSpec((None, tm, d), lambda bi, i: (bi, real(i), 0)),
                   pl.BlockSpec((None, tm, d), lambda bi, i: (bi, i, 0)),
                   pl.BlockSpec((None, n_heads, rows, tm), lambda bi, i: (bi, 0, 0, i)),
                   pl.BlockSpec((None, tm // blk, 1, d), lambda bi, i: (bi, real(i), 0, 0))],
        out_shape=[jax.ShapeDtypeStruct((b, s, d), BF16),
                   jax.ShapeDtypeStruct((b, s + pad, d), BF16),
                   jax.ShapeDtypeStruct((b, n_heads, rows, s + pad), BF16),
                   jax.ShapeDtypeStruct((b, s // blk, 1, d), F32)],
        compiler_params=_params("parallel", "arbitrary"),
        name="moba_proj",
    )(x, g.reshape(1, d), w)


def _split3_bf16(x):
    hi = x.astype(BF16)
    r = x - hi.astype(F32)
    mid = r.astype(BF16)
    return hi, mid, (r - mid.astype(F32)).astype(BF16)


def _moba_kernel(slopes_ref, q_ref, k_ref, vt_ref, km_ref, kaug_ref, o_ref, m_ref, acc_ref,
                 u0_ref, u1_ref, top0_ref, top1_ref, *, blk, n_top, group):
    qi = pl.program_id(2)
    tq = q_ref.shape[0]
    n_heads = vt_ref.shape[0]
    hd = q_ref.shape[1] // n_heads
    n_blk = km_ref.shape[0]
    nt = (((1,), (1,)), ((), ()))
    n_groups = (qi + group) // group
    u_refs = (u0_ref, u1_ref)
    top_refs = (top0_ref, top1_ref)
    heads = range(n_heads)
    cols = [slice(a * hd, (a + 1) * hd) for a in heads]
    slope2 = [slopes_ref[pl.program_id(1) * n_heads + a] * LOG2E for a in heads]
    qs = [q_ref[:, cols[a]] for a in heads]

    kk = lax.broadcasted_iota(jnp.int32, (blk, tq), 0)
    qq = lax.broadcasted_iota(jnp.int32, (blk, tq), 1)
    causal = kk <= qq

    lane = lax.broadcasted_iota(jnp.int32, (tq, LANES), 1)
    q_pos = lax.broadcasted_iota(jnp.int32, (tq, LANES), 0).astype(F32)
    q_aug = []
    for a in heads:
        parts = (_split3_bf16(jnp.full((tq, LANES), slope2[a], F32))
                 + _split3_bf16(-q_pos * slope2[a]))
        extra = jnp.zeros((tq, LANES), F32)
        for e, part in enumerate(parts):
            extra = jnp.where(lane == e, part.astype(F32), extra)
        q_aug.append(jnp.concatenate([qs[a], extra.astype(BF16)], axis=1))

    def window_start(g):
        return pl.multiple_of((qi - g * group + 1) * blk, blk)

    def score_group(g, slot, first=False):
        keys = pl.ds(window_start(g), group * blk)
        for a in heads:
            lhs = jnp.concatenate([k_ref[keys, cols[a]], kaug_ref[...]], axis=1)
            s = lax.dot_general(lhs, q_aug[a], nt, preferred_element_type=F32)
            for t in range(group):
                u = s[t * blk:(t + 1) * blk]
                if first and t == group - 1:
                    u = jnp.where(causal, u, NEG_INF)
                u_refs[slot][a, t] = u
                top_refs[slot][a, t] = jnp.max(u, axis=0, keepdims=True)

    score_group(0, 0, first=True)
    for a in heads:
        m_ref[a] = jnp.full((1, tq), NEG_INF, F32)
        acc_ref[a] = jnp.zeros(acc_ref.shape[1:], F32)

    blk_ids = lax.broadcasted_iota(jnp.int32, (n_blk, tq), 0)
    sels = []
    for a in heads:
        gate = lax.dot_general(km_ref[:, cols[a]].astype(BF16), qs[a], nt,
                               preferred_element_type=F32)
        gate = jnp.where(blk_ids < qi, gate, NEG_INF)
        picks = []
        for _ in range(n_top):
            best = jnp.max(gate, axis=0, keepdims=True)
            idx = jnp.min(jnp.where(gate == best, blk_ids, n_blk), axis=0, keepdims=True)
            picks.append(jnp.where(idx < qi, idx, n_blk))
            gate = jnp.where(blk_ids == idx, -jnp.inf, gate)
        sels.append(picks)

    def consume_group(g, slot):
        keys = pl.ds(window_start(g), group * blk)
        for a in heads:
            m_old = m_ref[a]
            m_new = m_old
            hits, offs = [], []
            for t in range(group):
                j = qi - g * group - (group - 1) + t
                hit = (sels[a][0] == j) | (j == qi)
                for sel in sels[a][1:]:
                    hit = hit | (sel == j)
                off = -slope2[a] * ((qi - j) * blk).astype(F32)
                m_new = jnp.maximum(m_new,
                                    jnp.where(hit, top_refs[slot][a, t] + off, NEG_INF))
                hits.append(hit); offs.append(off)
            ps = []
            for t in range(group):
                shift = jnp.where(hits[t], m_new - offs[t], -NEG_INF)
                ps.append(jnp.exp2((u_refs[slot][a, t] - shift).astype(BF16)))
            acc_ref[a] = (jnp.exp2(m_old - m_new) * acc_ref[a]
                          + jnp.dot(vt_ref[a, :, keys], jnp.concatenate(ps, axis=0),
                                    preferred_element_type=F32))
            m_ref[a] = m_new

    def group_body(g, carry):
        for slot in range(2):
            @pl.when(g % 2 == slot)
            def _():
                score_group(g + 1, 1 - slot)
                consume_group(g, slot)
        return carry

    lax.fori_loop(0, n_groups - 1, group_body, 0)
    for slot in range(2):
        @pl.when((n_groups - 1) % 2 == slot)
        def _():
            consume_group(n_groups - 1, slot)

    for a in heads:
        acc = acc_ref[a]
        o_ref[:, cols[a]] = (acc[0:hd] / acc[hd:hd + 1]).T.astype(o_ref.dtype)


def moba_attention(q, k, vt, k_mean, n_heads):
    b, s, d = q.shape
    hd = d // n_heads
    rows = vt.shape[2]
    blk = MOBA_BLOCK
    s_pad = k.shape[1]
    assert s_pad == s + MOBA_GROUP * blk
    n_blk = s // blk
    n_top = min(MOBA_TOPK, n_blk)
    slopes = jnp.exp2(-8.0 * jnp.arange(1, n_heads + 1, dtype=F32) / n_heads)
    key_pos = (jnp.arange(MOBA_GROUP * blk, dtype=jnp.int32) % blk).astype(F32)[:, None]
    lane = jnp.arange(LANES, dtype=jnp.int32)[None, :]
    key_aug = jnp.where(lane < 3, key_pos, jnp.where(lane < 6, 1.0, 0.0)).astype(BF16)
    hps = MOBA_HEADS_PER_STEP
    wide = hps * hd
    once = pl.Buffered(1)
    grid_spec = pltpu.PrefetchScalarGridSpec(
        num_scalar_prefetch=1,
        grid=(b, n_heads // hps, n_blk),
        in_specs=[
            pl.BlockSpec((None, blk, wide), lambda bi, h, i, *_: (bi, i, h)),
            pl.BlockSpec((None, s_pad, wide), lambda bi, h, i, *_: (bi, 0, h),
                         pipeline_mode=once),
            pl.BlockSpec((None, hps, rows, s_pad), lambda bi, h, i, *_: (bi, h, 0, 0),
                         pipeline_mode=once),
            pl.BlockSpec((None, n_blk, wide), lambda bi, h, i, *_: (bi, 0, h)),
            pl.BlockSpec(key_aug.shape, lambda bi, h, i, *_: (0, 0), pipeline_mode=once),
        ],
        out_specs=pl.BlockSpec((None, blk, wide), lambda bi, h, i, *_: (bi, i, h)),
        scratch_shapes=[pltpu.VMEM((hps, 1, blk), F32),
                        pltpu.VMEM((hps, rows, blk), F32),
                        pltpu.VMEM((hps, MOBA_GROUP, blk, blk), F32),
                        pltpu.VMEM((hps, MOBA_GROUP, blk, blk), F32),
                        pltpu.VMEM((hps, MOBA_GROUP, 1, blk), F32),
                        pltpu.VMEM((hps, MOBA_GROUP, 1, blk), F32)],
    )
    return pl.pallas_call(
        functools.partial(_moba_kernel, blk=blk, n_top=n_top, group=MOBA_GROUP),
        grid_spec=grid_spec,
        out_shape=jax.ShapeDtypeStruct((b, s, d), BF16),
        compiler_params=_params("parallel", "parallel", "arbitrary"),
        name="moba_attention",
    )(slopes, q, k, vt, k_mean, key_aug)


def _dot_hi(a, b):
    return jnp.dot(a, b, preferred_element_type=F32, precision=lax.Precision.HIGHEST)


def _split_bf16(a):
    hi = a.astype(BF16)
    return hi, (a - hi.astype(F32)).astype(BF16)


def _dot_split(a, b):
    a_hi, a_lo = _split_bf16(a)
    b_hi, b_lo = _split_bf16(b)
    return jnp.dot(jnp.concatenate([a_hi, a_hi, a_lo], axis=1),
                   jnp.concatenate([b_hi, b_lo, b_hi], axis=0), preferred_element_type=F32)


def _gdn_gates_kernel(sm_ref, alog_ref, dt_ref, o_ref, *, cs, n_heads):
    small = sm_ref[...]
    tm = small.shape[0]
    sp = small + dt_ref[...]
    softplus = jnp.maximum(sp, 0.0) + jnp.log1p(jnp.exp(-jnp.abs(sp)))
    g = -jnp.exp(alog_ref[...]) * softplus
    ii = lax.broadcasted_iota(jnp.int32, (tm, tm), 0)
    jj = lax.broadcasted_iota(jnp.int32, (tm, tm), 1)
    same_chunk_prefix = ((ii // cs) == (jj // cs)) & (ii >= jj)
    gc = _dot_hi(same_chunk_prefix.astype(F32), g)
    lane = lax.broadcasted_iota(jnp.int32, small.shape, 1)
    o_ref[...] = jnp.where(lane < n_heads, jax.nn.sigmoid(small), gc)


def gdn_gates(raw, a_log, dt_bias, n_heads, tm):
    t = raw.shape[0]
    pad = (n_heads, LANES - 2 * n_heads)
    alog_vec = jnp.pad(a_log.astype(F32), pad).reshape(1, LANES)
    dt_vec = jnp.pad(dt_bias.astype(F32), pad).reshape(1, LANES)
    return pl.pallas_call(
        functools.partial(_gdn_gates_kernel, cs=GDN_CHUNK, n_heads=n_heads),
        grid=(t // tm,),
        in_specs=[pl.BlockSpec((tm, LANES), lambda i: (i, 0)),
                  _resident((1, LANES)), _resident((1, LANES))],
        out_specs=pl.BlockSpec((tm, LANES), lambda i: (i, 0)),
        out_shape=jax.ShapeDtypeStruct((t, LANES), F32),
        compiler_params=_params("parallel"),
        name="gdn_gates",
    )(raw, alog_vec, dt_vec)


def _gdn_kernel(q_ref, k_ref, v_ref, z_ref, gate_ref, gcr_ref, cwq_ref, cwk_ref, cwv_ref,
                nw_ref, o_ref, qbuf, kbuf, vbuf, tail_ref, state_ref, *, cs, width, n_heads):
    ts = q_ref.shape[0]
    hps = state_ref.shape[0]
    dh = q_ref.shape[1] // hps
    nc = ts // cs
    heads = range(hps)
    cols = [slice(a * dh, (a + 1) * dh) for a in heads]
    chunks = [slice(c * cs, (c + 1) * cs) for c in range(nc)]
    pairs = [(a, c) for a in heads for c in range(nc)]
    head0 = pl.program_id(1) * hps

    @pl.when(pl.program_id(2) == 0)
    def _():
        tail_ref[...] = jnp.zeros(tail_ref.shape, F32)
        state_ref[...] = jnp.zeros(state_ref.shape, F32)

    def conv_silu(buf, n, ref, cw_ref, a):
        return _silu(_causal_conv_rows(buf.at[a], tail_ref.at[n, a],
                                       ref[:, cols[a]].astype(F32), cw_ref[:, cols[a]],
                                       width, ts))

    gates = gate_ref[...]
    lane = lax.broadcasted_iota(jnp.int32, gates.shape, 1)

    ii = lax.broadcasted_iota(jnp.int32, (cs, cs), 0)
    jj = lax.broadcasted_iota(jnp.int32, (cs, cs), 1)
    incl = ii >= jj
    strict = ii > jj
    eye = (ii == jj).astype(F32)
    nt = (((1,), (1,)), ((), ()))

    ks, gcs, k16s, kbq16s, vks, q_decs = [], [], [], [], [], []
    for a in heads:
        q = conv_silu(qbuf, 0, q_ref, cwq_ref, a)
        k = conv_silu(kbuf, 1, k_ref, cwk_ref, a)
        v = conv_silu(vbuf, 2, v_ref, cwv_ref, a)
        q = q * lax.rsqrt(jnp.sum(q * q, axis=-1, keepdims=True) + NORM_EPS) * (dh ** -0.5)
        k = k * lax.rsqrt(jnp.sum(k * k, axis=-1, keepdims=True) + NORM_EPS)
        beta = jnp.sum(jnp.where(lane == head0 + a, gates, 0.0), axis=1, keepdims=True)
        gc = jnp.sum(jnp.where(lane == n_heads + head0 + a, gates, 0.0), axis=1,
                     keepdims=True)
        e_gc = jnp.exp(gc)
        k_beta = k * beta
        ks.append(k); gcs.append(gc)
        vks.append(jnp.concatenate([(v * beta).astype(BF16),
                                    (k_beta * e_gc).astype(BF16)], axis=1))
        q_decs.append((q * e_gc).astype(BF16))
        kbq16s.append((k_beta.astype(BF16), q.astype(BF16)))
        k16s.append(k.astype(BF16))

    negs, attns = {}, {}
    for a, c in pairs:
        sl = chunks[c]
        diff = gcs[a][sl] - gcr_ref[a, c:c + 1, :]
        decay = jnp.where(incl, jnp.exp(jnp.where(incl, diff, 0.0)), 0.0)
        k_beta16, q16 = kbq16s[a]
        both = lax.dot_general(jnp.concatenate([k_beta16[sl], q16[sl]], axis=0), k16s[a][sl],
                               nt, preferred_element_type=F32)
        negs[a, c] = jnp.where(strict, -(both[0:cs] * decay), 0.0)
        attns[a, c] = (both[cs:2 * cs] * decay).astype(BF16)
    powers = {p: _dot_split(negs[p], negs[p]) for p in pairs}
    t_mats = {p: eye + negs[p] for p in pairs}
    span = 4
    while span <= cs:
        last = span == cs
        prods = {p: _dot_split(t_mats[p] if last else
                               jnp.concatenate([t_mats[p], powers[p]], axis=0), powers[p])
                 for p in pairs}
        t_mats = {p: t_mats[p] + prods[p][0:cs] for p in pairs}
        if not last:
            powers = {p: prods[p][cs:2 * cs] for p in pairs}
        span *= 2

    us, uws, kus, stacks, sds = {}, {}, {}, {}, {}
    for a, c in pairs:
        uws[a, c] = jnp.dot(t_mats[a, c].astype(BF16), vks[a][chunks[c]],
                            preferred_element_type=F32)
    for a, c in pairs:
        sl = chunks[c]
        gc = gcs[a][sl]
        g_last = gc[cs - 1:cs, :]
        kd_t = (ks[a][sl] * jnp.exp(g_last - gc)).T.astype(BF16)
        uw16 = uws[a, c].astype(BF16)
        kuw = jnp.dot(kd_t, uw16, preferred_element_type=F32)
        us[a, c] = uws[a, c][:, 0:dh]
        kus[a, c] = kuw[:, 0:dh]
        stacks[a, c] = jnp.concatenate([kuw[:, dh:2 * dh].astype(BF16), uw16[:, dh:2 * dh],
                                        q_decs[a][sl]], axis=0)
        sds[a, c] = jnp.exp(g_last)

    def emit_output(c, a, inter, v_new):
        sl = chunks[c]
        o = inter + jnp.dot(attns[a, c], v_new.astype(BF16), preferred_element_type=F32)
        o = o * lax.rsqrt(jnp.mean(o * o, axis=-1, keepdims=True) + NORM_EPS) * nw_ref[...]
        o_ref[sl, cols[a]] = (o * _silu(z_ref[sl, cols[a]].astype(F32))).astype(o_ref.dtype)

    states = [state_ref[a] for a in heads]
    pending = None
    for c in range(nc):
        prods = [jnp.dot(stacks[a, c], states[a].astype(BF16), preferred_element_type=F32)
                 for a in heads]
        if pending is not None:
            for a in heads:
                emit_output(c - 1, a, *pending[a])
        pending = []
        for a in heads:
            prod = prods[a]
            states[a] = states[a] * sds[a, c] - prod[0:dh] + kus[a, c]
            pending.append((prod[dh + cs:dh + 2 * cs], us[a, c] - prod[dh:dh + cs]))
    for a in heads:
        emit_output(nc - 1, a, *pending[a])
        state_ref[a] = states[a]


def gdn_core(proj, gates, conv_w, norm_w, n_heads, ts):
    b, s, _ = proj.shape
    dh = norm_w.shape[0]
    d = n_heads * dh
    width = conv_w.shape[0]
    cs = GDN_CHUNK
    gc_rows = gates[:, :, n_heads:2 * n_heads].reshape(b, s // cs, cs, n_heads)
    gc_rows = gc_rows.transpose(0, 3, 1, 2)

    hps = GDN_HEADS_PER_STEP
    wide = hps * dh
    groups = n_heads // hps

    def col(section):
        return pl.BlockSpec((None, ts, wide), lambda bi, h, i: (bi, i, section * groups + h))

    def cw(section):
        return pl.BlockSpec((width, wide), lambda bi, h, i: (0, section * groups + h))

    return pl.pallas_call(
        functools.partial(_gdn_kernel, cs=GDN_CHUNK, width=width, n_heads=n_heads),
        grid=(b, groups, s // ts),
        in_specs=[col(0), col(1), col(2), col(3),
                  pl.BlockSpec((None, ts, LANES), lambda bi, h, i: (bi, i, 0)),
                  pl.BlockSpec((None, hps, ts // cs, cs), lambda bi, h, i: (bi, h, i, 0)),
                  cw(0), cw(1), cw(2), _resident((1, dh))],
        out_specs=pl.BlockSpec((None, ts, wide), lambda bi, h, i: (bi, i, h)),
        out_shape=jax.ShapeDtypeStruct((b, s, d), BF16),
        scratch_shapes=[pltpu.VMEM((hps, ts + SUBLANES, dh), F32)] * 3
                       + [pltpu.VMEM((3, hps, SUBLANES, dh), F32),
                          pltpu.VMEM((hps, dh, dh), F32)],
        compiler_params=_params("parallel", "parallel", "arbitrary"),
        name="gdn_core",
    )(proj, proj, proj, proj, gates, gc_rows, conv_w, conv_w, conv_w,
      norm_w.astype(F32).reshape(1, dh))


def _row_tile(n, target):
    t = min(n, target)
    while n % t:
        t //= 2
    return t


def kernel(x, mix_norm, ffn_norm, final_norm, moba_w_qkv, moba_w_o, sconv_w_in, sconv_conv,
           sconv_w_out, gdn_w_in, gdn_conv, gdn_a_log, gdn_dt_bias, gdn_norm, gdn_w_o,
           ffn_w_up, ffn_conv, ffn_w_down):
    bsz, seq, d = x.shape
    depth = mix_norm.shape[0]
    t = bsz * seq
    tm = _row_tile(seq, ROW_TILE)
    ffn_tm = _row_tile(seq, FFN_ROW_TILE)
    for i in range(depth):
        kind, j = i % N_MIXERS, i // N_MIXERS
        mix = None
        if kind == 0:
            q, k, vt, k_mean = moba_proj(x, mix_norm[i], moba_w_qkv[j].astype(BF16),
                                         MOBA_HEADS, tm)
            o = moba_attention(q, k, vt, k_mean.reshape(bsz, seq // MOBA_BLOCK, d),
                               MOBA_HEADS)
            mix = (o, moba_w_o[j].astype(BF16))
        elif kind == 1:
            x = short_conv_layer(x, mix_norm[i], sconv_w_in[j].astype(BF16), sconv_conv[j],
                                 sconv_w_out[j].astype(BF16), tm)
        else:
            n_in = gdn_w_in.shape[-1]
            w_in = jnp.pad(gdn_w_in[j], ((0, 0), (0, 4 * d + LANES - n_in))).astype(BF16)
            proj, raw_gates = gdn_proj(x.reshape(t, d), mix_norm[i], w_in, tm)
            gates = gdn_gates(raw_gates, gdn_a_log[j], gdn_dt_bias[j], GDN_HEADS, tm)
            o = gdn_core(proj.reshape(bsz, seq, 4 * d), gates.reshape(bsz, seq, LANES),
                         gdn_conv[j], gdn_norm[j], GDN_HEADS, _row_tile(seq, GDN_ROW_TILE))
            mix = (o, gdn_w_o[j].astype(BF16))
        x = conv_ffn_layer(x, ffn_norm[i], ffn_w_up[i].astype(BF16), ffn_conv[i],
                           ffn_w_down[i].astype(BF16),
                           final_norm if i == depth - 1 else None, ffn_tm, FFN_COL_TILE, mix)
    return x
```

```python
import functools

import jax
import jax.numpy as jnp
from jax import lax
from jax.experimental import pallas as pl
from jax.experimental.pallas import tpu as pltpu

F32 = jnp.float32
BF16 = jnp.bfloat16

N_MIXERS = 3
MOBA_HEADS = 8
MOBA_BLOCK = 256
MOBA_TOPK = 3
GDN_HEADS = 8
GDN_CHUNK = 64
NORM_EPS = 1e-6
NEG_INF = -1e30
LOG2E = 1.4426950408889634
MOBA_GROUP = 4
MOBA_HEADS_PER_STEP = 4
GDN_HEADS_PER_STEP = 4
ROW_TILE = 512
FFN_ROW_TILE = 512
FFN_COL_TILE = 256
GDN_ROW_TILE = 512

LANES = 128
SUBLANES = 8
BF16_SUBLANES = 16
VMEM_LIMIT = 56 * 1024 * 1024


def _params(*sem):
    return pltpu.CompilerParams(dimension_semantics=sem, vmem_limit_bytes=VMEM_LIMIT)


def _rmsnorm(x, g):
    return x * lax.rsqrt(jnp.mean(x * x, axis=-1, keepdims=True) + NORM_EPS) * g


def _silu(x):
    return x * jax.nn.sigmoid(x)


def _resident(shape):
    return pl.BlockSpec(shape, lambda *_: (0,) * len(shape), pipeline_mode=pl.Buffered(1))


def _gdn_proj_kernel(x_ref, g_ref, w_ref, main_ref, gate_ref):
    hn = _rmsnorm(x_ref[...], g_ref[...]).astype(BF16)
    y = jnp.dot(hn, w_ref[...], preferred_element_type=F32)
    n_main = main_ref.shape[1]
    main_ref[...] = y[:, 0:n_main].astype(main_ref.dtype)
    gate_ref[...] = y[:, n_main:]


def gdn_proj(x2, g, w, tm):
    t, d = x2.shape
    n = w.shape[1]
    n_main = n - LANES
    return pl.pallas_call(
        _gdn_proj_kernel,
        grid=(t // tm,),
        in_specs=[pl.BlockSpec((tm, d), lambda i: (i, 0)), _resident((1, d)),
                  _resident((d, n))],
        out_specs=[pl.BlockSpec((tm, n_main), lambda i: (i, 0)),
                   pl.BlockSpec((tm, LANES), lambda i: (i, 0))],
        out_shape=[jax.ShapeDtypeStruct((t, n_main), BF16),
                   jax.ShapeDtypeStruct((t, LANES), F32)],
        compiler_params=_params("parallel"),
        name="gdn_proj",
    )(x2, g.reshape(1, d), w)


def _causal_conv_rows(buf_ref, tail_ref, cur, cw, width, tm):
    buf_ref[0:SUBLANES, :] = tail_ref[...]
    buf_ref[SUBLANES:SUBLANES + tm, :] = cur
    out = cw[width - 1:width, :] * cur
    for k in range(width - 1):
        off = SUBLANES - (width - 1) + k
        out = out + cw[k:k + 1, :] * buf_ref[off:off + tm, :]
    tail_ref[...] = cur[tm - SUBLANES:tm, :]
    return out


def _ffn_kernel(x_ref, g_ref, wup_ref, cw_ref, wdn_ref, *rest, ffn, tf, width, final, mixed):
    rest = list(rest)
    mix_ref, wo_ref = (rest.pop(0), rest.pop(0)) if mixed else (None, None)
    fn_ref = rest.pop(0) if final else None
    out_ref, gbuf, ubuf, tail_ref, act_ref = rest
    tm = x_ref.shape[0]

    @pl.when(pl.program_id(1) == 0)
    def _():
        tail_ref[...] = jnp.zeros(tail_ref.shape, F32)

    x = x_ref[...]
    if mixed:
        x = x + jnp.dot(mix_ref[...], wo_ref[...], preferred_element_type=F32)
    hn = _rmsnorm(x, g_ref[...]).astype(BF16)
    for c in range(ffn // tf):
        lo = c * tf
        hi = ffn + lo
        gate = jnp.dot(hn, wup_ref[:, lo:lo + tf], preferred_element_type=F32)
        up = jnp.dot(hn, wup_ref[:, hi:hi + tf], preferred_element_type=F32)
        gate = _causal_conv_rows(gbuf.at[c % 2], tail_ref.at[:, lo:lo + tf], gate,
                                 cw_ref[:, lo:lo + tf], width, tm)
        up = _causal_conv_rows(ubuf.at[c % 2], tail_ref.at[:, hi:hi + tf], up,
                               cw_ref[:, hi:hi + tf], width, tm)
        act_ref[:, lo:lo + tf] = (_silu(gate) * up).astype(BF16)
    res = x + jnp.dot(act_ref[...], wdn_ref[...], preferred_element_type=F32)
    if final:
        res = _rmsnorm(res, fn_ref[...])
    out_ref[...] = res


def conv_ffn_layer(x, g, w_up, conv_w, w_down, final_g, tm, tf, mix=None):
    b, s, d = x.shape
    ffn = w_down.shape[0]
    width = conv_w.shape[0]
    final = final_g is not None
    row = pl.BlockSpec((None, tm, d), lambda bi, i: (bi, i, 0))
    in_specs = [row, _resident((1, d)), _resident(w_up.shape), _resident(conv_w.shape),
                _resident(w_down.shape)]
    args = [x, g.reshape(1, d), w_up, conv_w, w_down]
    if mix is not None:
        in_specs += [row, _resident(mix[1].shape)]
        args += list(mix)
    if final:
        in_specs.append(_resident((1, d)))
        args.append(final_g.reshape(1, d))
    return pl.pallas_call(
        functools.partial(_ffn_kernel, ffn=ffn, tf=tf, width=width, final=final,
                          mixed=mix is not None),
        grid=(b, s // tm),
        in_specs=in_specs,
        out_specs=row,
        out_shape=jax.ShapeDtypeStruct((b, s, d), F32),
        scratch_shapes=[pltpu.VMEM((2, tm + SUBLANES, tf), F32),
                        pltpu.VMEM((2, tm + SUBLANES, tf), F32),
                        pltpu.VMEM((SUBLANES, 2 * ffn), F32),
                        pltpu.VMEM((tm, ffn), BF16)],
        compiler_params=_params("parallel", "arbitrary"),
        name="conv_ffn",
    )(*args)


def _sconv_kernel(x_ref, g_ref, win_ref, cw_ref, wout_ref, out_ref, cbuf, tail_ref, *, width):
    tm, d = x_ref.shape

    @pl.when(pl.program_id(1) == 0)
    def _():
        tail_ref[...] = jnp.zeros(tail_ref.shape, F32)

    x = x_ref[...]
    hn = _rmsnorm(x, g_ref[...]).astype(BF16)
    b_gate = jnp.dot(hn, win_ref[:, 0:d], preferred_element_type=F32)
    c_gate = jnp.dot(hn, win_ref[:, d:2 * d], preferred_element_type=F32)
    xv = jnp.dot(hn, win_ref[:, 2 * d:3 * d], preferred_element_type=F32)
    y = b_gate * _causal_conv_rows(cbuf, tail_ref, c_gate * xv, cw_ref[...], width, tm)
    out_ref[...] = x + jnp.dot(y.astype(BF16), wout_ref[...], preferred_element_type=F32)


def short_conv_layer(x, g, w_in, conv_w, w_out, tm):
    b, s, d = x.shape
    row = pl.BlockSpec((None, tm, d), lambda bi, i: (bi, i, 0))
    return pl.pallas_call(
        functools.partial(_sconv_kernel, width=conv_w.shape[0]),
        grid=(b, s // tm),
        in_specs=[row, _resident((1, d)), _resident(w_in.shape), _resident(conv_w.shape),
                  _resident(w_out.shape)],
        out_specs=row,
        out_shape=jax.ShapeDtypeStruct((b, s, d), F32),
        scratch_shapes=[pltpu.VMEM((tm + SUBLANES, d), F32), pltpu.VMEM((SUBLANES, d), F32)],
        compiler_params=_params("parallel", "arbitrary"),
        name="short_conv",
    )(x, g.reshape(1, d), w_in, conv_w, w_out)


def _moba_proj_kernel(x_ref, g_ref, w_ref, q_ref, k_ref, vt_ref, km_ref,
                      *, blk, q_scale, pad_tiles):
    step = pl.program_id(1)

    @pl.when(step < pad_tiles)
    def _():
        k_ref[...] = jnp.zeros(k_ref.shape, BF16)
        vt_ref[...] = jnp.zeros(vt_ref.shape, BF16)

    @pl.when(step >= pad_tiles)
    def _():
        tm, d = x_ref.shape
        n_heads, rows, _ = vt_ref.shape
        hd = d // n_heads
        hn = _rmsnorm(x_ref[...], g_ref[...]).astype(BF16)
        y = jnp.dot(hn, w_ref[...], preferred_element_type=F32)
        q_ref[...] = (y[:, 0:d] * q_scale).astype(BF16)
        k_ref[...] = y[:, d:2 * d].astype(BF16)
        vt_ref[:, 0:hd, :] = y[:, 2 * d:3 * d].T.reshape(n_heads, hd, tm).astype(BF16)
        vt_ref[:, hd:rows, :] = jnp.ones((n_heads, rows - hd, tm), BF16)
        km = jnp.mean(y[:, d:2 * d].reshape(tm // blk, blk, d), axis=1)
        km_ref[...] = km[:, None, :]


def moba_proj(x, g, w, n_heads, tm):
    b, s, d = x.shape
    blk = MOBA_BLOCK
    hd = d // n_heads
    rows = hd + BF16_SUBLANES
    pad = MOBA_GROUP * blk
    pad_tiles = pad // tm
    assert pad_tiles * tm == pad

    def real(i):
        return jnp.maximum(i - pad_tiles, 0)

    return pl.pallas_call(
        functools.partial(_moba_proj_kernel, blk=blk, q_scale=hd ** -0.5 * LOG2E,
                          pad_tiles=pad_tiles),
        grid=(b, s // tm + pad_tiles),
        in_specs=[pl.BlockSpec((None, tm, d), lambda bi, i: (bi, real(i), 0)),
                  _resident((1, d)), _resident(w.shape)],
        out_specs=[pl.BlockSpec((None, tm, d), lambda bi, i: (bi, real(i), 0)),
                   pl.BlockSpec((None, tm, d), lambda bi, i: (bi, i, 0)),
                   pl.BlockSpec((None, n_heads, rows, tm), lambda bi, i: (bi, 0, 0, i)),
                   pl.BlockSpec((None, tm // blk, 1, d), lambda bi, i: (bi, real(i), 0, 0))],
        out_shape=[jax.ShapeDtypeStruct((b, s, d), BF16),
                   jax.ShapeDtypeStruct((b, s + pad, d), BF16),
                   jax.ShapeDtypeStruct((b, n_heads, rows, s + pad), BF16),
                   jax.ShapeDtypeStruct((b, s // blk, 1, d), F32)],
        compiler_params=_params("parallel", "arbitrary"),
        name="moba_proj",
    )(x, g.reshape(1, d), w)


def _split3_bf16(x):
    hi = x.astype(BF16)
    r = x - hi.astype(F32)
    mid = r.astype(BF16)
    return hi, mid, (r - mid.astype(F32)).astype(BF16)


def _moba_kernel(slopes_ref, q_ref, k_ref, vt_ref, km_ref, kaug_ref, o_ref, m_ref, acc_ref,
                 u0_ref, u1_ref, top0_ref, top1_ref, *, blk, n_top, group):
    qi = pl.program_id(2)
    tq = q_ref.shape[0]
    n_heads = vt_ref.shape[0]
    hd = q_ref.shape[1] // n_heads
    n_blk = km_ref.shape[0]
    nt = (((1,), (1,)), ((), ()))
    n_groups = (qi + group) // group
    u_refs = (u0_ref, u1_ref)
    top_refs = (top0_ref, top1_ref)
    heads = range(n_heads)
    cols = [slice(a * hd, (a + 1) * hd) for a in heads]
    slope2 = [slopes_ref[pl.program_id(1) * n_heads + a] * LOG2E for a in heads]
    qs = [q_ref[:, cols[a]] for a in heads]

    kk = lax.broadcasted_iota(jnp.int32, (blk, tq), 0)
    qq = lax.broadcasted_iota(jnp.int32, (blk, tq), 1)
    causal = kk <= qq

    lane = lax.broadcasted_iota(jnp.int32, (tq, LANES), 1)
    q_pos = lax.broadcasted_iota(jnp.int32, (tq, LANES), 0).astype(F32)
    q_aug = []
    for a in heads:
        parts = (_split3_bf16(jnp.full((tq, LANES), slope2[a], F32))
                 + _split3_bf16(-q_pos * slope2[a]))
        extra = jnp.zeros((tq, LANES), F32)
        for e, part in enumerate(parts):
            extra = jnp.where(lane == e, part.astype(F32), extra)
        q_aug.append(jnp.concatenate([qs[a], extra.astype(BF16)], axis=1))

    def window_start(g):
        return pl.multiple_of((qi - g * group + 1) * blk, blk)

    def score_group(g, slot, first=False, only=None):
        keys = pl.ds(window_start(g), group * blk)
        for a in (heads if only is None else only):
            lhs = jnp.concatenate([k_ref[keys, cols[a]], kaug_ref[...]], axis=1)
            s = lax.dot_general(lhs, q_aug[a], nt, preferred_element_type=F32)
            for t in range(group):
                u = s[t * blk:(t + 1) * blk]
                if first and t == group - 1:
                    u = jnp.where(causal, u, NEG_INF)
                u_refs[slot][a, t] = u
                top_refs[slot][a, t] = jnp.max(u, axis=0, keepdims=True)

    score_group(0, 0, first=True)
    for a in heads:
        m_ref[a] = jnp.full((1, tq), NEG_INF, F32)
        acc_ref[a] = jnp.zeros(acc_ref.shape[1:], F32)

    blk_ids = lax.broadcasted_iota(jnp.int32, (n_blk, tq), 0)
    sels = []
    for a in heads:
        gate = lax.dot_general(km_ref[:, cols[a]].astype(BF16), qs[a], nt,
                               preferred_element_type=F32)
        gate = jnp.where(blk_ids < qi, gate, NEG_INF)
        picks = []
        for _ in range(n_top):
            best = jnp.max(gate, axis=0, keepdims=True)
            idx = jnp.min(jnp.where(gate == best, blk_ids, n_blk), axis=0, keepdims=True)
            picks.append(jnp.where(idx < qi, idx, n_blk))
            gate = jnp.where(blk_ids == idx, -jnp.inf, gate)
        sels.append(picks)

    def consume_group(g, slot, only=None):
        keys = pl.ds(window_start(g), group * blk)
        for a in (heads if only is None else only):
            m_old = m_ref[a]
            m_new = m_old
            hits, offs = [], []
            for t in range(group):
                j = qi - g * group - (group - 1) + t
                hit = (sels[a][0] == j) | (j == qi)
                for sel in sels[a][1:]:
                    hit = hit | (sel == j)
                off = -slope2[a] * ((qi - j) * blk).astype(F32)
                m_new = jnp.maximum(m_new,
                                    jnp.where(hit, top_refs[slot][a, t] + off, NEG_INF))
                hits.append(hit); offs.append(off)
            ps = []
            for t in range(group):
                shift = jnp.where(hits[t], m_new - offs[t], -NEG_INF)
                ps.append(jnp.exp2((u_refs[slot][a, t] - shift).astype(BF16)))
            acc_ref[a] = (jnp.exp2(m_old - m_new) * acc_ref[a]
                          + jnp.dot(vt_ref[a, :, keys], jnp.concatenate(ps, axis=0),
                                    preferred_element_type=F32))
            m_ref[a] = m_new

    def group_body(g, carry):
        for slot in range(2):
            @pl.when(g % 2 == slot)
            def _():
                for a in heads:
                    score_group(g + 1, 1 - slot, only=[a])
                    consume_group(g, slot, only=[a])
        return carry

    lax.fori_loop(0, n_groups - 1, group_body, 0)
    for slot in range(2):
        @pl.when((n_groups - 1) % 2 == slot)
        def _():
            consume_group(n_groups - 1, slot)

    for a in heads:
        acc = acc_ref[a]
        o_ref[:, cols[a]] = (acc[0:hd] / acc[hd:hd + 1]).T.astype(o_ref.dtype)


def moba_attention(q, k, vt, k_mean, n_heads):
    b, s, d = q.shape
    hd = d // n_heads
    rows = vt.shape[2]
    blk = MOBA_BLOCK
    s_pad = k.shape[1]
    assert s_pad == s + MOBA_GROUP * blk
    n_blk = s // blk
    n_top = min(MOBA_TOPK, n_blk)
    slopes = jnp.exp2(-8.0 * jnp.arange(1, n_heads + 1, dtype=F32) / n_heads)
    key_pos = (jnp.arange(MOBA_GROUP * blk, dtype=jnp.int32) % blk).astype(F32)[:, None]
    lane = jnp.arange(LANES, dtype=jnp.int32)[None, :]
    key_aug = jnp.where(lane < 3, key_pos, jnp.where(lane < 6, 1.0, 0.0)).astype(BF16)
    hps = MOBA_HEADS_PER_STEP
    wide = hps * hd
    once = pl.Buffered(1)
    grid_spec = pltpu.PrefetchScalarGridSpec(
        num_scalar_prefetch=1,
        grid=(b, n_heads // hps, n_blk),
        in_specs=[
            pl.BlockSpec((None, blk, wide), lambda bi, h, i, *_: (bi, i, h)),
            pl.BlockSpec((None, s_pad, wide), lambda bi, h, i, *_: (bi, 0, h),
                         pipeline_mode=once),
            pl.BlockSpec((None, hps, rows, s_pad), lambda bi, h, i, *_: (bi, h, 0, 0),
                         pipeline_mode=once),
            pl.BlockSpec((None, n_blk, wide), lambda bi, h, i, *_: (bi, 0, h)),
            pl.BlockSpec(key_aug.shape, lambda bi, h, i, *_: (0, 0), pipeline_mode=once),
        ],
        out_specs=pl.BlockSpec((None, blk, wide), lambda bi, h, i, *_: (bi, i, h)),
        scratch_shapes=[pltpu.VMEM((hps, 1, blk), F32),
                        pltpu.VMEM((hps, rows, blk), F32),
                        pltpu.VMEM((hps, MOBA_GROUP, blk, blk), F32),
                        pltpu.VMEM((hps, MOBA_GROUP, blk, blk), F32),
                        pltpu.VMEM((hps, MOBA_GROUP, 1, blk), F32),
                        pltpu.VMEM((hps, MOBA_GROUP, 1, blk), F32)],
    )
    return pl.pallas_call(
        functools.partial(_moba_kernel, blk=blk, n_top=n_top, group=MOBA_GROUP),
        grid_spec=grid_spec,
        out_shape=jax.ShapeDtypeStruct((b, s, d), BF16),
        compiler_params=_params("parallel", "parallel", "arbitrary"),
        name="moba_attention",
    )(slopes, q, k, vt, k_mean, key_aug)


def _dot_hi(a, b):
    return jnp.dot(a, b, preferred_element_type=F32, precision=lax.Precision.HIGHEST)


def _split_bf16(a):
    hi = a.astype(BF16)
    return hi, (a - hi.astype(F32)).astype(BF16)


def _dot_split(a, b):
    a_hi, a_lo = _split_bf16(a)
    b_hi, b_lo = _split_bf16(b)
    return jnp.dot(jnp.concatenate([a_hi, a_hi, a_lo], axis=1),
                   jnp.concatenate([b_hi, b_lo, b_hi], axis=0), preferred_element_type=F32)


def _gdn_gates_kernel(sm_ref, alog_ref, dt_ref, o_ref, *, cs, n_heads):
    small = sm_ref[...]
    tm = small.shape[0]
    sp = small + dt_ref[...]
    softplus = jnp.maximum(sp, 0.0) + jnp.log1p(jnp.exp(-jnp.abs(sp)))
    g = -jnp.exp(alog_ref[...]) * softplus
    ii = lax.broadcasted_iota(jnp.int32, (tm, tm), 0)
    jj = lax.broadcasted_iota(jnp.int32, (tm, tm), 1)
    same_chunk_prefix = ((ii // cs) == (jj // cs)) & (ii >= jj)
    gc = _dot_hi(same_chunk_prefix.astype(F32), g)
    lane = lax.broadcasted_iota(jnp.int32, small.shape, 1)
    o_ref[...] = jnp.where(lane < n_heads, jax.nn.sigmoid(small), gc)


def gdn_gates(raw, a_log, dt_bias, n_heads, tm):
    t = raw.shape[0]
    pad = (n_heads, LANES - 2 * n_heads)
    alog_vec = jnp.pad(a_log.astype(F32), pad).reshape(1, LANES)
    dt_vec = jnp.pad(dt_bias.astype(F32), pad).reshape(1, LANES)
    return pl.pallas_call(
        functools.partial(_gdn_gates_kernel, cs=GDN_CHUNK, n_heads=n_heads),
        grid=(t // tm,),
        in_specs=[pl.BlockSpec((tm, LANES), lambda i: (i, 0)),
                  _resident((1, LANES)), _resident((1, LANES))],
        out_specs=pl.BlockSpec((tm, LANES), lambda i: (i, 0)),
        out_shape=jax.ShapeDtypeStruct((t, LANES), F32),
        compiler_params=_params("parallel"),
        name="gdn_gates",
    )(raw, alog_vec, dt_vec)


def _gdn_kernel(q_ref, k_ref, v_ref, z_ref, gate_ref, gcr_ref, cwq_ref, cwk_ref, cwv_ref,
                nw_ref, o_ref, qbuf, kbuf, vbuf, tail_ref, state_ref, *, cs, width, n_heads):
    ts = q_ref.shape[0]
    hps = state_ref.shape[0]
    dh = q_ref.shape[1] // hps
    nc = ts // cs
    heads = range(hps)
    cols = [slice(a * dh, (a + 1) * dh) for a in heads]
    chunks = [slice(c * cs, (c + 1) * cs) for c in range(nc)]
    pairs = [(a, c) for a in heads for c in range(nc)]
    head0 = pl.program_id(1) * hps

    @pl.when(pl.program_id(2) == 0)
    def _():
        tail_ref[...] = jnp.zeros(tail_ref.shape, F32)
        state_ref[...] = jnp.zeros(state_ref.shape, F32)

    def conv_silu(buf, n, ref, cw_ref, a):
        return _silu(_causal_conv_rows(buf.at[a], tail_ref.at[n, a],
                                       ref[:, cols[a]].astype(F32), cw_ref[:, cols[a]],
                                       width, ts))

    gates = gate_ref[...]
    lane = lax.broadcasted_iota(jnp.int32, gates.shape, 1)

    ii = lax.broadcasted_iota(jnp.int32, (cs, cs), 0)
    jj = lax.broadcasted_iota(jnp.int32, (cs, cs), 1)
    incl = ii >= jj
    strict = ii > jj
    eye = (ii == jj).astype(F32)
    nt = (((1,), (1,)), ((), ()))

    ks, gcs, k16s, kbq16s, vks, q_decs = [], [], [], [], [], []
    for a in heads:
        q = conv_silu(qbuf, 0, q_ref, cwq_ref, a)
        k = conv_silu(kbuf, 1, k_ref, cwk_ref, a)
        v = conv_silu(vbuf, 2, v_ref, cwv_ref, a)
        q = q * lax.rsqrt(jnp.sum(q * q, axis=-1, keepdims=True) + NORM_EPS) * (dh ** -0.5)
        k = k * lax.rsqrt(jnp.sum(k * k, axis=-1, keepdims=True) + NORM_EPS)
        beta = jnp.sum(jnp.where(lane == head0 + a, gates, 0.0), axis=1, keepdims=True)
        gc = jnp.sum(jnp.where(lane == n_heads + head0 + a, gates, 0.0), axis=1,
                     keepdims=True)
        e_gc = jnp.exp(gc)
        k_beta = k * beta
        ks.append(k); gcs.append(gc)
        vks.append(jnp.concatenate([(v * beta).astype(BF16),
                                    (k_beta * e_gc).astype(BF16)], axis=1))
        q_decs.append((q * e_gc).astype(BF16))
        kbq16s.append((k_beta.astype(BF16), q.astype(BF16)))
        k16s.append(k.astype(BF16))

    negs, attns = {}, {}
    for a, c in pairs:
        sl = chunks[c]
        diff = gcs[a][sl] - gcr_ref[a, c:c + 1, :]
        decay = jnp.where(incl, jnp.exp(jnp.where(incl, diff, 0.0)), 0.0)
        k_beta16, q16 = kbq16s[a]
        both = lax.dot_general(jnp.concatenate([k_beta16[sl], q16[sl]], axis=0), k16s[a][sl],
                               nt, preferred_element_type=F32)
        negs[a, c] = jnp.where(strict, -(both[0:cs] * decay), 0.0)
        attns[a, c] = (both[cs:2 * cs] * decay).astype(BF16)
    powers = {p: _dot_split(negs[p], negs[p]) for p in pairs}
    t_mats = {p: eye + negs[p] for p in pairs}
    span = 4
    while span <= cs:
        last = span == cs
        prods = {p: _dot_split(t_mats[p] if last else
                               jnp.concatenate([t_mats[p], powers[p]], axis=0), powers[p])
                 for p in pairs}
        t_mats = {p: t_mats[p] + prods[p][0:cs] for p in pairs}
        if not last:
            powers = {p: prods[p][cs:2 * cs] for p in pairs}
        span *= 2

    us, uws, kus, stacks, sds = {}, {}, {}, {}, {}
    for a, c in pairs:
        uws[a, c] = jnp.dot(t_mats[a, c].astype(BF16), vks[a][chunks[c]],
                            preferred_element_type=F32)
    for a, c in pairs:
        sl = chunks[c]
        gc = gcs[a][sl]
        g_last = gc[cs - 1:cs, :]
        kd_t = (ks[a][sl] * jnp.exp(g_last - gc)).T.astype(BF16)
        uw16 = uws[a, c].astype(BF16)
        kuw = jnp.dot(kd_t, uw16, preferred_element_type=F32)
        us[a, c] = uws[a, c][:, 0:dh]
        kus[a, c] = kuw[:, 0:dh]
        stacks[a, c] = jnp.concatenate([kuw[:, dh:2 * dh].astype(BF16), uw16[:, dh:2 * dh],
                                        q_decs[a][sl]], axis=0)
        sds[a, c] = jnp.exp(g_last)

    def emit_output(c, a, inter, v_new):
        sl = chunks[c]
        o = inter + jnp.dot(attns[a, c], v_new.astype(BF16), preferred_element_type=F32)
        o = o * lax.rsqrt(jnp.mean(o * o, axis=-1, keepdims=True) + NORM_EPS) * nw_ref[...]
        o_ref[sl, cols[a]] = (o * _silu(z_ref[sl, cols[a]].astype(F32))).astype(o_ref.dtype)

    states = [state_ref[a] for a in heads]
    pending = None
    for c in range(nc):
        prods = [jnp.dot(stacks[a, c], states[a].astype(BF16), preferred_element_type=F32)
                 for a in heads]
        if pending is not None:
            for a in heads:
                emit_output(c - 1, a, *pending[a])
        pending = []
        for a in heads:
            prod = prods[a]
            states[a] = states[a] * sds[a, c] - prod[0:dh] + kus[a, c]
            pending.append((prod[dh + cs:dh + 2 * cs], us[a, c] - prod[dh:dh + cs]))
    for a in heads:
        emit_output(nc - 1, a, *pending[a])
        state_ref[a] = states[a]


def gdn_core(proj, gates, conv_w, norm_w, n_heads, ts):
    b, s, _ = proj.shape
    dh = norm_w.shape[0]
    d = n_heads * dh
    width = conv_w.shape[0]
    cs = GDN_CHUNK
    gc_rows = gates[:, :, n_heads:2 * n_heads].reshape(b, s // cs, cs, n_heads)
    gc_rows = gc_rows.transpose(0, 3, 1, 2)

    hps = GDN_HEADS_PER_STEP
    wide = hps * dh
    groups = n_heads // hps

    def col(section):
        return pl.BlockSpec((None, ts, wide), lambda bi, h, i: (bi, i, section * groups + h))

    def cw(section):
        return pl.BlockSpec((width, wide), lambda bi, h, i: (0, section * groups + h))

    return pl.pallas_call(
        functools.partial(_gdn_kernel, cs=GDN_CHUNK, width=width, n_heads=n_heads),
        grid=(b, groups, s // ts),
        in_specs=[col(0), col(1), col(2), col(3),
                  pl.BlockSpec((None, ts, LANES), lambda bi, h, i: (bi, i, 0)),
                  pl.BlockSpec((None, hps, ts // cs, cs), lambda bi, h, i: (bi, h, i, 0)),
                  cw(0), cw(1), cw(2), _resident((1, dh))],
        out_specs=pl.BlockSpec((None, ts, wide), lambda bi, h, i: (bi, i, h)),
        out_shape=jax.ShapeDtypeStruct((b, s, d), BF16),
        scratch_shapes=[pltpu.VMEM((hps, ts + SUBLANES, dh), F32)] * 3
                       + [pltpu.VMEM((3, hps, SUBLANES, dh), F32),
                          pltpu.VMEM((hps, dh, dh), F32)],
        compiler_params=_params("parallel", "parallel", "arbitrary"),
        name="gdn_core",
    )(proj, proj, proj, proj, gates, gc_rows, conv_w, conv_w, conv_w,
      norm_w.astype(F32).reshape(1, dh))


def _row_tile(n, target):
    t = min(n, target)
    while n % t:
        t //= 2
    return t


def kernel(x, mix_norm, ffn_norm, final_norm, moba_w_qkv, moba_w_o, sconv_w_in, sconv_conv,
           sconv_w_out, gdn_w_in, gdn_conv, gdn_a_log, gdn_dt_bias, gdn_norm, gdn_w_o,
           ffn_w_up, ffn_conv, ffn_w_down):
    bsz, seq, d = x.shape
    depth = mix_norm.shape[0]
    t = bsz * seq
    tm = _row_tile(seq, ROW_TILE)
    ffn_tm = _row_tile(seq, FFN_ROW_TILE)
    for i in range(depth):
        kind, j = i % N_MIXERS, i // N_MIXERS
        mix = None
        if kind == 0:
            q, k, vt, k_mean = moba_proj(x, mix_norm[i], moba_w_qkv[j].astype(BF16),
                                         MOBA_HEADS, tm)
            o = moba_attention(q, k, vt, k_mean.reshape(bsz, seq // MOBA_BLOCK, d),
                               MOBA_HEADS)
            mix = (o, moba_w_o[j].astype(BF16))
        elif kind == 1:
            x = short_conv_layer(x, mix_norm[i], sconv_w_in[j].astype(BF16), sconv_conv[j],
                                 sconv_w_out[j].astype(BF16), tm)
        else:
            n_in = gdn_w_in.shape[-1]
            w_in = jnp.pad(gdn_w_in[j], ((0, 0), (0, 4 * d + LANES - n_in))).astype(BF16)
            proj, raw_gates = gdn_proj(x.reshape(t, d), mix_norm[i], w_in, tm)
            gates = gdn_gates(raw_gates, gdn_a_log[j], gdn_dt_bias[j], GDN_HEADS, tm)
            o = gdn_core(proj.reshape(bsz, seq, 4 * d), gates.reshape(bsz, seq, LANES),
                         gdn_conv[j], gdn_norm[j], GDN_HEADS, _row_tile(seq, GDN_ROW_TILE))
            mix = (o, gdn_w_o[j].astype(BF16))
        x = conv_ffn_layer(x, ffn_norm[i], ffn_w_up[i].astype(BF16), ffn_conv[i],
                           ffn_w_down[i].astype(BF16),
                           final_norm if i == depth - 1 else None, ffn_tm, FFN_COL_TILE, mix)
    return x
```

```python
import functools

import jax
import jax.numpy as jnp
from jax import lax
from jax.experimental import pallas as pl
from jax.experimental.pallas import tpu as pltpu

F32 = jnp.float32
BF16 = jnp.bfloat16

N_MIXERS = 3
MOBA_HEADS = 8
MOBA_BLOCK = 256
MOBA_TOPK = 3
GDN_HEADS = 8
GDN_CHUNK = 64
NORM_EPS = 1e-6
NEG_INF = -1e30
LOG2E = 1.4426950408889634
MOBA_GROUP = 4
MOBA_HEADS_PER_STEP = 4
GDN_HEADS_PER_STEP = 4
ROW_TILE = 512
FFN_ROW_TILE = 512
FFN_COL_TILE = 256
GDN_ROW_TILE = 512

LANES = 128
SUBLANES = 8
BF16_SUBLANES = 16
VMEM_LIMIT = 56 * 1024 * 1024


def _params(*sem):
    return pltpu.CompilerParams(dimension_semantics=sem, vmem_limit_bytes=VMEM_LIMIT)


def _rmsnorm(x, g):
    return x * lax.rsqrt(jnp.mean(x * x, axis=-1, keepdims=True) + NORM_EPS) * g


def _silu(x):
    return x * jax.nn.sigmoid(x)


def _resident(shape):
    return pl.BlockSpec(shape, lambda *_: (0,) * len(shape), pipeline_mode=pl.Buffered(1))


def _gdn_proj_kernel(x_ref, g_ref, w_ref, main_ref, gate_ref):
    hn = _rmsnorm(x_ref[...], g_ref[...]).astype(BF16)
    y = jnp.dot(hn, w_ref[...], preferred_element_type=F32)
    n_main = main_ref.shape[1]
    main_ref[...] = y[:, 0:n_main].astype(main_ref.dtype)
    gate_ref[...] = y[:, n_main:]


def gdn_proj(x2, g, w, tm):
    t, d = x2.shape
    n = w.shape[1]
    n_main = n - LANES
    return pl.pallas_call(
        _gdn_proj_kernel,
        grid=(t // tm,),
        in_specs=[pl.BlockSpec((tm, d), lambda i: (i, 0)), _resident((1, d)),
                  _resident((d, n))],
        out_specs=[pl.BlockSpec((tm, n_main), lambda i: (i, 0)),
                   pl.BlockSpec((tm, LANES), lambda i: (i, 0))],
        out_shape=[jax.ShapeDtypeStruct((t, n_main), BF16),
                   jax.ShapeDtypeStruct((t, LANES), F32)],
        compiler_params=_params("parallel"),
        name="gdn_proj",
    )(x2, g.reshape(1, d), w)


def _causal_conv_rows(buf_ref, tail_ref, cur, cw, width, tm):
    buf_ref[0:SUBLANES, :] = tail_ref[...]
    buf_ref[SUBLANES:SUBLANES + tm, :] = cur
    out = cw[width - 1:width, :] * cur
    for k in range(width - 1):
        off = SUBLANES - (width - 1) + k
        out = out + cw[k:k + 1, :] * buf_ref[off:off + tm, :]
    tail_ref[...] = cur[tm - SUBLANES:tm, :]
    return out


def _ffn_kernel(x_ref, g_ref, wup_ref, cw_ref, wdn_ref, *rest, ffn, tf, width, final, mixed):
    rest = list(rest)
    mix_ref, wo_ref = (rest.pop(0), rest.pop(0)) if mixed else (None, None)
    fn_ref = rest.pop(0) if final else None
    out_ref, gbuf, ubuf, tail_ref, act_ref = rest
    tm = x_ref.shape[0]

    @pl.when(pl.program_id(1) == 0)
    def _():
        tail_ref[...] = jnp.zeros(tail_ref.shape, F32)

    x = x_ref[...]
    if mixed:
        x = x + jnp.dot(mix_ref[...], wo_ref[...], preferred_element_type=F32)
    hn = _rmsnorm(x, g_ref[...]).astype(BF16)
    for c in range(ffn // tf):
        lo = c * tf
        hi = ffn + lo
        gate = jnp.dot(hn, wup_ref[:, lo:lo + tf], preferred_element_type=F32)
        up = jnp.dot(hn, wup_ref[:, hi:hi + tf], preferred_element_type=F32)
        gate = _causal_conv_rows(gbuf.at[c % 2], tail_ref.at[:, lo:lo + tf], gate,
                                 cw_ref[:, lo:lo + tf], width, tm)
        up = _causal_conv_rows(ubuf.at[c % 2], tail_ref.at[:, hi:hi + tf], up,
                               cw_ref[:, hi:hi + tf], width, tm)
        act_ref[:, lo:lo + tf] = (_silu(gate) * up).astype(BF16)
    res = x + jnp.dot(act_ref[...], wdn_ref[...], preferred_element_type=F32)
    if final:
        res = _rmsnorm(res, fn_ref[...])
    out_ref[...] = res


def conv_ffn_layer(x, g, w_up, conv_w, w_down, final_g, tm, tf, mix=None):
    b, s, d = x.shape
    ffn = w_down.shape[0]
    width = conv_w.shape[0]
    final = final_g is not None
    row = pl.BlockSpec((None, tm, d), lambda bi, i: (bi, i, 0))
    in_specs = [row, _resident((1, d)), _resident(w_up.shape), _resident(conv_w.shape),
                _resident(w_down.shape)]
    args = [x, g.reshape(1, d), w_up, conv_w, w_down]
    if mix is not None:
        in_specs += [row, _resident(mix[1].shape)]
        args += list(mix)
    if final:
        in_specs.append(_resident((1, d)))
        args.append(final_g.reshape(1, d))
    return pl.pallas_call(
        functools.partial(_ffn_kernel, ffn=ffn, tf=tf, width=width, final=final,
                          mixed=mix is not None),
        grid=(b, s // tm),
        in_specs=in_specs,
        out_specs=row,
        out_shape=jax.ShapeDtypeStruct((b, s, d), F32),
        scratch_shapes=[pltpu.VMEM((2, tm + SUBLANES, tf), F32),
                        pltpu.VMEM((2, tm + SUBLANES, tf), F32),
                        pltpu.VMEM((SUBLANES, 2 * ffn), F32),
                        pltpu.VMEM((tm, ffn), BF16)],
        compiler_params=_params("parallel", "arbitrary"),
        name="conv_ffn",
    )(*args)


def _sconv_kernel(x_ref, g_ref, win_ref, cw_ref, wout_ref, out_ref, cbuf, tail_ref, *, width):
    tm, d = x_ref.shape

    @pl.when(pl.program_id(1) == 0)
    def _():
        tail_ref[...] = jnp.zeros(tail_ref.shape, F32)

    x = x_ref[...]
    hn = _rmsnorm(x, g_ref[...]).astype(BF16)
    b_gate = jnp.dot(hn, win_ref[:, 0:d], preferred_element_type=F32)
    c_gate = jnp.dot(hn, win_ref[:, d:2 * d], preferred_element_type=F32)
    xv = jnp.dot(hn, win_ref[:, 2 * d:3 * d], preferred_element_type=F32)
    y = b_gate * _causal_conv_rows(cbuf, tail_ref, c_gate * xv, cw_ref[...], width, tm)
    out_ref[...] = x + jnp.dot(y.astype(BF16), wout_ref[...], preferred_element_type=F32)


def short_conv_layer(x, g, w_in, conv_w, w_out, tm):
    b, s, d = x.shape
    row = pl.BlockSpec((None, tm, d), lambda bi, i: (bi, i, 0))
    return pl.pallas_call(
        functools.partial(_sconv_kernel, width=conv_w.shape[0]),
        grid=(b, s // tm),
        in_specs=[row, _resident((1, d)), _resident(w_in.shape), _resident(conv_w.shape),
                  _resident(w_out.shape)],
        out_specs=row,
        out_shape=jax.ShapeDtypeStruct((b, s, d), F32),
        scratch_shapes=[pltpu.VMEM((tm + SUBLANES, d), F32), pltpu.VMEM((SUBLANES, d), F32)],
        compiler_params=_params("parallel", "arbitrary"),
        name="short_conv",
    )(x, g.reshape(1, d), w_in, conv_w, w_out)


def _moba_proj_kernel(x_ref, g_ref, w_ref, q_ref, k_ref, vt_ref, km_ref,
                      *, blk, q_scale, pad_tiles):
    step = pl.program_id(1)

    @pl.when(step < pad_tiles)
    def _():
        k_ref[...] = jnp.zeros(k_ref.shape, BF16)
        vt_ref[...] = jnp.zeros(vt_ref.shape, BF16)

    @pl.when(step >= pad_tiles)
    def _():
        tm, d = x_ref.shape
        n_heads, rows, _ = vt_ref.shape
        hd = d // n_heads
        hn = _rmsnorm(x_ref[...], g_ref[...]).astype(BF16)
        y = jnp.dot(hn, w_ref[...], preferred_element_type=F32)
        q_ref[...] = (y[:, 0:d] * q_scale).astype(BF16)
        k_ref[...] = y[:, d:2 * d].astype(BF16)
        vt_ref[:, 0:hd, :] = y[:, 2 * d:3 * d].T.reshape(n_heads, hd, tm).astype(BF16)
        vt_ref[:, hd:rows, :] = jnp.ones((n_heads, rows - hd, tm), BF16)
        km = jnp.mean(y[:, d:2 * d].reshape(tm // blk, blk, d), axis=1)
        km_ref[...] = km[:, None, :]


def moba_proj(x, g, w, n_heads, tm):
    b, s, d = x.shape
    blk = MOBA_BLOCK
    hd = d // n_heads
    rows = hd + BF16_SUBLANES
    pad = MOBA_GROUP * blk
    pad_tiles = pad // tm
    assert pad_tiles * tm == pad

    def real(i):
        return jnp.maximum(i - pad_tiles, 0)

    return pl.pallas_call(
        functools.partial(_moba_proj_kernel, blk=blk, q_scale=hd ** -0.5 * LOG2E,
                          pad_tiles=pad_tiles),
        grid=(b, s // tm + pad_tiles),
        in_specs=[pl.BlockSpec((None, tm, d), lambda bi, i: (bi, real(i), 0)),
                  _resident((1, d)), _resident(w.shape)],
        out_specs=[pl.BlockSpec((None, tm, d), lambda bi, i: (bi, real(i), 0)),
                   pl.BlockSpec((None, tm, d), lambda bi, i: (bi, i, 0)),
                   pl.BlockSpec((None, n_heads, rows, tm), lambda bi, i: (bi, 0, 0, i)),
                   pl.BlockSpec((None, tm // blk, 1, d), lambda bi, i: (bi, real(i), 0, 0))],
        out_shape=[jax.ShapeDtypeStruct((b, s, d), BF16),
                   jax.ShapeDtypeStruct((b, s + pad, d), BF16),
                   jax.ShapeDtypeStruct((b, n_heads, rows, s + pad), BF16),
                   jax.ShapeDtypeStruct((b, s // blk, 1, d), F32)],
        compiler_params=_params("parallel", "arbitrary"),
        name="moba_proj",
    )(x, g.reshape(1, d), w)


def _split3_bf16(x):
    hi = x.astype(BF16)
    r = x - hi.astype(F32)
    mid = r.astype(BF16)
    return hi, mid, (r - mid.astype(F32)).astype(BF16)


def _moba_kernel(slopes_ref, q_ref, k_ref, vt_ref, km_ref, kaug_ref, o_ref, m_ref, acc_ref,
                 u0_ref, u1_ref, top0_ref, top1_ref, *, blk, n_top, group):
    qi = pl.program_id(2)
    tq = q_ref.shape[0]
    n_heads = vt_ref.shape[0]
    hd = q_ref.shape[1] // n_heads
    n_blk = km_ref.shape[0]
    nt = (((1,), (1,)), ((), ()))
    n_groups = (qi + group) // group
    u_refs = (u0_ref, u1_ref)
    top_refs = (top0_ref, top1_ref)
    heads = range(n_heads)
    cols = [slice(a * hd, (a + 1) * hd) for a in heads]
    slope2 = [slopes_ref[pl.program_id(1) * n_heads + a] * LOG2E for a in heads]
    qs = [q_ref[:, cols[a]] for a in heads]

    kk = lax.broadcasted_iota(jnp.int32, (blk, tq), 0)
    qq = lax.broadcasted_iota(jnp.int32, (blk, tq), 1)
    causal = kk <= qq

    lane = lax.broadcasted_iota(jnp.int32, (tq, LANES), 1)
    q_pos = lax.broadcasted_iota(jnp.int32, (tq, LANES), 0).astype(F32)
    q_aug = []
    for a in heads:
        parts = (_split3_bf16(jnp.full((tq, LANES), slope2[a], F32))
                 + _split3_bf16(-q_pos * slope2[a]))
        extra = jnp.zeros((tq, LANES), F32)
        for e, part in enumerate(parts):
            extra = jnp.where(lane == e, part.astype(F32), extra)
        q_aug.append(jnp.concatenate([qs[a], extra.astype(BF16)], axis=1))

    def window_start(g):
        return pl.multiple_of((qi - g * group + 1) * blk, blk)

    def score_group(g, slot, first=False, only=None):
        keys = pl.ds(window_start(g), group * blk)
        for a in (heads if only is None else only):
            lhs = jnp.concatenate([k_ref[keys, cols[a]], kaug_ref[...]], axis=1)
            s = lax.dot_general(lhs, q_aug[a], nt, preferred_element_type=F32)
            for t in range(group):
                u = s[t * blk:(t + 1) * blk]
                if first and t == group - 1:
                    u = jnp.where(causal, u, NEG_INF)
                u_refs[slot][a, t] = u
                top_refs[slot][a, t] = jnp.max(u, axis=0, keepdims=True)

    score_group(0, 0, first=True)
    for a in heads:
        m_ref[a] = jnp.full((1, tq), NEG_INF, F32)
        acc_ref[a] = jnp.zeros(acc_ref.shape[1:], F32)

    blk_ids = lax.broadcasted_iota(jnp.int32, (n_blk, tq), 0)
    sels = []
    for a in heads:
        gate = lax.dot_general(km_ref[:, cols[a]].astype(BF16), qs[a], nt,
                               preferred_element_type=F32)
        gate = jnp.where(blk_ids < qi, gate, NEG_INF)
        picks = []
        for _ in range(n_top):
            best = jnp.max(gate, axis=0, keepdims=True)
            idx = jnp.min(jnp.where(gate == best, blk_ids, n_blk), axis=0, keepdims=True)
            picks.append(jnp.where(idx < qi, idx, n_blk))
            gate = jnp.where(blk_ids == idx, -jnp.inf, gate)
        sels.append(picks)

    def consume_group(g, slot, only=None):
        keys = pl.ds(window_start(g), group * blk)
        for a in (heads if only is None else only):
            m_old = m_ref[a]
            m_new = m_old
            hits, offs = [], []
            for t in range(group):
                j = qi - g * group - (group - 1) + t
                hit = (sels[a][0] == j) | (j == qi)
                for sel in sels[a][1:]:
                    hit = hit | (sel == j)
                off = -slope2[a] * ((qi - j) * blk).astype(F32)
                m_new = jnp.maximum(m_new,
                                    jnp.where(hit, top_refs[slot][a, t] + off, NEG_INF))
                hits.append(hit); offs.append(off)
            ps = []
            for t in range(group):
                shift = jnp.where(hits[t], m_new - offs[t], -NEG_INF)
                ps.append(jnp.exp2((u_refs[slot][a, t] - shift).astype(BF16)))
            acc_ref[a] = (jnp.exp2(m_old - m_new) * acc_ref[a]
                          + jnp.dot(vt_ref[a, :, keys], jnp.concatenate(ps, axis=0),
                                    preferred_element_type=F32))
            m_ref[a] = m_new

    def group_body(g, carry):
        for slot in range(2):
            @pl.when(g % 2 == slot)
            def _():
                for a in heads:
                    score_group(g + 1, 1 - slot, only=[a])
                    if a >= 1:
                        consume_group(g, slot, only=[a - 1])
                consume_group(g, slot, only=[n_heads - 1])
        return carry

    lax.fori_loop(0, n_groups - 1, group_body, 0)
    for slot in range(2):
        @pl.when((n_groups - 1) % 2 == slot)
        def _():
            consume_group(n_groups - 1, slot)

    for a in heads:
        acc = acc_ref[a]
        o_ref[:, cols[a]] = (acc[0:hd] / acc[hd:hd + 1]).T.astype(o_ref.dtype)


def moba_attention(q, k, vt, k_mean, n_heads):
    b, s, d = q.shape
    hd = d // n_heads
    rows = vt.shape[2]
    blk = MOBA_BLOCK
    s_pad = k.shape[1]
    assert s_pad == s + MOBA_GROUP * blk
    n_blk = s // blk
    n_top = min(MOBA_TOPK, n_blk)
    slopes = jnp.exp2(-8.0 * jnp.arange(1, n_heads + 1, dtype=F32) / n_heads)
    key_pos = (jnp.arange(MOBA_GROUP * blk, dtype=jnp.int32) % blk).astype(F32)[:, None]
    lane = jnp.arange(LANES, dtype=jnp.int32)[None, :]
    key_aug = jnp.where(lane < 3, key_pos, jnp.where(lane < 6, 1.0, 0.0)).astype(BF16)
    hps = MOBA_HEADS_PER_STEP
    wide = hps * hd
    once = pl.Buffered(1)
    grid_spec = pltpu.PrefetchScalarGridSpec(
        num_scalar_prefetch=1,
        grid=(b, n_heads // hps, n_blk),
        in_specs=[
            pl.BlockSpec((None, blk, wide), lambda bi, h, i, *_: (bi, i, h)),
            pl.BlockSpec((None, s_pad, wide), lambda bi, h, i, *_: (bi, 0, h),
                         pipeline_mode=once),
            pl.BlockSpec((None, hps, rows, s_pad), lambda bi, h, i, *_: (bi, h, 0, 0),
                         pipeline_mode=once),
            pl.BlockSpec((None, n_blk, wide), lambda bi, h, i, *_: (bi, 0, h)),
            pl.BlockSpec(key_aug.shape, lambda bi, h, i, *_: (0, 0), pipeline_mode=once),
        ],
        out_specs=pl.BlockSpec((None, blk, wide), lambda bi, h, i, *_: (bi, i, h)),
        scratch_shapes=[pltpu.VMEM((hps, 1, blk), F32),
                        pltpu.VMEM((hps, rows, blk), F32),
                        pltpu.VMEM((hps, MOBA_GROUP, blk, blk), F32),
                        pltpu.VMEM((hps, MOBA_GROUP, blk, blk), F32),
                        pltpu.VMEM((hps, MOBA_GROUP, 1, blk), F32),
                        pltpu.VMEM((hps, MOBA_GROUP, 1, blk), F32)],
    )
    return pl.pallas_call(
        functools.partial(_moba_kernel, blk=blk, n_top=n_top, group=MOBA_GROUP),
        grid_spec=grid_spec,
        out_shape=jax.ShapeDtypeStruct((b, s, d), BF16),
        compiler_params=_params("parallel", "parallel", "arbitrary"),
        name="moba_attention",
    )(slopes, q, k, vt, k_mean, key_aug)


def _dot_hi(a, b):
    return jnp.dot(a, b, preferred_element_type=F32, precision=lax.Precision.HIGHEST)


def _split_bf16(a):
    hi = a.astype(BF16)
    return hi, (a - hi.astype(F32)).astype(BF16)


def _dot_split(a, b):
    a_hi, a_lo = _split_bf16(a)
    b_hi, b_lo = _split_bf16(b)
    return jnp.dot(jnp.concatenate([a_hi, a_hi, a_lo], axis=1),
                   jnp.concatenate([b_hi, b_lo, b_hi], axis=0), preferred_element_type=F32)


def _gdn_gates_kernel(sm_ref, alog_ref, dt_ref, o_ref, *, cs, n_heads):
    small = sm_ref[...]
    tm = small.shape[0]
    sp = small + dt_ref[...]
    softplus = jnp.maximum(sp, 0.0) + jnp.log1p(jnp.exp(-jnp.abs(sp)))
    g = -jnp.exp(alog_ref[...]) * softplus
    ii = lax.broadcasted_iota(jnp.int32, (tm, tm), 0)
    jj = lax.broadcasted_iota(jnp.int32, (tm, tm), 1)
    same_chunk_prefix = ((ii // cs) == (jj // cs)) & (ii >= jj)
    gc = _dot_hi(same_chunk_prefix.astype(F32), g)
    lane = lax.broadcasted_iota(jnp.int32, small.shape, 1)
    o_ref[...] = jnp.where(lane < n_heads, jax.nn.sigmoid(small), gc)


def gdn_gates(raw, a_log, dt_bias, n_heads, tm):
    t = raw.shape[0]
    pad = (n_heads, LANES - 2 * n_heads)
    alog_vec = jnp.pad(a_log.astype(F32), pad).reshape(1, LANES)
    dt_vec = jnp.pad(dt_bias.astype(F32), pad).reshape(1, LANES)
    return pl.pallas_call(
        functools.partial(_gdn_gates_kernel, cs=GDN_CHUNK, n_heads=n_heads),
        grid=(t // tm,),
        in_specs=[pl.BlockSpec((tm, LANES), lambda i: (i, 0)),
                  _resident((1, LANES)), _resident((1, LANES))],
        out_specs=pl.BlockSpec((tm, LANES), lambda i: (i, 0)),
        out_shape=jax.ShapeDtypeStruct((t, LANES), F32),
        compiler_params=_params("parallel"),
        name="gdn_gates",
    )(raw, alog_vec, dt_vec)


def _gdn_kernel(q_ref, k_ref, v_ref, z_ref, gate_ref, gcr_ref, cwq_ref, cwk_ref, cwv_ref,
                nw_ref, o_ref, qbuf, kbuf, vbuf, tail_ref, state_ref, *, cs, width, n_heads):
    ts = q_ref.shape[0]
    hps = state_ref.shape[0]
    dh = q_ref.shape[1] // hps
    nc = ts // cs
    heads = range(hps)
    cols = [slice(a * dh, (a + 1) * dh) for a in heads]
    chunks = [slice(c * cs, (c + 1) * cs) for c in range(nc)]
    pairs = [(a, c) for a in heads for c in range(nc)]
    head0 = pl.program_id(1) * hps

    @pl.when(pl.program_id(2) == 0)
    def _():
        tail_ref[...] = jnp.zeros(tail_ref.shape, F32)
        state_ref[...] = jnp.zeros(state_ref.shape, F32)

    def conv_silu(buf, n, ref, cw_ref, a):
        return _silu(_causal_conv_rows(buf.at[a], tail_ref.at[n, a],
                                       ref[:, cols[a]].astype(F32), cw_ref[:, cols[a]],
                                       width, ts))

    gates = gate_ref[...]
    lane = lax.broadcasted_iota(jnp.int32, gates.shape, 1)

    ii = lax.broadcasted_iota(jnp.int32, (cs, cs), 0)
    jj = lax.broadcasted_iota(jnp.int32, (cs, cs), 1)
    incl = ii >= jj
    strict = ii > jj
    eye = (ii == jj).astype(F32)
    nt = (((1,), (1,)), ((), ()))

    ks, gcs, k16s, kbq16s, vks, q_decs = [], [], [], [], [], []
    for a in heads:
        q = conv_silu(qbuf, 0, q_ref, cwq_ref, a)
        k = conv_silu(kbuf, 1, k_ref, cwk_ref, a)
        v = conv_silu(vbuf, 2, v_ref, cwv_ref, a)
        q = q * lax.rsqrt(jnp.sum(q * q, axis=-1, keepdims=True) + NORM_EPS) * (dh ** -0.5)
        k = k * lax.rsqrt(jnp.sum(k * k, axis=-1, keepdims=True) + NORM_EPS)
        beta = jnp.sum(jnp.where(lane == head0 + a, gates, 0.0), axis=1, keepdims=True)
        gc = jnp.sum(jnp.where(lane == n_heads + head0 + a, gates, 0.0), axis=1,
                     keepdims=True)
        e_gc = jnp.exp(gc)
        k_beta = k * beta
        ks.append(k); gcs.append(gc)
        vks.append(jnp.concatenate([(v * beta).astype(BF16),
                                    (k_beta * e_gc).astype(BF16)], axis=1))
        q_decs.append((q * e_gc).astype(BF16))
        kbq16s.append((k_beta.astype(BF16), q.astype(BF16)))
        k16s.append(k.astype(BF16))

    negs, attns = {}, {}
    for a, c in pairs:
        sl = chunks[c]
        diff = gcs[a][sl] - gcr_ref[a, c:c + 1, :]
        decay = jnp.where(incl, jnp.exp(jnp.where(incl, diff, 0.0)), 0.0)
        k_beta16, q16 = kbq16s[a]
        both = lax.dot_general(jnp.concatenate([k_beta16[sl], q16[sl]], axis=0), k16s[a][sl],
                               nt, preferred_element_type=F32)
        negs[a, c] = jnp.where(strict, -(both[0:cs] * decay), 0.0)
        attns[a, c] = (both[cs:2 * cs] * decay).astype(BF16)
    powers = {p: _dot_split(negs[p], negs[p]) for p in pairs}
    t_mats = {p: eye + negs[p] for p in pairs}
    span = 4
    while span <= cs:
        last = span == cs
        prods = {p: _dot_split(t_mats[p] if last else
                               jnp.concatenate([t_mats[p], powers[p]], axis=0), powers[p])
                 for p in pairs}
        t_mats = {p: t_mats[p] + prods[p][0:cs] for p in pairs}
        if not last:
            powers = {p: prods[p][cs:2 * cs] for p in pairs}
        span *= 2

    us, uws, kus, stacks, sds = {}, {}, {}, {}, {}
    for a, c in pairs:
        uws[a, c] = jnp.dot(t_mats[a, c].astype(BF16), vks[a][chunks[c]],
                            preferred_element_type=F32)
    for a, c in pairs:
        sl = chunks[c]
        gc = gcs[a][sl]
        g_last = gc[cs - 1:cs, :]
        kd_t = (ks[a][sl] * jnp.exp(g_last - gc)).T.astype(BF16)
        uw16 = uws[a, c].astype(BF16)
        kuw = jnp.dot(kd_t, uw16, preferred_element_type=F32)
        us[a, c] = uws[a, c][:, 0:dh]
        kus[a, c] = kuw[:, 0:dh]
        stacks[a, c] = jnp.concatenate([kuw[:, dh:2 * dh].astype(BF16), uw16[:, dh:2 * dh],
                                        q_decs[a][sl]], axis=0)
        sds[a, c] = jnp.exp(g_last)

    def emit_output(c, a, inter, v_new):
        sl = chunks[c]
        o = inter + jnp.dot(attns[a, c], v_new.astype(BF16), preferred_element_type=F32)
        o = o * lax.rsqrt(jnp.mean(o * o, axis=-1, keepdims=True) + NORM_EPS) * nw_ref[...]
        o_ref[sl, cols[a]] = (o * _silu(z_ref[sl, cols[a]].astype(F32))).astype(o_ref.dtype)

    states = [state_ref[a] for a in heads]
    pending = None
    for c in range(nc):
        prods = [jnp.dot(stacks[a, c], states[a].astype(BF16), preferred_element_type=F32)
                 for a in heads]
        if pending is not None:
            for a in heads:
                emit_output(c - 1, a, *pending[a])
        pending = []
        for a in heads:
            prod = prods[a]
            states[a] = states[a] * sds[a, c] - prod[0:dh] + kus[a, c]
            pending.append((prod[dh + cs:dh + 2 * cs], us[a, c] - prod[dh:dh + cs]))
    for a in heads:
        emit_output(nc - 1, a, *pending[a])
        state_ref[a] = states[a]


def gdn_core(proj, gates, conv_w, norm_w, n_heads, ts):
    b, s, _ = proj.shape
    dh = norm_w.shape[0]
    d = n_heads * dh
    width = conv_w.shape[0]
    cs = GDN_CHUNK
    gc_rows = gates[:, :, n_heads:2 * n_heads].reshape(b, s // cs, cs, n_heads)
    gc_rows = gc_rows.transpose(0, 3, 1, 2)

    hps = GDN_HEADS_PER_STEP
    wide = hps * dh
    groups = n_heads // hps

    def col(section):
        return pl.BlockSpec((None, ts, wide), lambda bi, h, i: (bi, i, section * groups + h))

    def cw(section):
        return pl.BlockSpec((width, wide), lambda bi, h, i: (0, section * groups + h))

    return pl.pallas_call(
        functools.partial(_gdn_kernel, cs=GDN_CHUNK, width=width, n_heads=n_heads),
        grid=(b, groups, s // ts),
        in_specs=[col(0), col(1), col(2), col(3),
                  pl.BlockSpec((None, ts, LANES), lambda bi, h, i: (bi, i, 0)),
                  pl.BlockSpec((None, hps, ts // cs, cs), lambda bi, h, i: (bi, h, i, 0)),
                  cw(0), cw(1), cw(2), _resident((1, dh))],
        out_specs=pl.BlockSpec((None, ts, wide), lambda bi, h, i: (bi, i, h)),
        out_shape=jax.ShapeDtypeStruct((b, s, d), BF16),
        scratch_shapes=[pltpu.VMEM((hps, ts + SUBLANES, dh), F32)] * 3
                       + [pltpu.VMEM((3, hps, SUBLANES, dh), F32),
                          pltpu.VMEM((hps, dh, dh), F32)],
        compiler_params=_params("parallel", "parallel", "arbitrary"),
        name="gdn_core",
    )(proj, proj, proj, proj, gates, gc_rows, conv_w, conv_w, conv_w,
      norm_w.astype(F32).reshape(1, dh))


def _row_tile(n, target):
    t = min(n, target)
    while n % t:
        t //= 2
    return t


def kernel(x, mix_norm, ffn_norm, final_norm, moba_w_qkv, moba_w_o, sconv_w_in, sconv_conv,
           sconv_w_out, gdn_w_in, gdn_conv, gdn_a_log, gdn_dt_bias, gdn_norm, gdn_w_o,
           ffn_w_up, ffn_conv, ffn_w_down):
    bsz, seq, d = x.shape
    depth = mix_norm.shape[0]
    t = bsz * seq
    tm = _row_tile(seq, ROW_TILE)
    ffn_tm = _row_tile(seq, FFN_ROW_TILE)
    for i in range(depth):
        kind, j = i % N_MIXERS, i // N_MIXERS
        mix = None
        if kind == 0:
            q, k, vt, k_mean = moba_proj(x, mix_norm[i], moba_w_qkv[j].astype(BF16),
                                         MOBA_HEADS, tm)
            o = moba_attention(q, k, vt, k_mean.reshape(bsz, seq // MOBA_BLOCK, d),
                               MOBA_HEADS)
            mix = (o, moba_w_o[j].astype(BF16))
        elif kind == 1:
            x = short_conv_layer(x, mix_norm[i], sconv_w_in[j].astype(BF16), sconv_conv[j],
                                 sconv_w_out[j].astype(BF16), tm)
        else:
            n_in = gdn_w_in.shape[-1]
            w_in = jnp.pad(gdn_w_in[j], ((0, 0), (0, 4 * d + LANES - n_in))).astype(BF16)
            proj, raw_gates = gdn_proj(x.reshape(t, d), mix_norm[i], w_in, tm)
            gates = gdn_gates(raw_gates, gdn_a_log[j], gdn_dt_bias[j], GDN_HEADS, tm)
            o = gdn_core(proj.reshape(bsz, seq, 4 * d), gates.reshape(bsz, seq, LANES),
                         gdn_conv[j], gdn_norm[j], GDN_HEADS, _row_tile(seq, GDN_ROW_TILE))
            mix = (o, gdn_w_o[j].astype(BF16))
        x = conv_ffn_layer(x, ffn_norm[i], ffn_w_up[i].astype(BF16), ffn_conv[i],
                           ffn_w_down[i].astype(BF16),
                           final_norm if i == depth - 1 else None, ffn_tm, FFN_COL_TILE, mix)
    return x
```
